```python
import jax, jax.numpy as jnp
from jax import lax
import numpy as np

D_MODEL = 1024
BATCH = 4
SEQ = 8192
DEPTH = 2

GRID_W = 64
CTX_LEN = 256
CHUNK = 64
EPS = 1e-6
ROPE_BASE = 10000.0

RET_HEADS = 4
RET_DK = 64
RET_DV = 128
MLSTM_HEADS = 4
MLSTM_DK = 64
MLSTM_DV = 128
HGRN_HEADS = 4
HGRN_DK = 128
HGRN_DV = 128
RWKV_HEADS = 8
RWKV_HD = 64
RWKV_W_RANK = 64
RWKV_A_RANK = 64
RWKV_G_RANK = 128
N_EXPERTS = 32
TOP_K = 4
D_FF = D_MODEL
SWIGLU_LIMIT = 7.0
SWIGLU_ALPHA = 1.702
MOE_BLOCK = 128

RET_QK = RET_HEADS * RET_DK
RET_W = RET_HEADS * RET_DV
ML_QK = MLSTM_HEADS * MLSTM_DK
ML_W = MLSTM_HEADS * MLSTM_DV
HG_K = HGRN_HEADS * HGRN_DK
HG_W = HGRN_HEADS * HGRN_DV
RW_W = RWKV_HEADS * RWKV_HD
AB_SPLITS = (RET_QK, RET_QK, RET_W, RET_W, 2 * ML_QK, ML_W, ML_W, 4 * MLSTM_HEADS)
RWKV_SPLITS = (RW_W, RW_W, RW_W, 2 * RWKV_W_RANK, 2 * RWKV_A_RANK, RWKV_G_RANK)
RWKV_IN = sum(RWKV_SPLITS)
CD_SPLITS = (HG_K, 2 * HG_K, HG_W, HG_W, RWKV_IN)
AB_IN = sum(AB_SPLITS)
CD_IN = sum(CD_SPLITS)
MIX_W_AB = RET_W + ML_W
MIX_W_CD = HG_W + RW_W
N_EVEN = (DEPTH + 1) // 2
N_ODD = DEPTH // 2
F32 = jnp.float32

kernel_name = 'hybrid_retnet_mlstm_hgrn2_rwkv7_moe_dit'


def _split(z, sizes):
    return jnp.split(z, [int(s) for s in np.cumsum(sizes)[:-1]], axis=-1)


def rms_norm(x, g):
    xf = x.astype(F32)
    y = xf * lax.rsqrt(jnp.mean(xf * xf, axis=-1, keepdims=True) + EPS)
    return (y * g.astype(F32)).astype(x.dtype)


def head_rms_norm(o, g):
    return o * lax.rsqrt(jnp.mean(o * o, axis=-1, keepdims=True) + EPS) * g.astype(F32)


def modulate(n, shift, scale):
    return n * (1.0 + scale[:, None]) + shift[:, None]


def dwconv3(x, w):
    xp = jnp.pad(x, ((0, 0), (1, 1), (0, 0)))
    return w[0] * xp[:, :-2] + w[1] * xp[:, 1:-1] + w[2] * xp[:, 2:]


def rope_tables(pos_r, pos_c):
    nf = RET_DK // 4
    inv = ROPE_BASE ** (-jnp.arange(nf, dtype=F32) / nf)
    ang = jnp.concatenate([pos_r[:, None] * inv, pos_c[:, None] * inv], axis=-1)
    return jnp.cos(ang), jnp.sin(ang)


def apply_rope(x, cos, sin):
    half = x.shape[-1] // 2
    x1, x2 = x[..., :half], x[..., half:]
    c, s = cos[None, :, None, :], sin[None, :, None, :]
    return jnp.concatenate([x1 * c - x2 * s, x1 * s + x2 * c], axis=-1)


def to_chunks(a):
    B, T, H = a.shape[:3]
    a = a.reshape((B, T // CHUNK, CHUNK, H) + a.shape[3:])
    return jnp.moveaxis(jnp.moveaxis(a, 1, 0), 3, 2)


def from_chunks(o):
    n, B, H, L, d = o.shape
    return jnp.moveaxis(jnp.moveaxis(o, 2, 3), 0, 1).reshape(B, n * L, H, d)


def gla_chunk_scan(q, k, v, g, s0):
    tri = jnp.tril(jnp.ones((CHUNK, CHUNK), bool))

    def step(S, inp):
        qc, kc, vc, gc = inp
        b = jnp.cumsum(gc, axis=2)
        rel = jnp.where(tri[:, :, None], b[:, :, :, None, :] - b[:, :, None, :, :], -jnp.inf)
        att = jnp.einsum('bhid,bhjd,bhijd->bhij', qc, kc, jnp.exp(rel))
        o = jnp.einsum('bhid,bhdv->bhiv', qc * jnp.exp(b), S) + jnp.einsum('bhij,bhjv->bhiv', att, vc)
        b_last = b[:, :, -1]
        k_dec = kc * jnp.exp(b_last[:, :, None, :] - b)
        S_new = jnp.exp(b_last)[..., None] * S + jnp.einsum('bhjd,bhjv->bhdv', k_dec, vc)
        return S_new, o

    S, o = lax.scan(step, s0, (to_chunks(q), to_chunks(k), to_chunks(v), to_chunks(g)))
    return from_chunks(o), S


def mlstm_chunk_scan(q, k, v, log_i, log_f, state):
    tri = jnp.tril(jnp.ones((CHUNK, CHUNK), bool))

    def step(carry, inp):
        C, nv, m = carry
        qc, kc, vc, ic, fc = inp
        b = jnp.cumsum(fc, axis=-1)
        log_d = jnp.where(tri, b[..., :, None] - b[..., None, :] + ic[..., None, :], -jnp.inf)
        inter = b + m[..., None]
        m_row = jnp.maximum(inter, jnp.max(log_d, axis=-1))
        s = jnp.einsum('bhid,bhjd->bhij', qc, kc) * jnp.exp(log_d - m_row[..., None])
        w_inter = jnp.exp(inter - m_row)
        num = w_inter[..., None] * jnp.einsum('bhvd,bhid->bhiv', C, qc) + jnp.einsum('bhij,bhjv->bhiv', s, vc)
        den = w_inter * jnp.einsum('bhd,bhid->bhi', nv, qc) + jnp.sum(s, axis=-1)
        h = num / jnp.maximum(jnp.abs(den), jnp.exp(-m_row))[..., None]
        b_last = b[..., -1]
        log_w = b_last[..., None] - b + ic
        m_new = jnp.maximum(b_last + m, jnp.max(log_w, axis=-1))
        w = jnp.exp(log_w - m_new[..., None])
        decay = jnp.exp(b_last + m - m_new)
        C_new = decay[..., None, None] * C + jnp.einsum('bhj,bhjv,bhjd->bhvd', w, vc, kc)
        n_new = decay[..., None] * nv + jnp.einsum('bhj,bhjd->bhd', w, kc)
        return (C_new, n_new, m_new), h

    state, h = lax.scan(step, state, (to_chunks(q), to_chunks(k), to_chunks(v), to_chunks(log_i), to_chunks(log_f)))
    return from_chunks(h), state


def rwkv7_scan(r, w, k, v, kk, a, s0):
    def step(S, inp):
        rt, wt, kt, vt, kkt, at = inp
        sk = jnp.einsum('bhvk,bhk->bhv', S, kkt)
        S = S * wt[:, :, None, :] - sk[..., None] * (kkt * at)[:, :, None, :] + vt[..., None] * kt[:, :, None, :]
        return S, jnp.einsum('bhvk,bhk->bhv', S, rt)

    xs = tuple(jnp.moveaxis(t, 1, 0) for t in (r, w, k, v, kk, a))
    S, y = lax.scan(step, s0, xs)
    return jnp.moveaxis(y, 0, 1), S


def prefix_scan(scan_fn, ctx_args, lat_args, init, reverse):
    if reverse:
        ctx_args = tuple(jnp.flip(t, 1) for t in ctx_args)
        lat_args = tuple(jnp.flip(t, 1) for t in lat_args)
    o_c, s_c = scan_fn(*ctx_args, init)
    o_l, _ = scan_fn(*lat_args, s_c)
    if reverse:
        o_c, o_l = jnp.flip(o_c, 1), jnp.flip(o_l, 1)
    return o_c, o_l


def mixer_ab(u_c, u_l, rope, w_in, w_out, ret_decay, ret_norm, ml_conv, ml_gate_b, ml_norm, need_ctx):
    def project(u, rope_cs):
        B, T, _ = u.shape
        rq, rk, rv, rg, mqk, mv, mo, mg = _split(u @ w_in, AB_SPLITS)
        rq = rq.astype(F32).reshape(B, T, RET_HEADS, RET_DK)
        rk = rk.astype(F32).reshape(B, T, RET_HEADS, RET_DK) * RET_DK ** -0.5
        if rope_cs is not None:
            rq, rk = apply_rope(rq, *rope_cs), apply_rope(rk, *rope_cs)
        mq, mk = _split(jax.nn.silu(dwconv3(mqk, ml_conv).astype(F32)), (ML_QK, ML_QK))
        return {'rq': rq, 'rk': rk, 'rv': rv.astype(F32).reshape(B, T, RET_HEADS, RET_DV), 'rg': rg.astype(F32),
                'mq': mq.reshape(B, T, MLSTM_HEADS, MLSTM_DK) * MLSTM_DK ** -0.5,
                'mk': mk.reshape(B, T, MLSTM_HEADS, MLSTM_DK),
                'mv': mv.astype(F32).reshape(B, T, MLSTM_HEADS, MLSTM_DV), 'mo': mo.astype(F32),
                'mg': (mg.astype(F32) + ml_gate_b.astype(F32)).reshape(B, T, 2, 2, MLSTM_HEADS)}

    def ret_args(p, d):
        log_gamma = jnp.log1p(-jnp.exp(ret_decay[d].astype(F32)))[:, None]
        return (p['rq'], p['rk'], p['rv'], jnp.broadcast_to(log_gamma, p['rq'].shape))

    def mlstm_args(p, d):
        return (p['mq'], p['mk'], p['mv'], p['mg'][:, :, d, 0], jax.nn.log_sigmoid(p['mg'][:, :, d, 1]))

    pc, pl = project(u_c, None), project(u_l, rope)
    B = u_l.shape[0]
    ret_c = ret_l = ml_c = ml_l = 0.0
    for d in range(2):
        s0 = jnp.zeros((B, RET_HEADS, RET_DK, RET_DV), F32)
        oc, ol = prefix_scan(gla_chunk_scan, ret_args(pc, d), ret_args(pl, d), s0, d == 1)
        ret_c, ret_l = ret_c + oc, ret_l + ol
        st0 = (jnp.zeros((B, MLSTM_HEADS, MLSTM_DV, MLSTM_DK), F32),
               jnp.zeros((B, MLSTM_HEADS, MLSTM_DK), F32), jnp.zeros((B, MLSTM_HEADS), F32))
        oc, ol = prefix_scan(mlstm_chunk_scan, mlstm_args(pc, d), mlstm_args(pl, d), st0, d == 1)
        ml_c, ml_l = ml_c + oc, ml_l + ol

    def merge(p, ret_sum, ml_sum):
        Bm, T = ret_sum.shape[:2]
        ret = head_rms_norm(ret_sum, ret_norm).reshape(Bm, T, RET_W) * jax.nn.silu(p['rg'])
        ml = head_rms_norm(ml_sum, ml_norm).reshape(Bm, T, ML_W) * jax.nn.sigmoid(p['mo'])
        return jnp.concatenate([ret, ml], axis=-1).astype(u_l.dtype) @ w_out

    y_c = merge(pc, ret_c, ml_c) if need_ctx else None
    return y_c, merge(pl, ret_l, ml_l)


def mixer_cd(u_c, u_l, w_in, w_out, lb, hg_norm, rw_shift, rw_w0, rw_w2, rw_a0, rw_a2, rw_g2,
             rw_kk_scale, rw_k_a, rw_r_k, rw_norm, need_ctx):
    lb = lb.reshape(HGRN_HEADS, HGRN_DK)

    def project(u):
        B, T, _ = u.shape
        hq, hf, hi, hg, zr = _split(u @ w_in, CD_SPLITS)
        p = {'hq': hq.astype(F32).reshape(B, T, HGRN_HEADS, HGRN_DK),
             'hf': hf.astype(F32).reshape(B, T, 2, HGRN_HEADS, HGRN_DK),
             'hi': hi.astype(F32).reshape(B, T, HGRN_HEADS, HGRN_DV),
             'hg': hg.astype(F32)}
        r, k, v, wl, al, gl = _split(dwconv3(zr, rw_shift).astype(F32), RWKV_SPLITS)
        p['r'] = r.reshape(B, T, RWKV_HEADS, RWKV_HD)
        p['k'] = k.reshape(B, T, RWKV_HEADS, RWKV_HD)
        p['v'] = v.reshape(B, T, RWKV_HEADS, RWKV_HD)
        kk = p['k'] * rw_kk_scale.astype(F32).reshape(RWKV_HEADS, RWKV_HD)
        p['kk'] = kk * lax.rsqrt(jnp.sum(kk * kk, axis=-1, keepdims=True) + EPS)
        p['wl'], p['al'] = wl, al
        p['g'] = jax.nn.sigmoid(gl) @ rw_g2.astype(F32)
        return p

    def hgrn_args(p, d):
        f = lb + (1.0 - lb) * jax.nn.sigmoid(p['hf'][:, :, d])
        return (p['hq'], 1.0 - f, p['hi'], jnp.log(f))

    def rwkv_args(p, d):
        B, T = p['r'].shape[:2]
        wl = p['wl'][..., d * RWKV_W_RANK:(d + 1) * RWKV_W_RANK]
        al = p['al'][..., d * RWKV_A_RANK:(d + 1) * RWKV_A_RANK]
        w_log = -jax.nn.softplus(-(rw_w0[d].astype(F32) + jnp.tanh(wl) @ rw_w2[d].astype(F32))) - 0.5
        decay = jnp.exp(-jnp.exp(w_log)).reshape(B, T, RWKV_HEADS, RWKV_HD)
        a = jax.nn.sigmoid(rw_a0[d].astype(F32) + al @ rw_a2[d].astype(F32)).reshape(B, T, RWKV_HEADS, RWKV_HD)
        k_eff = p['k'] * (1.0 + (a - 1.0) * rw_k_a.astype(F32).reshape(RWKV_HEADS, RWKV_HD))
        return (p['r'], decay, k_eff, p['v'], p['kk'], a)

    pc, pl = project(u_c), project(u_l)
    B = u_l.shape[0]
    hg_c = hg_l = rw_c = rw_l = 0.0
    dir_c, dir_l = [], []
    for d in range(2):
        s0 = jnp.zeros((B, HGRN_HEADS, HGRN_DK, HGRN_DV), F32)
        oc, ol = prefix_scan(gla_chunk_scan, hgrn_args(pc, d), hgrn_args(pl, d), s0, d == 1)
        hg_c, hg_l = hg_c + oc, hg_l + ol
        ac, al_ = rwkv_args(pc, d), rwkv_args(pl, d)
        dir_c.append(ac)
        dir_l.append(al_)
        s0 = jnp.zeros((B, RWKV_HEADS, RWKV_HD, RWKV_HD), F32)
        oc, ol = prefix_scan(rwkv7_scan, ac, al_, s0, d == 1)
        rw_c, rw_l = rw_c + oc, rw_l + ol

    def merge(p, hg_sum, rw_sum, dir_args):
        Bm, T = hg_sum.shape[:2]
        bonus = sum(jnp.sum(p['r'] * da[2] * rw_r_k.astype(F32), axis=-1, keepdims=True) * p['v'] for da in dir_args)
        hg = head_rms_norm(hg_sum, hg_norm).reshape(Bm, T, HG_W) * jax.nn.sigmoid(p['hg'])
        rw = (head_rms_norm(rw_sum, rw_norm) + bonus).reshape(Bm, T, RW_W) * p['g']
        return jnp.concatenate([hg, rw], axis=-1).astype(u_l.dtype) @ w_out

    y_c = merge(pc, hg_c, rw_c, dir_c) if need_ctx else None
    return y_c, merge(pl, hg_l, rw_l, dir_l)


def hgrn_lower_bound(p, layer):
    sm = jax.nn.softmax(p.astype(F32), axis=0)
    return jnp.cumsum(sm, axis=0)[layer] - sm[0]


def moe_ffn(t, router_w, router_b, w_gu, b_gu, w_dn, b_dn):
    N, D = t.shape
    logits = (t @ router_w).astype(F32) + router_b.astype(F32)
    top_logit, top_e = lax.top_k(logits, TOP_K)
    top_w = jax.nn.softmax(top_logit, axis=-1)
    flat_e = top_e.reshape(-1)
    order = jnp.argsort(flat_e)
    e_sorted = flat_e[order]
    tok_sorted = (order // TOP_K).astype(jnp.int32)
    counts = jnp.bincount(flat_e, length=N_EXPERTS)
    padded = (counts + MOE_BLOCK - 1) // MOE_BLOCK * MOE_BLOCK
    pad_end = jnp.cumsum(padded)
    pad_start = pad_end - padded
    grp_start = jnp.cumsum(counts) - counts
    dest = pad_start[e_sorted] + jnp.arange(N * TOP_K, dtype=jnp.int32) - grp_start[e_sorted]
    n_blocks = -(-(N * TOP_K) // MOE_BLOCK) + N_EXPERTS
    n_rows = n_blocks * MOE_BLOCK
    row_tok = jnp.full((n_rows,), N, jnp.int32).at[dest].set(tok_sorted)
    row_w = jnp.zeros((n_rows,), F32).at[dest].set(top_w.reshape(-1)[order])
    blk_e = jnp.minimum(jnp.searchsorted(pad_end, jnp.arange(n_blocks, dtype=jnp.int32) * MOE_BLOCK, side='right'),
                        N_EXPERTS - 1)
    t_pad = jnp.concatenate([t, jnp.zeros((1, D), t.dtype)], axis=0)
    xb = t_pad[row_tok].reshape(n_blocks, MOE_BLOCK, D)

    def expert_block(args):
        xe, e = args
        gu = (xe @ w_gu[e] + b_gu[e]).astype(F32)
        glu, lin = jnp.split(gu, 2, axis=-1)
        glu = jnp.minimum(glu, SWIGLU_LIMIT)
        lin = jnp.clip(lin, -SWIGLU_LIMIT, SWIGLU_LIMIT)
        act = glu * jax.nn.sigmoid(SWIGLU_ALPHA * glu) * (lin + 1.0)
        return act.astype(t.dtype) @ w_dn[e] + b_dn[e]

    yb = lax.map(expert_block, (xb, blk_e))
    y = jnp.zeros((N + 1, D), F32).at[row_tok].add(yb.reshape(n_rows, D).astype(F32) * row_w[:, None])
    return y[:N].astype(t.dtype)


def setup_inputs(seed: int = 0) -> dict:
    key = jax.random.key(seed)
    ks = iter(jax.random.split(key, 40))

    def nrm(shape, scale=1.0):
        return jax.random.normal(next(ks), shape, F32) * scale

    def gain(shape):
        return 1.0 + nrm(shape, 0.05)

    D = D_MODEL
    ret_base = -(5.0 + jnp.arange(RET_HEADS, dtype=F32)) * float(np.log(2.0))
    gate_base = jnp.stack([jnp.zeros((MLSTM_HEADS,), F32), jnp.linspace(3.0, 6.0, MLSTM_HEADS, dtype=F32)])
    return {
        'x': nrm((BATCH, SEQ, D)),
        'c': nrm((BATCH, D)),
        'ctx': nrm((BATCH, CTX_LEN, D)),
        'c_ctx': nrm((D,)),
        'mod_w': nrm((DEPTH, D, 6 * D), 0.5 * D ** -0.5),
        'mod_b': nrm((DEPTH, 6 * D), 0.02),
        'norm_mix': gain((DEPTH, D)),
        'norm_ffn': gain((DEPTH, D)),
        'norm_final': gain((D,)),
        'ab_w_in': nrm((N_EVEN, D, AB_IN), D ** -0.5),
        'ab_w_out': nrm((N_EVEN, MIX_W_AB, D), MIX_W_AB ** -0.5),
        'ret_decay': ret_base + nrm((N_EVEN, 2, RET_HEADS), 0.05),
        'ret_norm': gain((N_EVEN, RET_HEADS, RET_DV)),
        'mlstm_conv': jnp.array([0.0, 1.0, 0.0], F32)[:, None] + nrm((N_EVEN, 3, 2 * ML_QK), 0.3),
        'mlstm_gate_b': (gate_base + nrm((N_EVEN, 2, 2, MLSTM_HEADS), 0.1)).reshape(N_EVEN, 4 * MLSTM_HEADS),
        'mlstm_norm': gain((N_EVEN, MLSTM_HEADS, MLSTM_DV)),
        'cd_w_in': nrm((N_ODD, D, CD_IN), D ** -0.5),
        'cd_w_out': nrm((N_ODD, MIX_W_CD, D), MIX_W_CD ** -0.5),
        'hgrn_lb': nrm((DEPTH, HG_K), 0.5),
        'hgrn_norm': gain((N_ODD, HGRN_HEADS, HGRN_DV)),
        'rwkv_shift': jnp.array([0.25, 0.5, 0.25], F32)[:, None] + nrm((N_ODD, 3, RWKV_IN), 0.05),
        'rwkv_w0': jnp.linspace(-6.0, -1.0, RW_W, dtype=F32) + nrm((N_ODD, 2, RW_W), 0.1),
        'rwkv_w2': nrm((N_ODD, 2, RWKV_W_RANK, RW_W), 0.5 * RWKV_W_RANK ** -0.5),
        'rwkv_a0': nrm((N_ODD, 2, RW_W), 0.1),
        'rwkv_a2': nrm((N_ODD, 2, RWKV_A_RANK, RW_W), RWKV_A_RANK ** -0.5),
        'rwkv_g2': nrm((N_ODD, RWKV_G_RANK, RW_W), RWKV_G_RANK ** -0.5),
        'rwkv_kk_scale': 0.85 + nrm((N_ODD, RW_W), 0.05),
        'rwkv_k_a': 1.0 + nrm((N_ODD, RW_W), 0.05),
        'rwkv_r_k': nrm((N_ODD, RWKV_HEADS, RWKV_HD), 0.1),
        'rwkv_norm': gain((N_ODD, RWKV_HEADS, RWKV_HD)),
        'router_w': nrm((DEPTH, D, N_EXPERTS), D ** -0.5),
        'router_b': nrm((DEPTH, N_EXPERTS), 0.01),
        'exp_w_gate_up': nrm((DEPTH, N_EXPERTS, D, 2 * D_FF), D ** -0.5),
        'exp_b_gate_up': nrm((DEPTH, N_EXPERTS, 2 * D_FF), 0.01),
        'exp_w_down': nrm((DEPTH, N_EXPERTS, D_FF, D), D_FF ** -0.5),
        'exp_b_down': nrm((DEPTH, N_EXPERTS, D), 0.01),
    }


def reference(x, c, ctx, c_ctx, mod_w, mod_b, norm_mix, norm_ffn, norm_final,
              ab_w_in, ab_w_out, ret_decay, ret_norm, mlstm_conv, mlstm_gate_b, mlstm_norm,
              cd_w_in, cd_w_out, hgrn_lb, hgrn_norm, rwkv_shift, rwkv_w0, rwkv_w2, rwkv_a0,
              rwkv_a2, rwkv_g2, rwkv_kk_scale, rwkv_k_a, rwkv_r_k, rwkv_norm,
              router_w, router_b, exp_w_gate_up, exp_b_gate_up, exp_w_down, exp_b_down):
    T = x.shape[1]
    rows = T // GRID_W
    pos_r = jnp.repeat(jnp.arange(rows, dtype=F32), GRID_W)
    pos_c = jnp.tile(jnp.arange(GRID_W, dtype=F32), rows)
    rope = rope_tables(pos_r, pos_c)
    cond_l = jax.nn.silu(c)
    cond_c = jax.nn.silu(c_ctx)[None, :]
    h_l, h_c = x, ctx
    for i in range(DEPTH):
        last = i == DEPTH - 1
        j = i // 2
        m_l = jnp.split(cond_l @ mod_w[i] + mod_b[i], 6, axis=-1)
        m_c = jnp.split(cond_c @ mod_w[i] + mod_b[i], 6, axis=-1)
        u_l = modulate(rms_norm(h_l, norm_mix[i]), m_l[0], m_l[1])
        u_c = modulate(rms_norm(h_c, norm_mix[i]), m_c[0], m_c[1])
        if i % 2 == 0:
            y_c, y_l = mixer_ab(u_c, u_l, rope, ab_w_in[j], ab_w_out[j], ret_decay[j], ret_norm[j],
                                mlstm_conv[j], mlstm_gate_b[j], mlstm_norm[j], not last)
        else:
            y_c, y_l = mixer_cd(u_c, u_l, cd_w_in[j], cd_w_out[j], hgrn_lower_bound(hgrn_lb, i), hgrn_norm[j],
                                rwkv_shift[j], rwkv_w0[j], rwkv_w2[j], rwkv_a0[j], rwkv_a2[j], rwkv_g2[j],
                                rwkv_kk_scale[j], rwkv_k_a[j], rwkv_r_k[j], rwkv_norm[j], not last)
        moe_args = (router_w[i], router_b[i], exp_w_gate_up[i], exp_b_gate_up[i], exp_w_down[i], exp_b_down[i])
        h_l = h_l + m_l[2][:, None] * y_l
        v_l = modulate(rms_norm(h_l, norm_ffn[i]), m_l[3], m_l[4])
        if last:
            h_l = h_l + m_l[5][:, None] * moe_ffn(v_l.reshape(-1, D_MODEL), *moe_args).reshape(v_l.shape)
        else:
            h_c = h_c + m_c[2][:, None] * y_c
            v_c = modulate(rms_norm(h_c, norm_ffn[i]), m_c[3], m_c[4])
            n_c = v_c.shape[0] * v_c.shape[1]
            f = moe_ffn(jnp.concatenate([v_c.reshape(-1, D_MODEL), v_l.reshape(-1, D_MODEL)], axis=0), *moe_args)
            h_c = h_c + m_c[5][:, None] * f[:n_c].reshape(v_c.shape)
            h_l = h_l + m_l[5][:, None] * f[n_c:].reshape(v_l.shape)
    return rms_norm(h_l, norm_final)
```

```python
import functools

import numpy as np
import jax
import jax.numpy as jnp
from jax import lax
from jax.experimental import pallas as pl
from jax.experimental.pallas import tpu as pltpu

F32 = jnp.float32
BF16 = jnp.bfloat16

CHUNK = 64
GRID_W = 64
EPS = 1e-6
ROPE_BASE = 10000.0
RET_HEADS, RET_DK, RET_DV = 4, 64, 128
ML_HEADS, ML_DK, ML_DV = 4, 64, 128
HG_HEADS, HG_DK, HG_DV = 4, 128, 128
RW_HEADS, RW_HD = 8, 64
RW_W_RANK, RW_A_RANK, RW_G_RANK = 64, 64, 128
N_EXPERTS, TOP_K = 32, 4
SWIGLU_LIMIT, SWIGLU_ALPHA = 7.0, 1.702

RET_QK, RET_W = RET_HEADS * RET_DK, RET_HEADS * RET_DV
ML_QK, ML_W = ML_HEADS * ML_DK, ML_HEADS * ML_DV
HG_K, HG_W = HG_HEADS * HG_DK, HG_HEADS * HG_DV
RW_W = RW_HEADS * RW_HD
AB_SPLITS = (RET_QK, RET_QK, RET_W, RET_W, 2 * ML_QK, ML_W, ML_W, 4 * ML_HEADS)
RWKV_SPLITS = (RW_W, RW_W, RW_W, 2 * RW_W_RANK, 2 * RW_A_RANK, RW_G_RANK)
RWKV_IN = sum(RWKV_SPLITS)
CD_SPLITS = (HG_K, 2 * HG_K, HG_W, HG_W, RWKV_IN)

RW_GROUP = 4
RW_GW = RW_GROUP * RW_HD
RW_BASE = 8
HG_LEVELS = 6

MM_ROWS = 256
MOE_ROWS = 512
VMEM_LIMIT = 56 * 1024 * 1024


def _split(z, sizes):
    return jnp.split(z, [int(s) for s in np.cumsum(sizes)[:-1]], axis=-1)


def _dot(a, b):
    return jnp.dot(a, b, preferred_element_type=F32)


def _dot_nt(a, b):
    return lax.dot_general(a, b, (((1,), (1,)), ((), ())), preferred_element_type=F32)


def _dot_tn(a, b):
    return lax.dot_general(a, b, (((0,), (0,)), ((), ())), preferred_element_type=F32)


def _mask_dot(m16, x):
    hi = x.astype(BF16)
    lo = (x - hi.astype(F32)).astype(BF16)
    return _dot(m16, hi) + _dot(m16, lo)


def _chunk_index(d, c, n_ctx_chunks, n_chunks):
    rev = jnp.where(c < n_ctx_chunks, n_ctx_chunks - 1 - c, n_chunks + n_ctx_chunks - 1 - c)
    return jnp.where(d == 0, c, rev)


def _mm_kernel(x_ref, w_ref, o_ref):
    o_ref[...] = _dot(x_ref[...], w_ref[...])


def matmul(x, w):
    n, k = x.shape
    m = w.shape[1]
    assert n % MM_ROWS == 0 and m % 128 == 0
    return pl.pallas_call(
        _mm_kernel,
        grid=(n // MM_ROWS,),
        in_specs=[pl.BlockSpec((MM_ROWS, k), lambda i: (i, 0)),
                  pl.BlockSpec((k, m), lambda i: (0, 0))],
        out_specs=pl.BlockSpec((MM_ROWS, m), lambda i: (i, 0)),
        out_shape=jax.ShapeDtypeStruct((n, m), F32),
        compiler_params=pltpu.CompilerParams(dimension_semantics=("parallel",),
                                             vmem_limit_bytes=VMEM_LIMIT),
        name="dense_proj",
    )(x, w)


def _ret_kernel(lg_ref, q_ref, k_ref, v_ref, o_ref, s_ref):
    d = pl.program_id(1)
    c = pl.program_id(2)

    @pl.when(c == 0)
    def _():
        s_ref[...] = jnp.zeros_like(s_ref)

    L = CHUNK
    ti = lax.broadcasted_iota(jnp.int32, (L, L), 0)
    si = lax.broadcasted_iota(jnp.int32, (L, L), 1)
    dist = jnp.where(d == 0, ti - si, si - ti)
    causal = dist >= 0
    distf = jnp.maximum(dist, 0).astype(F32)
    row = lax.broadcasted_iota(jnp.int32, (L, 1), 0)
    pos = jnp.where(d == 0, row, L - 1 - row).astype(F32)
    q = q_ref[0]
    k = k_ref[0]
    v = v_ref[0]
    for h in range(RET_HEADS):
        lg = lg_ref[d, h]
        qh = q[:, h * RET_DK:(h + 1) * RET_DK]
        kh = k[:, h * RET_DK:(h + 1) * RET_DK]
        vh = v[:, h * RET_DV:(h + 1) * RET_DV].astype(BF16)
        dm = jnp.where(causal, jnp.exp(lg * distf), 0.0)
        qd = (qh * jnp.exp(lg * (pos + 1.0))).astype(BF16)
        kd = (kh * jnp.exp(lg * (L - 1.0 - pos))).astype(BF16)
        att = _dot_nt(qh.astype(BF16), kh.astype(BF16)) * dm
        s = s_ref[h]
        o = _dot(qd, s.astype(BF16)) + _dot(att.astype(BF16), vh)
        o_ref[0, 0, :, h * RET_DV:(h + 1) * RET_DV] = o
        s_ref[h] = jnp.exp(lg * L) * s + _dot_tn(kd, vh)


def retention_scan(q, k, v, log_gamma, n_ctx_chunks):
    b, t, _ = q.shape
    n_chunks = t // CHUNK
    idx = functools.partial(_chunk_index, n_ctx_chunks=n_ctx_chunks, n_chunks=n_chunks)
    in_map = lambda bi, d, c, lg: (bi, idx(d, c), 0)
    return pl.pallas_call(
        _ret_kernel,
        grid_spec=pltpu.PrefetchScalarGridSpec(
            num_scalar_prefetch=1,
            grid=(b, 2, n_chunks),
            in_specs=[pl.BlockSpec((1, CHUNK, RET_QK), in_map),
                      pl.BlockSpec((1, CHUNK, RET_QK), in_map),
                      pl.BlockSpec((1, CHUNK, RET_W), in_map)],
            out_specs=pl.BlockSpec((1, 1, CHUNK, RET_W), lambda bi, d, c, lg: (d, bi, idx(d, c), 0)),
            scratch_shapes=[pltpu.VMEM((RET_HEADS, RET_DK, RET_DV), F32)]),
        out_shape=jax.ShapeDtypeStruct((2, b, t, RET_W), F32),
        compiler_params=pltpu.CompilerParams(
            dimension_semantics=("parallel", "parallel", "arbitrary"), vmem_limit_bytes=VMEM_LIMIT),
        name="retention_scan",
    )(log_gamma, q, k, v)


def _mlstm_kernel(q_ref, k_ref, v_ref, gc_ref, gr_ref, o_ref, c_ref, m_ref):
    d = pl.program_id(1)
    c = pl.program_id(2)

    @pl.when(c == 0)
    def _():
        c_ref[...] = jnp.zeros_like(c_ref)
        m_ref[...] = jnp.zeros_like(m_ref)

    L = CHUNK
    ti = lax.broadcasted_iota(jnp.int32, (L, L), 0)
    si = lax.broadcasted_iota(jnp.int32, (L, L), 1)
    dist = jnp.where(d == 0, ti - si, si - ti)
    le = dist >= 0
    le_t = dist <= 0
    lane = lax.broadcasted_iota(jnp.int32, (L, ML_DV), 1)
    ones_col = jnp.where(lane == 0, 1.0, 0.0).astype(BF16)
    q = q_ref[0]
    k = k_ref[0]
    v = v_ref[0]
    gc = gc_ref[0, 0]
    gr = gr_ref[0, 0, 0]
    H = ML_HEADS
    for h in range(H):
        i_col = gc[:, h:h + 1]
        f_col = gc[:, H + h:H + h + 1]
        i_row = gr[h:h + 1, :]
        f_row = gr[H + h:H + h + 1, :]
        b_col = jnp.sum(jnp.where(le, f_row, 0.0), axis=1, keepdims=True)
        b_row = jnp.sum(jnp.where(le_t, f_col, 0.0), axis=0, keepdims=True)
        b_last = jnp.sum(f_row, axis=1, keepdims=True)
        m = m_ref[h:h + 1, 0:1]
        log_d = jnp.where(le, b_col - b_row + i_row, -jnp.inf)
        inter = b_col + m
        m_row = jnp.maximum(inter, jnp.max(log_d, axis=1, keepdims=True))
        qh = q[:, h * ML_DK:(h + 1) * ML_DK].astype(BF16)
        kh = k[:, h * ML_DK:(h + 1) * ML_DK]
        vh = v[:, h * ML_DV:(h + 1) * ML_DV].astype(BF16)
        v_aug = jnp.concatenate([vh, ones_col], axis=1)
        s = _dot_nt(qh, kh.astype(BF16)) * jnp.exp(log_d - m_row)
        w_inter = jnp.exp(inter - m_row)
        c_aug = c_ref[h]
        num = w_inter * _dot(qh, c_aug.astype(BF16)) + _dot(s.astype(BF16), v_aug)
        den = num[:, ML_DV:ML_DV + 1]
        hh = num[:, :ML_DV] / jnp.maximum(jnp.abs(den), jnp.exp(-m_row))
        o_ref[0, 0, :, h * ML_DV:(h + 1) * ML_DV] = hh
        log_w_row = b_last - b_row + i_row
        m_new = jnp.maximum(b_last + m, jnp.max(log_w_row, axis=1, keepdims=True))
        w_col = jnp.exp(b_last - b_col + i_col - m_new)
        decay = jnp.exp(b_last + m - m_new)
        c_ref[h] = decay * c_aug + _dot_tn((kh * w_col).astype(BF16), v_aug)
        m_ref[h:h + 1, :] = jnp.broadcast_to(m_new, (1, 128))


def mlstm_scan(q, k, v, g_col, g_row, n_ctx_chunks):
    b, t, _ = q.shape
    n_chunks = t // CHUNK
    idx = functools.partial(_chunk_index, n_ctx_chunks=n_ctx_chunks, n_chunks=n_chunks)
    in_map = lambda bi, d, c: (bi, idx(d, c), 0)
    return pl.pallas_call(
        _mlstm_kernel,
        grid=(b, 2, n_chunks),
        in_specs=[pl.BlockSpec((1, CHUNK, ML_QK), in_map),
                  pl.BlockSpec((1, CHUNK, ML_QK), in_map),
                  pl.BlockSpec((1, CHUNK, ML_W), in_map),
                  pl.BlockSpec((1, 1, CHUNK, 2 * ML_HEADS), lambda bi, d, c: (d, bi, idx(d, c), 0)),
                  pl.BlockSpec((1, 1, 1, 2 * ML_HEADS, CHUNK), lambda bi, d, c: (d, bi, idx(d, c), 0, 0))],
        out_specs=pl.BlockSpec((1, 1, CHUNK, ML_W), lambda bi, d, c: (d, bi, idx(d, c), 0)),
        out_shape=jax.ShapeDtypeStruct((2, b, t, ML_W), F32),
        scratch_shapes=[pltpu.VMEM((ML_HEADS, ML_DK, 2 * ML_DV), F32), pltpu.VMEM((8, 128), F32)],
        compiler_params=pltpu.CompilerParams(
            dimension_semantics=("parallel", "parallel", "arbitrary"), vmem_limit_bytes=VMEM_LIMIT),
        name="mlstm_scan",
    )(q, k, v, g_col, g_row)


def _hgrn_masks():
    L = CHUNK
    stack = np.zeros((2, (1 + 2 * HG_LEVELS) * L, L), np.float32)
    pair = np.zeros((2, HG_LEVELS + 1, L, L), np.float32)
    for d in range(2):
        p = np.arange(L) if d == 0 else L - 1 - np.arange(L)
        pt, ps = p[:, None], p[None, :]
        stack[d, :L] = ps <= pt
        for l in range(HG_LEVELS):
            parent, half = p >> (l + 1), (p >> l) & 1
            split = parent * (2 << l) + (1 << l) - 1
            same = parent[:, None] == parent[None, :]
            e = same & (half[:, None] == 1) & (ps > split[:, None]) & (ps <= pt)
            f = same & (half[:, None] == 0) & (ps > pt) & (ps <= split[:, None])
            stack[d, (1 + 2 * l) * L:(2 + 2 * l) * L] = e
            stack[d, (2 + 2 * l) * L:(3 + 2 * l) * L] = f
            pair[d, l] = same & (half[:, None] == 1) & (half[None, :] == 0)
        pair[d, HG_LEVELS] = np.eye(L)
    return stack, pair


def _hgrn_kernel(q_ref, hf_ref, v_ref, lb_ref, stack_ref, pair_ref, o_ref, s_ref):
    c = pl.program_id(2)

    @pl.when(c == 0)
    def _():
        s_ref[...] = jnp.zeros_like(s_ref)

    L = CHUNK
    lb = lb_ref[...]
    f = lb + (1.0 - lb) * jax.nn.sigmoid(hf_ref[0])
    kk = 1.0 - f
    g = jnp.log(f)
    sums = _mask_dot(stack_ref[0], g)
    q = q_ref[0]
    v = v_ref[0]
    for h in range(HG_HEADS):
        hs = slice(h * HG_DK, (h + 1) * HG_DK)
        qh, kh, gh = q[:, hs], kk[:, hs], g[:, hs]
        vh = v[:, h * HG_DV:(h + 1) * HG_DV].astype(BF16)
        b = sums[0:L, hs]
        att = pair_ref[0, HG_LEVELS] * _dot_nt(qh.astype(BF16), kh.astype(BF16))
        for l in range(HG_LEVELS):
            e = sums[(1 + 2 * l) * L:(2 + 2 * l) * L, hs]
            fl = sums[(2 + 2 * l) * L:(3 + 2 * l) * L, hs]
            att = att + pair_ref[0, l] * _dot_nt((qh * jnp.exp(e)).astype(BF16), (kh * jnp.exp(fl)).astype(BF16))
        b_last = jnp.sum(gh, axis=0, keepdims=True)
        st = s_ref[h]
        o = _dot_nt((qh * jnp.exp(b)).astype(BF16), st.astype(BF16)) + _dot(att.astype(BF16), vh)
        o_ref[0, 0, :, h * HG_DV:(h + 1) * HG_DV] = o
        k_dec = (kh * jnp.exp(b_last - b)).astype(BF16)
        s_ref[h] = st * jnp.exp(b_last) + _dot_tn(vh, k_dec)


def hgrn_scan(q, hf, v, lb, n_ctx_chunks):
    b, t, _ = q.shape
    n_chunks = t // CHUNK
    stack, pair = _hgrn_masks()
    idx = functools.partial(_chunk_index, n_ctx_chunks=n_ctx_chunks, n_chunks=n_chunks)
    in_map = lambda bi, d, c: (bi, idx(d, c), 0)
    return pl.pallas_call(
        _hgrn_kernel,
        grid=(b, 2, n_chunks),
        in_specs=[pl.BlockSpec((1, CHUNK, HG_K), in_map),
                  pl.BlockSpec((1, CHUNK, HG_K), lambda bi, d, c: (bi, idx(d, c), d)),
                  pl.BlockSpec((1, CHUNK, HG_W), in_map),
                  pl.BlockSpec((1, HG_K), lambda bi, d, c: (0, 0)),
                  pl.BlockSpec((1,) + stack.shape[1:], lambda bi, d, c: (d, 0, 0)),
                  pl.BlockSpec((1,) + pair.shape[1:], lambda bi, d, c: (d, 0, 0, 0))],
        out_specs=pl.BlockSpec((1, 1, CHUNK, HG_W), lambda bi, d, c: (d, bi, idx(d, c), 0)),
        out_shape=jax.ShapeDtypeStruct((2, b, t, HG_W), F32),
        scratch_shapes=[pltpu.VMEM((HG_HEADS, HG_DV, HG_DK), F32)],
        compiler_params=pltpu.CompilerParams(
            dimension_semantics=("parallel", "parallel", "arbitrary"), vmem_limit_bytes=VMEM_LIMIT),
        name="hgrn_scan",
    )(q, hf, v, lb, jnp.asarray(stack, BF16), jnp.asarray(pair, F32))


def _rwkv_masks():
    L, G = CHUNK, RW_GROUP
    n = G * L
    head = np.arange(n) // L
    same = head[:, None] == head[None, :]
    bd = same.astype(np.float32)
    t = np.arange(n) % L
    strict = np.zeros((2, n, n), np.float32)
    incl = np.zeros((2, n, n), np.float32)
    cum = np.zeros((2, L, L), np.float32)
    n_lvl = int(np.log2(L // RW_BASE))
    solve = np.zeros((2, 2 + n_lvl, n, n), np.float32)
    for d in range(2):
        p = t if d == 0 else L - 1 - t
        strict[d] = same & (p[None, :] < p[:, None])
        incl[d] = same & (p[None, :] <= p[:, None])
        pc = p[:L]
        cum[d] = pc[None, :] <= pc[:, None]
        solve[d, 0] = np.eye(n)
        solve[d, 1] = same & ((p // RW_BASE)[:, None] == (p // RW_BASE)[None, :])
        for i in range(n_lvl):
            m = RW_BASE << i
            parent, half = p // (2 * m), (p // m) % 2
            solve[d, 2 + i] = same & (parent[:, None] == parent[None, :]) & (half[:, None] == 1) & (half[None, :] == 0)
    return bd, strict, incl, cum, solve


def _dot3(a, b):
    ah = a.astype(BF16)
    al = (a - ah.astype(F32)).astype(BF16)
    bh = b.astype(BF16)
    bl = (b - bh.astype(F32)).astype(BF16)
    return _dot(ah, bh) + _dot(ah, bl) + _dot(al, bh)


def _unit_lower_inverse(n_mat, solve_ref):
    eye = solve_ref[0, 0]
    nd = n_mat * solve_ref[0, 1]
    t_inv = eye - nd
    pw = nd
    for _ in range(int(np.log2(RW_BASE)) - 1):
        pw = _dot3(pw, pw)
        t_inv = t_inv + _dot3(t_inv, pw)
    for i in range(solve_ref.shape[1] - 2):
        t_inv = t_inv - _dot3(_dot3(t_inv, n_mat * solve_ref[0, 2 + i]), t_inv)
    return t_inv


def _rwkv_kernel(r_ref, k_ref, v_ref, kk_ref, a_ref, lw_ref, bd_ref, strict_ref, incl_ref, cum_ref,
                 solve_ref, o_ref, s_ref):
    c = pl.program_id(2)

    @pl.when(c == 0)
    def _():
        s_ref[...] = jnp.zeros_like(s_ref)

    G = RW_GROUP
    bd = bd_ref[...]
    strict = strict_ref[0]
    incl = incl_ref[0]
    cum = cum_ref[0]

    def expand(x):
        return (jnp.concatenate([x] * G, axis=0) * bd).astype(BF16)

    def tile(x):
        return jnp.concatenate([x] * G, axis=0).astype(BF16)

    for g in range(RW_HEADS // G):
        gs = slice(g * RW_GW, (g + 1) * RW_GW)
        r, k, v = r_ref[0, :, gs], k_ref[0, 0, :, gs], v_ref[0, :, gs]
        kk, a, lw = kk_ref[0, :, gs], a_ref[0, 0, :, gs], lw_ref[0, 0, :, gs]
        kb = kk * a
        cs = _mask_dot(cum, lw)
        c_last = jnp.sum(lw, axis=0, keepdims=True)
        p_inv = jnp.exp(-cs)
        p_to_end = jnp.exp(c_last - cs)
        kkd = expand(kk * jnp.exp(cs - lw))
        rp = expand(r * jnp.exp(cs))
        v_bd = expand(v)
        t_k = tile(k * p_inv)
        t_b = tile(kb * p_inv)
        n_mat = strict * _dot_nt(kkd, t_b)
        m_k = (strict * _dot_nt(kkd, t_k)).astype(BF16)
        m_rk = (incl * _dot_nt(rp, t_k)).astype(BF16)
        m_rb = (incl * _dot_nt(rp, t_b)).astype(BF16)
        st = s_ref[g]
        st16 = st.astype(BF16)
        u = _dot3(_unit_lower_inverse(n_mat, solve_ref), _dot_nt(kkd, st16) + _dot(m_k, v_bd))
        u16 = u.astype(BF16)
        y = _dot_nt(rp, st16) + _dot(m_rk, v_bd) - _dot(m_rb, u16)
        L = CHUNK
        o_ref[0, 0, :, gs] = y[0:L] + y[L:2 * L] + y[2 * L:3 * L] + y[3 * L:4 * L]
        s_ref[g] = (st * jnp.exp(c_last) + _dot_tn(v_bd, expand(k * p_to_end))
                    - _dot_tn(u16, expand(kb * p_to_end)))


def rwkv_scan(r, k_eff, v, kk, a, lw, n_ctx_chunks):
    b, t, _ = r.shape
    n_chunks = t // CHUNK
    bd, strict, incl, cum, solve = _rwkv_masks()
    idx = functools.partial(_chunk_index, n_ctx_chunks=n_ctx_chunks, n_chunks=n_chunks)
    shared = pl.BlockSpec((1, CHUNK, RW_W), lambda bi, d, c: (bi, idx(d, c), 0))
    per_dir = pl.BlockSpec((1, 1, CHUNK, RW_W), lambda bi, d, c: (d, bi, idx(d, c), 0))
    n = RW_GROUP * CHUNK
    return pl.pallas_call(
        _rwkv_kernel,
        grid=(b, 2, n_chunks),
        in_specs=[shared, per_dir, shared, shared, per_dir, per_dir,
                  pl.BlockSpec((n, RW_GW), lambda bi, d, c: (0, 0)),
                  pl.BlockSpec((1, n, n), lambda bi, d, c: (d, 0, 0)),
                  pl.BlockSpec((1, n, n), lambda bi, d, c: (d, 0, 0)),
                  pl.BlockSpec((1, CHUNK, CHUNK), lambda bi, d, c: (d, 0, 0)),
                  pl.BlockSpec((1,) + solve.shape[1:], lambda bi, d, c: (d, 0, 0, 0))],
        out_specs=per_dir,
        out_shape=jax.ShapeDtypeStruct((2, b, t, RW_W), F32),
        scratch_shapes=[pltpu.VMEM((RW_HEADS // RW_GROUP, RW_GW, RW_GW), F32)],
        compiler_params=pltpu.CompilerParams(
            dimension_semantics=("parallel", "parallel", "arbitrary"), vmem_limit_bytes=VMEM_LIMIT),
        name="rwkv7_scan",
    )(r, k_eff, v, kk, a, lw, jnp.asarray(bd, F32), jnp.asarray(strict, F32), jnp.asarray(incl, F32),
      jnp.asarray(cum, BF16), jnp.asarray(solve, F32))


def _moe_kernel(e_ref, x_ref, wgu_ref, bgu_ref, wdn_ref, bdn_ref, o_ref):
    d_ff = wdn_ref.shape[1]
    gu = _dot(x_ref[...], wgu_ref[0]) + bgu_ref[0]
    glu = jnp.minimum(gu[:, :d_ff], SWIGLU_LIMIT)
    lin = jnp.clip(gu[:, d_ff:], -SWIGLU_LIMIT, SWIGLU_LIMIT)
    act = glu * jax.nn.sigmoid(SWIGLU_ALPHA * glu) * (lin + 1.0)
    o_ref[...] = _dot(act.astype(BF16), wdn_ref[0]) + bdn_ref[0]


def moe_experts(xb, blk_e, w_gu, b_gu, w_dn, b_dn):
    n_rows, dm = xb.shape
    d_ff = w_dn.shape[1]
    n_blocks = n_rows // MOE_ROWS
    return pl.pallas_call(
        _moe_kernel,
        grid_spec=pltpu.PrefetchScalarGridSpec(
            num_scalar_prefetch=1,
            grid=(n_blocks,),
            in_specs=[pl.BlockSpec((MOE_ROWS, dm), lambda i, e: (i, 0)),
                      pl.BlockSpec((1, dm, 2 * d_ff), lambda i, e: (e[i], 0, 0)),
                      pl.BlockSpec((1, 1, 2 * d_ff), lambda i, e: (e[i], 0, 0)),
                      pl.BlockSpec((1, d_ff, dm), lambda i, e: (e[i], 0, 0)),
                      pl.BlockSpec((1, 1, dm), lambda i, e: (e[i], 0, 0))],
            out_specs=pl.BlockSpec((MOE_ROWS, dm), lambda i, e: (i, 0))),
        out_shape=jax.ShapeDtypeStruct((n_rows, dm), F32),
        compiler_params=pltpu.CompilerParams(dimension_semantics=("arbitrary",),
                                             vmem_limit_bytes=VMEM_LIMIT),
        name="moe_experts",
    )(blk_e, xb, w_gu, b_gu, w_dn, b_dn)


def moe_ffn(t, router_w, router_b, w_gu, b_gu, w_dn, b_dn):
    n, dm = t.shape
    logits = jnp.dot(t, router_w, precision=lax.Precision.HIGHEST) + router_b
    top_logit, top_e = lax.top_k(logits, TOP_K)
    top_w = jax.nn.softmax(top_logit, axis=-1)
    flat_e = top_e.reshape(-1)
    order = jnp.argsort(flat_e)
    e_sorted = flat_e[order]
    tok_sorted = (order // TOP_K).astype(jnp.int32)
    counts = jnp.bincount(flat_e, length=N_EXPERTS)
    padded = (counts + MOE_ROWS - 1) // MOE_ROWS * MOE_ROWS
    pad_end = jnp.cumsum(padded)
    pad_start = pad_end - padded
    grp_start = jnp.cumsum(counts) - counts
    dest = (pad_start[e_sorted] + jnp.arange(n * TOP_K, dtype=jnp.int32) - grp_start[e_sorted]).astype(jnp.int32)
    n_blocks = -(-(n * TOP_K) // MOE_ROWS) + N_EXPERTS
    n_rows = n_blocks * MOE_ROWS
    row_tok = jnp.zeros((n_rows,), jnp.int32).at[dest].set(tok_sorted)
    blk_e = jnp.minimum(jnp.searchsorted(pad_end, jnp.arange(n_blocks, dtype=jnp.int32) * MOE_ROWS, side='right'),
                        N_EXPERTS - 1).astype(jnp.int32)
    xb = t.astype(BF16)[row_tok]
    yb = moe_experts(xb, blk_e, w_gu.astype(BF16), b_gu[:, None, :], w_dn.astype(BF16), b_dn[:, None, :])
    slot = jnp.zeros((n * TOP_K,), jnp.int32).at[order].set(dest).reshape(n, TOP_K)
    y = jnp.zeros((n, dm), F32)
    for j in range(TOP_K):
        y = y + yb[slot[:, j]] * top_w[:, j:j + 1]
    return y


def rms_norm(x, g):
    return x * lax.rsqrt(jnp.mean(x * x, axis=-1, keepdims=True) + EPS) * g


def head_rms_norm(o, g):
    return o * lax.rsqrt(jnp.mean(o * o, axis=-1, keepdims=True) + EPS) * g


def dwconv3(x, w, n_ctx):
    t = x.shape[1]
    xp = jnp.pad(x, ((0, 0), (1, 1), (0, 0)))
    pos = jnp.arange(t)[None, :, None]
    has_prev = (pos != 0) & (pos != n_ctx)
    has_next = (pos != n_ctx - 1) & (pos != t - 1)
    return (w[0] * jnp.where(has_prev, xp[:, :-2], 0.0) + w[1] * x
            + w[2] * jnp.where(has_next, xp[:, 2:], 0.0))


def rope_tables(n_ctx, seq):
    rows = seq // GRID_W
    pos_r = jnp.repeat(jnp.arange(rows, dtype=F32), GRID_W)
    pos_c = jnp.tile(jnp.arange(GRID_W, dtype=F32), rows)
    nf = RET_DK // 4
    inv = ROPE_BASE ** (-jnp.arange(nf, dtype=F32) / nf)
    ang = jnp.concatenate([pos_r[:, None] * inv, pos_c[:, None] * inv], axis=-1)
    cos = jnp.concatenate([jnp.ones((n_ctx, RET_DK // 2), F32), jnp.cos(ang)], axis=0)
    sin = jnp.concatenate([jnp.zeros((n_ctx, RET_DK // 2), F32), jnp.sin(ang)], axis=0)
    return cos, sin


def apply_rope(x, cos, sin):
    half = x.shape[-1] // 2
    x1, x2 = x[..., :half], x[..., half:]
    c, s = cos[None, :, None, :], sin[None, :, None, :]
    return jnp.concatenate([x1 * c - x2 * s, x1 * s + x2 * c], axis=-1)


def project(u, w):
    b, t, dm = u.shape
    m = w.shape[1]
    m_pad = -(-m // 128) * 128
    w16 = jnp.pad(w, ((0, 0), (0, m_pad - m))).astype(BF16)
    z = matmul(u.reshape(b * t, dm).astype(BF16), w16)
    return z[:, :m].reshape(b, t, m)


def mixer_ab(u, n_ctx, rope, w_in, w_out, ret_decay, ret_norm, ml_conv, ml_gate_b, ml_norm):
    b, t, _ = u.shape
    n_ctx_chunks = n_ctx // CHUNK
    rq, rk, rv, rg, mqk, mv, mo, mg = _split(project(u, w_in), AB_SPLITS)
    rq = apply_rope(rq.reshape(b, t, RET_HEADS, RET_DK), *rope).reshape(b, t, RET_QK)
    rk = apply_rope(rk.reshape(b, t, RET_HEADS, RET_DK) * RET_DK ** -0.5, *rope).reshape(b, t, RET_QK)
    log_gamma = jnp.log1p(-jnp.exp(ret_decay))
    ret = retention_scan(rq, rk, rv, log_gamma, n_ctx_chunks)
    ret = ret[0] + ret[1]

    mq, mk = _split(jax.nn.silu(dwconv3(mqk, ml_conv, n_ctx)), (ML_QK, ML_QK))
    mq = mq * ML_DK ** -0.5
    gates = (mg + ml_gate_b).reshape(b, t, 2, 2, ML_HEADS)
    gates = jnp.stack([gates[:, :, :, 0], jax.nn.log_sigmoid(gates[:, :, :, 1])], axis=3)
    g_col = jnp.moveaxis(gates, 2, 0).reshape(2, b, t, 2 * ML_HEADS)
    g_row = jnp.swapaxes(g_col.reshape(2, b, t // CHUNK, CHUNK, 2 * ML_HEADS), 3, 4)
    ml = mlstm_scan(mq, mk, mv, g_col, g_row, n_ctx_chunks)
    ml = ml[0] + ml[1]

    ret = head_rms_norm(ret.reshape(b, t, RET_HEADS, RET_DV), ret_norm).reshape(b, t, RET_W) * jax.nn.silu(rg)
    ml = head_rms_norm(ml.reshape(b, t, ML_HEADS, ML_DV), ml_norm).reshape(b, t, ML_W) * jax.nn.sigmoid(mo)
    return project(jnp.concatenate([ret, ml], axis=-1), w_out)


def mixer_cd(u, n_ctx, w_in, w_out, lb, hg_norm, rw_shift, rw_w0, rw_w2, rw_a0, rw_a2, rw_g2,
             rw_kk_scale, rw_k_a, rw_r_k, rw_norm):
    b, t, _ = u.shape
    n_ctx_chunks = n_ctx // CHUNK
    hq, hf, hi, hg, zr = _split(project(u, w_in), CD_SPLITS)
    hgo = hgrn_scan(hq, hf, hi, lb.reshape(1, HG_K), n_ctx_chunks)
    hgo = hgo[0] + hgo[1]

    r, k, v, wl, al, gl = _split(dwconv3(zr, rw_shift, n_ctx), RWKV_SPLITS)
    kk = (k * rw_kk_scale).reshape(b, t, RW_HEADS, RW_HD)
    kk = (kk * lax.rsqrt(jnp.sum(kk * kk, axis=-1, keepdims=True) + EPS)).reshape(b, t, RW_W)
    gate = jax.nn.sigmoid(gl) @ rw_g2
    k_eff, a_all, lw_all = [], [], []
    for d in range(2):
        wl_d = wl[..., d * RW_W_RANK:(d + 1) * RW_W_RANK]
        al_d = al[..., d * RW_A_RANK:(d + 1) * RW_A_RANK]
        w_log = -jax.nn.softplus(-(rw_w0[d] + jnp.tanh(wl_d) @ rw_w2[d])) - 0.5
        lw_all.append(-jnp.exp(w_log))
        a = jax.nn.sigmoid(rw_a0[d] + al_d @ rw_a2[d])
        a_all.append(a)
        k_eff.append(k * (1.0 + (a - 1.0) * rw_k_a))
    k_eff, a_all, lw_all = jnp.stack(k_eff), jnp.stack(a_all), jnp.stack(lw_all)
    rwo = rwkv_scan(r, k_eff, v, kk, a_all, lw_all, n_ctx_chunks)
    rwo = (rwo[0] + rwo[1]).reshape(b, t, RW_HEADS, RW_HD)

    r4, v4 = r.reshape(b, t, RW_HEADS, RW_HD), v.reshape(b, t, RW_HEADS, RW_HD)
    bonus = sum(jnp.sum(r4 * k_eff[d].reshape(b, t, RW_HEADS, RW_HD) * rw_r_k, axis=-1, keepdims=True) * v4
                for d in range(2))
    hgo = head_rms_norm(hgo.reshape(b, t, HG_HEADS, HG_DV), hg_norm).reshape(b, t, HG_W) * jax.nn.sigmoid(hg)
    rwo = (head_rms_norm(rwo, rw_norm) + bonus).reshape(b, t, RW_W) * gate
    return project(jnp.concatenate([hgo, rwo], axis=-1), w_out)


def hgrn_lower_bound(p, layer):
    sm = jax.nn.softmax(p, axis=0)
    return jnp.cumsum(sm, axis=0)[layer] - sm[0]


def kernel(x, c, ctx, c_ctx, mod_w, mod_b, norm_mix, norm_ffn, norm_final, ab_w_in, ab_w_out, ret_decay, ret_norm, mlstm_conv, mlstm_gate_b, mlstm_norm, cd_w_in, cd_w_out, hgrn_lb, hgrn_norm, rwkv_shift, rwkv_w0, rwkv_w2, rwkv_a0, rwkv_a2, rwkv_g2, rwkv_kk_scale, rwkv_k_a, rwkv_r_k, rwkv_norm, router_w, router_b, exp_w_gate_up, exp_b_gate_up, exp_w_down, exp_b_down):
    bsz, seq, dm = x.shape
    n_ctx = ctx.shape[1]
    depth = mod_w.shape[0]
    t = n_ctx + seq
    rope = rope_tables(n_ctx, seq)
    cond_l = jax.nn.silu(c)
    cond_c = jax.nn.silu(c_ctx)[None, :]
    h = jnp.concatenate([ctx, x], axis=1)
    hp = lax.Precision.HIGHEST

    def per_token(m_c, m_l):
        return jnp.concatenate([jnp.broadcast_to(m_c[:, None], (bsz, n_ctx, dm)),
                                jnp.broadcast_to(m_l[:, None], (bsz, seq, dm))], axis=1)

    for i in range(depth):
        j = i // 2
        m_l = jnp.split(jnp.dot(cond_l, mod_w[i], precision=hp) + mod_b[i], 6, axis=-1)
        m_c = jnp.split(jnp.dot(cond_c, mod_w[i], precision=hp) + mod_b[i], 6, axis=-1)
        m = [per_token(a, b_) for a, b_ in zip(m_c, m_l)]
        u = rms_norm(h, norm_mix[i]) * (1.0 + m[1]) + m[0]
        if i % 2 == 0:
            y = mixer_ab(u, n_ctx, rope, ab_w_in[j], ab_w_out[j], ret_decay[j], ret_norm[j],
                         mlstm_conv[j], mlstm_gate_b[j], mlstm_norm[j])
        else:
            y = mixer_cd(u, n_ctx, cd_w_in[j], cd_w_out[j], hgrn_lower_bound(hgrn_lb, i), hgrn_norm[j],
                         rwkv_shift[j], rwkv_w0[j], rwkv_w2[j], rwkv_a0[j], rwkv_a2[j], rwkv_g2[j],
                         rwkv_kk_scale[j], rwkv_k_a[j], rwkv_r_k[j], rwkv_norm[j])
        h = h + m[2] * y
        vv = rms_norm(h, norm_ffn[i]) * (1.0 + m[4]) + m[3]
        f = moe_ffn(vv.reshape(-1, dm), router_w[i], router_b[i], exp_w_gate_up[i], exp_b_gate_up[i],
                    exp_w_down[i], exp_b_down[i]).reshape(vv.shape)
        h = h + m[5] * f
    return rms_norm(h[:, n_ctx:], norm_final)
```

```python
import functools

import numpy as np
import jax
import jax.numpy as jnp
from jax import lax
from jax.experimental import pallas as pl
from jax.experimental.pallas import tpu as pltpu

F32 = jnp.float32
BF16 = jnp.bfloat16

CHUNK = 64
GRID_W = 64
EPS = 1e-6
ROPE_BASE = 10000.0
RET_HEADS, RET_DK, RET_DV = 4, 64, 128
ML_HEADS, ML_DK, ML_DV = 4, 64, 128
HG_HEADS, HG_DK, HG_DV = 4, 128, 128
RW_HEADS, RW_HD = 8, 64
RW_W_RANK, RW_A_RANK, RW_G_RANK = 64, 64, 128
N_EXPERTS, TOP_K = 32, 4
SWIGLU_LIMIT, SWIGLU_ALPHA = 7.0, 1.702

RET_QK, RET_W = RET_HEADS * RET_DK, RET_HEADS * RET_DV
ML_QK, ML_W = ML_HEADS * ML_DK, ML_HEADS * ML_DV
HG_K, HG_W = HG_HEADS * HG_DK, HG_HEADS * HG_DV
RW_W = RW_HEADS * RW_HD
AB_SPLITS = (RET_QK, RET_QK, RET_W, RET_W, 2 * ML_QK, ML_W, ML_W, 4 * ML_HEADS)
RWKV_SPLITS = (RW_W, RW_W, RW_W, 2 * RW_W_RANK, 2 * RW_A_RANK, RW_G_RANK)
RWKV_IN = sum(RWKV_SPLITS)
CD_SPLITS = (HG_K, 2 * HG_K, HG_W, HG_W, RWKV_IN)

RW_GROUP = 4
RW_GW = RW_GROUP * RW_HD
RW_BASE = 8
HG_LEVELS = 6

MM_ROWS = 256
MOE_ROWS = 512
VMEM_LIMIT = 56 * 1024 * 1024


def _split(z, sizes):
    return jnp.split(z, [int(s) for s in np.cumsum(sizes)[:-1]], axis=-1)


def _dot(a, b):
    return jnp.dot(a, b, preferred_element_type=F32)


def _dot_nt(a, b):
    return lax.dot_general(a, b, (((1,), (1,)), ((), ())), preferred_element_type=F32)


def _dot_tn(a, b):
    return lax.dot_general(a, b, (((0,), (0,)), ((), ())), preferred_element_type=F32)


def _mask_dot(m16, x):
    hi = x.astype(BF16)
    lo = (x - hi.astype(F32)).astype(BF16)
    return _dot(m16, hi) + _dot(m16, lo)


def _chunk_index(d, c, n_ctx_chunks, n_chunks):
    rev = jnp.where(c < n_ctx_chunks, n_ctx_chunks - 1 - c, n_chunks + n_ctx_chunks - 1 - c)
    return jnp.where(d == 0, c, rev)


def _mm_kernel(x_ref, w_ref, o_ref):
    o_ref[...] = _dot(x_ref[...], w_ref[...])


def matmul(x, w):
    n, k = x.shape
    m = w.shape[1]
    assert n % MM_ROWS == 0 and m % 128 == 0
    return pl.pallas_call(
        _mm_kernel,
        grid=(n // MM_ROWS,),
        in_specs=[pl.BlockSpec((MM_ROWS, k), lambda i: (i, 0)),
                  pl.BlockSpec((k, m), lambda i: (0, 0))],
        out_specs=pl.BlockSpec((MM_ROWS, m), lambda i: (i, 0)),
        out_shape=jax.ShapeDtypeStruct((n, m), F32),
        compiler_params=pltpu.CompilerParams(dimension_semantics=("parallel",),
                                             vmem_limit_bytes=VMEM_LIMIT),
        name="dense_proj",
    )(x, w)


def _ret_kernel(lg_ref, q_ref, k_ref, v_ref, o_ref, s_ref):
    d = pl.program_id(1)
    c = pl.program_id(2)

    @pl.when(c == 0)
    def _():
        s_ref[...] = jnp.zeros_like(s_ref)

    L = CHUNK
    ti = lax.broadcasted_iota(jnp.int32, (L, L), 0)
    si = lax.broadcasted_iota(jnp.int32, (L, L), 1)
    dist = jnp.where(d == 0, ti - si, si - ti)
    causal = dist >= 0
    distf = jnp.maximum(dist, 0).astype(F32)
    row = lax.broadcasted_iota(jnp.int32, (L, 1), 0)
    pos = jnp.where(d == 0, row, L - 1 - row).astype(F32)
    q = q_ref[0]
    k = k_ref[0]
    v = v_ref[0]
    for h in range(RET_HEADS):
        lg = lg_ref[d, h]
        qh = q[:, h * RET_DK:(h + 1) * RET_DK]
        kh = k[:, h * RET_DK:(h + 1) * RET_DK]
        vh = v[:, h * RET_DV:(h + 1) * RET_DV].astype(BF16)
        dm = jnp.where(causal, jnp.exp(lg * distf), 0.0)
        qd = (qh * jnp.exp(lg * (pos + 1.0))).astype(BF16)
        kd = (kh * jnp.exp(lg * (L - 1.0 - pos))).astype(BF16)
        att = _dot_nt(qh.astype(BF16), kh.astype(BF16)) * dm
        s = s_ref[h]
        o = _dot(qd, s.astype(BF16)) + _dot(att.astype(BF16), vh)
        o_ref[0, 0, :, h * RET_DV:(h + 1) * RET_DV] = o
        s_ref[h] = jnp.exp(lg * L) * s + _dot_tn(kd, vh)


def retention_scan(q, k, v, log_gamma, n_ctx_chunks):
    b, t, _ = q.shape
    n_chunks = t // CHUNK
    idx = functools.partial(_chunk_index, n_ctx_chunks=n_ctx_chunks, n_chunks=n_chunks)
    in_map = lambda bi, d, c, lg: (bi, idx(d, c), 0)
    return pl.pallas_call(
        _ret_kernel,
        grid_spec=pltpu.PrefetchScalarGridSpec(
            num_scalar_prefetch=1,
            grid=(b, 2, n_chunks),
            in_specs=[pl.BlockSpec((1, CHUNK, RET_QK), in_map),
                      pl.BlockSpec((1, CHUNK, RET_QK), in_map),
                      pl.BlockSpec((1, CHUNK, RET_W), in_map)],
            out_specs=pl.BlockSpec((1, 1, CHUNK, RET_W), lambda bi, d, c, lg: (d, bi, idx(d, c), 0)),
            scratch_shapes=[pltpu.VMEM((RET_HEADS, RET_DK, RET_DV), F32)]),
        out_shape=jax.ShapeDtypeStruct((2, b, t, RET_W), F32),
        compiler_params=pltpu.CompilerParams(
            dimension_semantics=("parallel", "parallel", "arbitrary"), vmem_limit_bytes=VMEM_LIMIT),
        name="retention_scan",
    )(log_gamma, q, k, v)


def _mlstm_kernel(q_ref, k_ref, v_ref, gc_ref, gr_ref, o_ref, c_ref, m_ref):
    d = pl.program_id(1)
    c = pl.program_id(2)

    @pl.when(c == 0)
    def _():
        c_ref[...] = jnp.zeros_like(c_ref)
        m_ref[...] = jnp.zeros_like(m_ref)

    L = CHUNK
    ti = lax.broadcasted_iota(jnp.int32, (L, L), 0)
    si = lax.broadcasted_iota(jnp.int32, (L, L), 1)
    dist = jnp.where(d == 0, ti - si, si - ti)
    le = dist >= 0
    le_t = dist <= 0
    lane = lax.broadcasted_iota(jnp.int32, (L, ML_DV), 1)
    ones_col = jnp.where(lane == 0, 1.0, 0.0).astype(BF16)
    q = q_ref[0]
    k = k_ref[0]
    v = v_ref[0]
    gc = gc_ref[0, 0]
    gr = gr_ref[0, 0, 0]
    H = ML_HEADS
    for h in range(H):
        i_col = gc[:, h:h + 1]
        f_col = gc[:, H + h:H + h + 1]
        i_row = gr[h:h + 1, :]
        f_row = gr[H + h:H + h + 1, :]
        b_col = jnp.sum(jnp.where(le, f_row, 0.0), axis=1, keepdims=True)
        b_row = jnp.sum(jnp.where(le_t, f_col, 0.0), axis=0, keepdims=True)
        b_last = jnp.sum(f_row, axis=1, keepdims=True)
        m = m_ref[h:h + 1, 0:1]
        log_d = jnp.where(le, b_col - b_row + i_row, -jnp.inf)
        inter = b_col + m
        m_row = jnp.maximum(inter, jnp.max(log_d, axis=1, keepdims=True))
        qh = q[:, h * ML_DK:(h + 1) * ML_DK].astype(BF16)
        kh = k[:, h * ML_DK:(h + 1) * ML_DK]
        vh = v[:, h * ML_DV:(h + 1) * ML_DV].astype(BF16)
        v_aug = jnp.concatenate([vh, ones_col], axis=1)
        s = _dot_nt(qh, kh.astype(BF16)) * jnp.exp(log_d - m_row)
        w_inter = jnp.exp(inter - m_row)
        c_aug = c_ref[h]
        num = w_inter * _dot(qh, c_aug.astype(BF16)) + _dot(s.astype(BF16), v_aug)
        den = num[:, ML_DV:ML_DV + 1]
        hh = num[:, :ML_DV] / jnp.maximum(jnp.abs(den), jnp.exp(-m_row))
        o_ref[0, 0, :, h * ML_DV:(h + 1) * ML_DV] = hh
        log_w_row = b_last - b_row + i_row
        m_new = jnp.maximum(b_last + m, jnp.max(log_w_row, axis=1, keepdims=True))
        w_col = jnp.exp(b_last - b_col + i_col - m_new)
        decay = jnp.exp(b_last + m - m_new)
        c_ref[h] = decay * c_aug + _dot_tn((kh * w_col).astype(BF16), v_aug)
        m_ref[h:h + 1, :] = jnp.broadcast_to(m_new, (1, 128))


def mlstm_scan(q, k, v, g_col, g_row, n_ctx_chunks):
    b, t, _ = q.shape
    n_chunks = t // CHUNK
    idx = functools.partial(_chunk_index, n_ctx_chunks=n_ctx_chunks, n_chunks=n_chunks)
    in_map = lambda bi, d, c: (bi, idx(d, c), 0)
    return pl.pallas_call(
        _mlstm_kernel,
        grid=(b, 2, n_chunks),
        in_specs=[pl.BlockSpec((1, CHUNK, ML_QK), in_map),
                  pl.BlockSpec((1, CHUNK, ML_QK), in_map),
                  pl.BlockSpec((1, CHUNK, ML_W), in_map),
                  pl.BlockSpec((1, 1, CHUNK, 2 * ML_HEADS), lambda bi, d, c: (d, bi, idx(d, c), 0)),
                  pl.BlockSpec((1, 1, 1, 2 * ML_HEADS, CHUNK), lambda bi, d, c: (d, bi, idx(d, c), 0, 0))],
        out_specs=pl.BlockSpec((1, 1, CHUNK, ML_W), lambda bi, d, c: (d, bi, idx(d, c), 0)),
        out_shape=jax.ShapeDtypeStruct((2, b, t, ML_W), F32),
        scratch_shapes=[pltpu.VMEM((ML_HEADS, ML_DK, 2 * ML_DV), F32), pltpu.VMEM((8, 128), F32)],
        compiler_params=pltpu.CompilerParams(
            dimension_semantics=("parallel", "parallel", "arbitrary"), vmem_limit_bytes=VMEM_LIMIT),
        name="mlstm_scan",
    )(q, k, v, g_col, g_row)


def _hgrn_masks():
    L = CHUNK
    stack = np.zeros((2, (1 + 2 * HG_LEVELS) * L, L), np.float32)
    pair = np.zeros((2, HG_LEVELS + 1, L, L), np.float32)
    for d in range(2):
        p = np.arange(L) if d == 0 else L - 1 - np.arange(L)
        pt, ps = p[:, None], p[None, :]
        stack[d, :L] = ps <= pt
        for l in range(HG_LEVELS):
            parent, half = p >> (l + 1), (p >> l) & 1
            split = parent * (2 << l) + (1 << l) - 1
            same = parent[:, None] == parent[None, :]
            e = same & (half[:, None] == 1) & (ps > split[:, None]) & (ps <= pt)
            f = same & (half[:, None] == 0) & (ps > pt) & (ps <= split[:, None])
            stack[d, (1 + 2 * l) * L:(2 + 2 * l) * L] = e
            stack[d, (2 + 2 * l) * L:(3 + 2 * l) * L] = f
            pair[d, l] = same & (half[:, None] == 1) & (half[None, :] == 0)
        pair[d, HG_LEVELS] = np.eye(L)
    return stack, pair


def _hgrn_kernel(q_ref, hf_ref, v_ref, lb_ref, stack_ref, pair_ref, o_ref, s_ref):
    c = pl.program_id(2)

    @pl.when(c == 0)
    def _():
        s_ref[...] = jnp.zeros_like(s_ref)

    L = CHUNK
    lb = lb_ref[...]
    f = lb + (1.0 - lb) * jax.nn.sigmoid(hf_ref[0])
    kk = 1.0 - f
    g = jnp.log(f)
    sums = _mask_dot(stack_ref[0], g)
    q = q_ref[0]
    v = v_ref[0]
    for h in range(HG_HEADS):
        hs = slice(h * HG_DK, (h + 1) * HG_DK)
        qh, kh, gh = q[:, hs], kk[:, hs], g[:, hs]
        vh = v[:, h * HG_DV:(h + 1) * HG_DV].astype(BF16)
        b = sums[0:L, hs]
        att = pair_ref[0, HG_LEVELS] * _dot_nt(qh.astype(BF16), kh.astype(BF16))
        for l in range(HG_LEVELS):
            e = sums[(1 + 2 * l) * L:(2 + 2 * l) * L, hs]
            fl = sums[(2 + 2 * l) * L:(3 + 2 * l) * L, hs]
            att = att + pair_ref[0, l] * _dot_nt((qh * jnp.exp(e)).astype(BF16), (kh * jnp.exp(fl)).astype(BF16))
        b_last = jnp.sum(gh, axis=0, keepdims=True)
        st = s_ref[h]
        o = _dot_nt((qh * jnp.exp(b)).astype(BF16), st.astype(BF16)) + _dot(att.astype(BF16), vh)
        o_ref[0, 0, :, h * HG_DV:(h + 1) * HG_DV] = o
        k_dec = (kh * jnp.exp(b_last - b)).astype(BF16)
        s_ref[h] = st * jnp.exp(b_last) + _dot_tn(vh, k_dec)


def hgrn_scan(q, hf, v, lb, n_ctx_chunks):
    b, t, _ = q.shape
    n_chunks = t // CHUNK
    stack, pair = _hgrn_masks()
    idx = functools.partial(_chunk_index, n_ctx_chunks=n_ctx_chunks, n_chunks=n_chunks)
    in_map = lambda bi, d, c: (bi, idx(d, c), 0)
    return pl.pallas_call(
        _hgrn_kernel,
        grid=(b, 2, n_chunks),
        in_specs=[pl.BlockSpec((1, CHUNK, HG_K), in_map),
                  pl.BlockSpec((1, CHUNK, HG_K), lambda bi, d, c: (bi, idx(d, c), d)),
                  pl.BlockSpec((1, CHUNK, HG_W), in_map),
                  pl.BlockSpec((1, HG_K), lambda bi, d, c: (0, 0)),
                  pl.BlockSpec((1,) + stack.shape[1:], lambda bi, d, c: (d, 0, 0)),
                  pl.BlockSpec((1,) + pair.shape[1:], lambda bi, d, c: (d, 0, 0, 0))],
        out_specs=pl.BlockSpec((1, 1, CHUNK, HG_W), lambda bi, d, c: (d, bi, idx(d, c), 0)),
        out_shape=jax.ShapeDtypeStruct((2, b, t, HG_W), F32),
        scratch_shapes=[pltpu.VMEM((HG_HEADS, HG_DV, HG_DK), F32)],
        compiler_params=pltpu.CompilerParams(
            dimension_semantics=("parallel", "parallel", "arbitrary"), vmem_limit_bytes=VMEM_LIMIT),
        name="hgrn_scan",
    )(q, hf, v, lb, jnp.asarray(stack, BF16), jnp.asarray(pair, F32))


def _rwkv_masks():
    L, G = CHUNK, RW_GROUP
    n = G * L
    head = np.arange(n) // L
    same = head[:, None] == head[None, :]
    bd = same.astype(np.float32)
    t = np.arange(n) % L
    strict = np.zeros((2, n, n), np.float32)
    incl = np.zeros((2, n, n), np.float32)
    cum = np.zeros((2, L, L), np.float32)
    n_lvl = int(np.log2(L // RW_BASE))
    solve = np.zeros((2, 2 + n_lvl, n, n), np.float32)
    for d in range(2):
        p = t if d == 0 else L - 1 - t
        strict[d] = same & (p[None, :] < p[:, None])
        incl[d] = same & (p[None, :] <= p[:, None])
        pc = p[:L]
        cum[d] = pc[None, :] <= pc[:, None]
        solve[d, 0] = np.eye(n)
        solve[d, 1] = same & ((p // RW_BASE)[:, None] == (p // RW_BASE)[None, :])
        for i in range(n_lvl):
            m = RW_BASE << i
            parent, half = p // (2 * m), (p // m) % 2
            solve[d, 2 + i] = same & (parent[:, None] == parent[None, :]) & (half[:, None] == 1) & (half[None, :] == 0)
    return bd, strict, incl, cum, solve


def _unit_lower_inverse(n_mat, solve_ref):
    dot16 = lambda a, b: _dot(a.astype(BF16), b.astype(BF16))
    eye = solve_ref[0, 0]
    nd = n_mat * solve_ref[0, 1]
    t_inv = eye - nd
    pw = nd
    for _ in range(int(np.log2(RW_BASE)) - 1):
        pw = dot16(pw, pw)
        t_inv = t_inv + dot16(t_inv, pw)
    for i in range(solve_ref.shape[1] - 2):
        t_inv = t_inv - dot16(dot16(t_inv, n_mat * solve_ref[0, 2 + i]), t_inv)
    return t_inv


def _rwkv_kernel(r_ref, k_ref, v_ref, kk_ref, a_ref, lw_ref, bd_ref, strict_ref, incl_ref, cum_ref,
                 solve_ref, o_ref, s_ref):
    c = pl.program_id(2)

    @pl.when(c == 0)
    def _():
        s_ref[...] = jnp.zeros_like(s_ref)

    G = RW_GROUP
    bd = bd_ref[...]
    strict = strict_ref[0]
    incl = incl_ref[0]
    cum = cum_ref[0]

    def expand(x):
        return (jnp.concatenate([x] * G, axis=0) * bd).astype(BF16)

    def tile(x):
        return jnp.concatenate([x] * G, axis=0).astype(BF16)

    for g in range(RW_HEADS // G):
        gs = slice(g * RW_GW, (g + 1) * RW_GW)
        r, k, v = r_ref[0, :, gs], k_ref[0, 0, :, gs], v_ref[0, :, gs]
        kk, a, lw = kk_ref[0, :, gs], a_ref[0, 0, :, gs], lw_ref[0, 0, :, gs]
        kb = kk * a
        cs = _mask_dot(cum, lw)
        c_last = jnp.sum(lw, axis=0, keepdims=True)
        p_inv = jnp.exp(-cs)
        p_to_end = jnp.exp(c_last - cs)
        kkd = expand(kk * jnp.exp(cs - lw))
        rp = expand(r * jnp.exp(cs))
        v_bd = expand(v)
        t_k = tile(k * p_inv)
        t_b = tile(kb * p_inv)
        n_mat = strict * _dot_nt(kkd, t_b)
        m_k = (strict * _dot_nt(kkd, t_k)).astype(BF16)
        m_rk = (incl * _dot_nt(rp, t_k)).astype(BF16)
        m_rb = (incl * _dot_nt(rp, t_b)).astype(BF16)
        st = s_ref[g]
        st16 = st.astype(BF16)
        t_inv = _unit_lower_inverse(n_mat, solve_ref).astype(BF16)
        u16 = _dot(t_inv, (_dot_nt(kkd, st16) + _dot(m_k, v_bd)).astype(BF16)).astype(BF16)
        y = _dot_nt(rp, st16) + _dot(m_rk, v_bd) - _dot(m_rb, u16)
        L = CHUNK
        o_ref[0, 0, :, gs] = y[0:L] + y[L:2 * L] + y[2 * L:3 * L] + y[3 * L:4 * L]
        s_ref[g] = (st * jnp.exp(c_last) + _dot_tn(v_bd, expand(k * p_to_end))
                    - _dot_tn(u16, expand(kb * p_to_end)))


def rwkv_scan(r, k_eff, v, kk, a, lw, n_ctx_chunks):
    b, t, _ = r.shape
    n_chunks = t // CHUNK
    bd, strict, incl, cum, solve = _rwkv_masks()
    idx = functools.partial(_chunk_index, n_ctx_chunks=n_ctx_chunks, n_chunks=n_chunks)
    shared = pl.BlockSpec((1, CHUNK, RW_W), lambda bi, d, c: (bi, idx(d, c), 0))
    per_dir = pl.BlockSpec((1, 1, CHUNK, RW_W), lambda bi, d, c: (d, bi, idx(d, c), 0))
    n = RW_GROUP * CHUNK
    return pl.pallas_call(
        _rwkv_kernel,
        grid=(b, 2, n_chunks),
        in_specs=[shared, per_dir, shared, shared, per_dir, per_dir,
                  pl.BlockSpec((n, RW_GW), lambda bi, d, c: (0, 0)),
                  pl.BlockSpec((1, n, n), lambda bi, d, c: (d, 0, 0)),
                  pl.BlockSpec((1, n, n), lambda bi, d, c: (d, 0, 0)),
                  pl.BlockSpec((1, CHUNK, CHUNK), lambda bi, d, c: (d, 0, 0)),
                  pl.BlockSpec((1,) + solve.shape[1:], lambda bi, d, c: (d, 0, 0, 0))],
        out_specs=per_dir,
        out_shape=jax.ShapeDtypeStruct((2, b, t, RW_W), F32),
        scratch_shapes=[pltpu.VMEM((RW_HEADS // RW_GROUP, RW_GW, RW_GW), F32)],
        compiler_params=pltpu.CompilerParams(
            dimension_semantics=("parallel", "parallel", "arbitrary"), vmem_limit_bytes=VMEM_LIMIT),
        name="rwkv7_scan",
    )(r, k_eff, v, kk, a, lw, jnp.asarray(bd, F32), jnp.asarray(strict, F32), jnp.asarray(incl, F32),
      jnp.asarray(cum, BF16), jnp.asarray(solve, F32))


def _moe_kernel(e_ref, x_ref, wgu_ref, bgu_ref, wdn_ref, bdn_ref, o_ref, wgu16_ref, wdn16_ref):
    i = pl.program_id(0)
    new_expert = jnp.logical_or(i == 0, e_ref[i] != e_ref[jnp.maximum(i - 1, 0)])

    @pl.when(new_expert)
    def _():
        wgu16_ref[...] = wgu_ref[0].astype(BF16)
        wdn16_ref[...] = wdn_ref[0].astype(BF16)

    d_ff = wdn_ref.shape[1]
    gu = _dot(x_ref[...], wgu16_ref[...]) + bgu_ref[0]
    glu = jnp.minimum(gu[:, :d_ff], SWIGLU_LIMIT)
    lin = jnp.clip(gu[:, d_ff:], -SWIGLU_LIMIT, SWIGLU_LIMIT)
    act = glu * jax.nn.sigmoid(SWIGLU_ALPHA * glu) * (lin + 1.0)
    o_ref[...] = _dot(act.astype(BF16), wdn16_ref[...]) + bdn_ref[0]


def moe_experts(xb, blk_e, w_gu, b_gu, w_dn, b_dn):
    n_rows, dm = xb.shape
    d_ff = w_dn.shape[1]
    n_blocks = n_rows // MOE_ROWS
    return pl.pallas_call(
        _moe_kernel,
        grid_spec=pltpu.PrefetchScalarGridSpec(
            num_scalar_prefetch=1,
            grid=(n_blocks,),
            in_specs=[pl.BlockSpec((MOE_ROWS, dm), lambda i, e: (i, 0)),
                      pl.BlockSpec((1, dm, 2 * d_ff), lambda i, e: (e[i], 0, 0)),
                      pl.BlockSpec((1, 1, 2 * d_ff), lambda i, e: (e[i], 0, 0)),
                      pl.BlockSpec((1, d_ff, dm), lambda i, e: (e[i], 0, 0)),
                      pl.BlockSpec((1, 1, dm), lambda i, e: (e[i], 0, 0))],
            out_specs=pl.BlockSpec((MOE_ROWS, dm), lambda i, e: (i, 0)),
            scratch_shapes=[pltpu.VMEM((dm, 2 * d_ff), BF16), pltpu.VMEM((d_ff, dm), BF16)]),
        out_shape=jax.ShapeDtypeStruct((n_rows, dm), F32),
        compiler_params=pltpu.CompilerParams(dimension_semantics=("arbitrary",),
                                             vmem_limit_bytes=VMEM_LIMIT),
        name="moe_experts",
    )(blk_e, xb, w_gu, b_gu, w_dn, b_dn)


def moe_ffn(t, router_w, router_b, w_gu, b_gu, w_dn, b_dn):
    n, dm = t.shape
    logits = jnp.dot(t, router_w, precision=lax.Precision.HIGHEST) + router_b
    top_logit, top_e = lax.top_k(logits, TOP_K)
    top_w = jax.nn.softmax(top_logit, axis=-1)
    flat_e = top_e.reshape(-1).astype(jnp.int32)
    n_asg = n * TOP_K
    seg = 128
    assert n_asg % seg == 0
    onehot = (flat_e.reshape(n_asg // seg, seg, 1) == jnp.arange(N_EXPERTS, dtype=jnp.int32)).astype(F32)
    within = jnp.einsum('ts,bse->bte', jnp.tril(jnp.ones((seg, seg), F32)), onehot)
    seg_tot = within[:, -1, :]
    before = jnp.cumsum(seg_tot, axis=0) - seg_tot
    rank = (jnp.sum(onehot * (within + before[:, None, :]), axis=-1) - 1.0).astype(jnp.int32).reshape(-1)
    counts = jnp.sum(seg_tot, axis=0).astype(jnp.int32)
    padded = (counts + MOE_ROWS - 1) // MOE_ROWS * MOE_ROWS
    pad_end = jnp.cumsum(padded)
    pad_start = pad_end - padded
    grp_start = jnp.cumsum(counts) - counts
    n_blocks = -(-n_asg // MOE_ROWS) + N_EXPERTS
    n_rows = n_blocks * MOE_ROWS
    blk_e = jnp.minimum(jnp.searchsorted(pad_end, jnp.arange(n_blocks, dtype=jnp.int32) * MOE_ROWS, side='right'),
                        N_EXPERTS - 1).astype(jnp.int32)
    order = jnp.argsort(flat_e)
    e_row = jnp.repeat(blk_e, MOE_ROWS)
    j = jnp.arange(n_rows, dtype=jnp.int32) - pad_start[e_row]
    used = j < counts[e_row]
    row_tok = jnp.where(used, order[jnp.where(used, grp_start[e_row] + j, 0)] // TOP_K, 0)
    xb = t.astype(BF16)[row_tok]
    yb = moe_experts(xb, blk_e, w_gu, b_gu[:, None, :], w_dn, b_dn[:, None, :])
    slot = (pad_start[flat_e] + rank).reshape(n, TOP_K)
    y = jnp.zeros((n, dm), F32)
    for j in range(TOP_K):
        y = y + yb[slot[:, j]] * top_w[:, j:j + 1]
    return y


def rms_norm(x, g):
    return x * lax.rsqrt(jnp.mean(x * x, axis=-1, keepdims=True) + EPS) * g


def head_rms_norm(o, g):
    return o * lax.rsqrt(jnp.mean(o * o, axis=-1, keepdims=True) + EPS) * g


def dwconv3(x, w, n_ctx):
    t = x.shape[1]
    xp = jnp.pad(x, ((0, 0), (1, 1), (0, 0)))
    pos = jnp.arange(t)[None, :, None]
    has_prev = (pos != 0) & (pos != n_ctx)
    has_next = (pos != n_ctx - 1) & (pos != t - 1)
    return (w[0] * jnp.where(has_prev, xp[:, :-2], 0.0) + w[1] * x
            + w[2] * jnp.where(has_next, xp[:, 2:], 0.0))


def rope_tables(n_ctx, seq):
    rows = seq // GRID_W
    pos_r = jnp.repeat(jnp.arange(rows, dtype=F32), GRID_W)
    pos_c = jnp.tile(jnp.arange(GRID_W, dtype=F32), rows)
    nf = RET_DK // 4
    inv = ROPE_BASE ** (-jnp.arange(nf, dtype=F32) / nf)
    ang = jnp.concatenate([pos_r[:, None] * inv, pos_c[:, None] * inv], axis=-1)
    cos = jnp.concatenate([jnp.ones((n_ctx, RET_DK // 2), F32), jnp.cos(ang)], axis=0)
    sin = jnp.concatenate([jnp.zeros((n_ctx, RET_DK // 2), F32), jnp.sin(ang)], axis=0)
    return cos, sin


def apply_rope(x, cos, sin):
    half = x.shape[-1] // 2
    x1, x2 = x[..., :half], x[..., half:]
    c, s = cos[None, :, None, :], sin[None, :, None, :]
    return jnp.concatenate([x1 * c - x2 * s, x1 * s + x2 * c], axis=-1)


def project(u, w):
    b, t, dm = u.shape
    m = w.shape[1]
    m_pad = -(-m // 128) * 128
    w16 = jnp.pad(w, ((0, 0), (0, m_pad - m))).astype(BF16)
    z = matmul(u.reshape(b * t, dm).astype(BF16), w16)
    return z[:, :m].reshape(b, t, m)


def mixer_ab(u, n_ctx, rope, w_in, w_out, ret_decay, ret_norm, ml_conv, ml_gate_b, ml_norm):
    b, t, _ = u.shape
    n_ctx_chunks = n_ctx // CHUNK
    rq, rk, rv, rg, mqk, mv, mo, mg = _split(project(u, w_in), AB_SPLITS)
    rq = apply_rope(rq.reshape(b, t, RET_HEADS, RET_DK), *rope).reshape(b, t, RET_QK)
    rk = apply_rope(rk.reshape(b, t, RET_HEADS, RET_DK) * RET_DK ** -0.5, *rope).reshape(b, t, RET_QK)
    log_gamma = jnp.log1p(-jnp.exp(ret_decay))
    ret = retention_scan(rq, rk, rv, log_gamma, n_ctx_chunks)
    ret = ret[0] + ret[1]

    mq, mk = _split(jax.nn.silu(dwconv3(mqk, ml_conv, n_ctx)), (ML_QK, ML_QK))
    mq = mq * ML_DK ** -0.5
    gates = (mg + ml_gate_b).reshape(b, t, 2, 2, ML_HEADS)
    gates = jnp.stack([gates[:, :, :, 0], jax.nn.log_sigmoid(gates[:, :, :, 1])], axis=3)
    g_col = jnp.moveaxis(gates, 2, 0).reshape(2, b, t, 2 * ML_HEADS)
    g_row = jnp.swapaxes(g_col.reshape(2, b, t // CHUNK, CHUNK, 2 * ML_HEADS), 3, 4)
    ml = mlstm_scan(mq, mk, mv, g_col, g_row, n_ctx_chunks)
    ml = ml[0] + ml[1]

    ret = head_rms_norm(ret.reshape(b, t, RET_HEADS, RET_DV), ret_norm).reshape(b, t, RET_W) * jax.nn.silu(rg)
    ml = head_rms_norm(ml.reshape(b, t, ML_HEADS, ML_DV), ml_norm).reshape(b, t, ML_W) * jax.nn.sigmoid(mo)
    return project(jnp.concatenate([ret, ml], axis=-1), w_out)


def mixer_cd(u, n_ctx, w_in, w_out, lb, hg_norm, rw_shift, rw_w0, rw_w2, rw_a0, rw_a2, rw_g2,
             rw_kk_scale, rw_k_a, rw_r_k, rw_norm):
    b, t, _ = u.shape
    n_ctx_chunks = n_ctx // CHUNK
    hq, hf, hi, hg, zr = _split(project(u, w_in), CD_SPLITS)
    hgo = hgrn_scan(hq, hf, hi, lb.reshape(1, HG_K), n_ctx_chunks)
    hgo = hgo[0] + hgo[1]

    r, k, v, wl, al, gl = _split(dwconv3(zr, rw_shift, n_ctx), RWKV_SPLITS)
    kk = (k * rw_kk_scale).reshape(b, t, RW_HEADS, RW_HD)
    kk = (kk * lax.rsqrt(jnp.sum(kk * kk, axis=-1, keepdims=True) + EPS)).reshape(b, t, RW_W)
    gate = jax.nn.sigmoid(gl) @ rw_g2
    k_eff, a_all, lw_all = [], [], []
    for d in range(2):
        wl_d = wl[..., d * RW_W_RANK:(d + 1) * RW_W_RANK]
        al_d = al[..., d * RW_A_RANK:(d + 1) * RW_A_RANK]
        w_log = -jax.nn.softplus(-(rw_w0[d] + jnp.tanh(wl_d) @ rw_w2[d])) - 0.5
        lw_all.append(-jnp.exp(w_log))
        a = jax.nn.sigmoid(rw_a0[d] + al_d @ rw_a2[d])
        a_all.append(a)
        k_eff.append(k * (1.0 + (a - 1.0) * rw_k_a))
    k_eff, a_all, lw_all = jnp.stack(k_eff), jnp.stack(a_all), jnp.stack(lw_all)
    rwo = rwkv_scan(r, k_eff, v, kk, a_all, lw_all, n_ctx_chunks)
    rwo = (rwo[0] + rwo[1]).reshape(b, t, RW_HEADS, RW_HD)

    r4, v4 = r.reshape(b, t, RW_HEADS, RW_HD), v.reshape(b, t, RW_HEADS, RW_HD)
    bonus = sum(jnp.sum(r4 * k_eff[d].reshape(b, t, RW_HEADS, RW_HD) * rw_r_k, axis=-1, keepdims=True) * v4
                for d in range(2))
    hgo = head_rms_norm(hgo.reshape(b, t, HG_HEADS, HG_DV), hg_norm).reshape(b, t, HG_W) * jax.nn.sigmoid(hg)
    rwo = (head_rms_norm(rwo, rw_norm) + bonus).reshape(b, t, RW_W) * gate
    return project(jnp.concatenate([hgo, rwo], axis=-1), w_out)


def hgrn_lower_bound(p, layer):
    sm = jax.nn.softmax(p, axis=0)
    return jnp.cumsum(sm, axis=0)[layer] - sm[0]


def kernel(x, c, ctx, c_ctx, mod_w, mod_b, norm_mix, norm_ffn, norm_final, ab_w_in, ab_w_out, ret_decay, ret_norm, mlstm_conv, mlstm_gate_b, mlstm_norm, cd_w_in, cd_w_out, hgrn_lb, hgrn_norm, rwkv_shift, rwkv_w0, rwkv_w2, rwkv_a0, rwkv_a2, rwkv_g2, rwkv_kk_scale, rwkv_k_a, rwkv_r_k, rwkv_norm, router_w, router_b, exp_w_gate_up, exp_b_gate_up, exp_w_down, exp_b_down):
    bsz, seq, dm = x.shape
    n_ctx = ctx.shape[1]
    depth = mod_w.shape[0]
    t = n_ctx + seq
    rope = rope_tables(n_ctx, seq)
    cond_l = jax.nn.silu(c)
    cond_c = jax.nn.silu(c_ctx)[None, :]
    h = jnp.concatenate([ctx, x], axis=1)
    hp = lax.Precision.HIGHEST

    def per_token(m_c, m_l):
        return jnp.concatenate([jnp.broadcast_to(m_c[:, None], (bsz, n_ctx, dm)),
                                jnp.broadcast_to(m_l[:, None], (bsz, seq, dm))], axis=1)

    for i in range(depth):
        j = i // 2
        m_l = jnp.split(jnp.dot(cond_l, mod_w[i], precision=hp) + mod_b[i], 6, axis=-1)
        m_c = jnp.split(jnp.dot(cond_c, mod_w[i], precision=hp) + mod_b[i], 6, axis=-1)
        m = [per_token(a, b_) for a, b_ in zip(m_c, m_l)]
        u = rms_norm(h, norm_mix[i]) * (1.0 + m[1]) + m[0]
        if i % 2 == 0:
            y = mixer_ab(u, n_ctx, rope, ab_w_in[j], ab_w_out[j], ret_decay[j], ret_norm[j],
                         mlstm_conv[j], mlstm_gate_b[j], mlstm_norm[j])
        else:
            y = mixer_cd(u, n_ctx, cd_w_in[j], cd_w_out[j], hgrn_lower_bound(hgrn_lb, i), hgrn_norm[j],
                         rwkv_shift[j], rwkv_w0[j], rwkv_w2[j], rwkv_a0[j], rwkv_a2[j], rwkv_g2[j],
                         rwkv_kk_scale[j], rwkv_k_a[j], rwkv_r_k[j], rwkv_norm[j])
        h = h + m[2] * y
        vv = rms_norm(h, norm_ffn[i]) * (1.0 + m[4]) + m[3]
        f = moe_ffn(vv.reshape(-1, dm), router_w[i], router_b[i], exp_w_gate_up[i], exp_b_gate_up[i],
                    exp_w_down[i], exp_b_down[i]).reshape(vv.shape)
        h = h + m[5] * f
    return rms_norm(h[:, n_ctx:], norm_final)
```

```python
import functools

import numpy as np
import jax
import jax.numpy as jnp
from jax import lax
from jax.experimental import pallas as pl
from jax.experimental.pallas import tpu as pltpu

F32 = jnp.float32
BF16 = jnp.bfloat16

CHUNK = 64
GRID_W = 64
EPS = 1e-6
ROPE_BASE = 10000.0
RET_HEADS, RET_DK, RET_DV = 4, 64, 128
ML_HEADS, ML_DK, ML_DV = 4, 64, 128
HG_HEADS, HG_DK, HG_DV = 4, 128, 128
RW_HEADS, RW_HD = 8, 64
RW_W_RANK, RW_A_RANK, RW_G_RANK = 64, 64, 128
N_EXPERTS, TOP_K = 32, 4
SWIGLU_LIMIT, SWIGLU_ALPHA = 7.0, 1.702

RET_QK, RET_W = RET_HEADS * RET_DK, RET_HEADS * RET_DV
ML_QK, ML_W = ML_HEADS * ML_DK, ML_HEADS * ML_DV
HG_K, HG_W = HG_HEADS * HG_DK, HG_HEADS * HG_DV
RW_W = RW_HEADS * RW_HD
AB_SPLITS = (RET_QK, RET_QK, RET_W, RET_W, 2 * ML_QK, ML_W, ML_W, 4 * ML_HEADS)
RWKV_SPLITS = (RW_W, RW_W, RW_W, 2 * RW_W_RANK, 2 * RW_A_RANK, RW_G_RANK)
RWKV_IN = sum(RWKV_SPLITS)
CD_SPLITS = (HG_K, 2 * HG_K, HG_W, HG_W, RWKV_IN)

RW_GROUP = 4
RW_GW = RW_GROUP * RW_HD
RW_BASE = 8
HG_LEVELS = 6

MM_ROWS = 256
MOE_ROWS = 512
VMEM_LIMIT = 56 * 1024 * 1024


def _split(z, sizes):
    return jnp.split(z, [int(s) for s in np.cumsum(sizes)[:-1]], axis=-1)


def _dot(a, b):
    return jnp.dot(a, b, preferred_element_type=F32)


def _dot_nt(a, b):
    return lax.dot_general(a, b, (((1,), (1,)), ((), ())), preferred_element_type=F32)


def _dot_tn(a, b):
    return lax.dot_general(a, b, (((0,), (0,)), ((), ())), preferred_element_type=F32)


def _mask_dot(m16, x):
    hi = x.astype(BF16)
    lo = (x - hi.astype(F32)).astype(BF16)
    return _dot(m16, hi) + _dot(m16, lo)


def _chunk_index(d, c, n_ctx_chunks, n_chunks):
    rev = jnp.where(c < n_ctx_chunks, n_ctx_chunks - 1 - c, n_chunks + n_ctx_chunks - 1 - c)
    return jnp.where(d == 0, c, rev)


def _mm_kernel(x_ref, w_ref, o_ref):
    o_ref[...] = _dot(x_ref[...], w_ref[...])


def matmul(x, w):
    n, k = x.shape
    m = w.shape[1]
    assert n % MM_ROWS == 0 and m % 128 == 0
    return pl.pallas_call(
        _mm_kernel,
        grid=(n // MM_ROWS,),
        in_specs=[pl.BlockSpec((MM_ROWS, k), lambda i: (i, 0)),
                  pl.BlockSpec((k, m), lambda i: (0, 0))],
        out_specs=pl.BlockSpec((MM_ROWS, m), lambda i: (i, 0)),
        out_shape=jax.ShapeDtypeStruct((n, m), F32),
        compiler_params=pltpu.CompilerParams(dimension_semantics=("parallel",),
                                             vmem_limit_bytes=VMEM_LIMIT),
        name="dense_proj",
    )(x, w)


def _ret_kernel(lg_ref, q_ref, k_ref, v_ref, o_ref, s_ref):
    d = pl.program_id(1)
    c = pl.program_id(2)

    @pl.when(c == 0)
    def _():
        s_ref[...] = jnp.zeros_like(s_ref)

    L = CHUNK
    ti = lax.broadcasted_iota(jnp.int32, (L, L), 0)
    si = lax.broadcasted_iota(jnp.int32, (L, L), 1)
    dist = jnp.where(d == 0, ti - si, si - ti)
    causal = dist >= 0
    distf = jnp.maximum(dist, 0).astype(F32)
    row = lax.broadcasted_iota(jnp.int32, (L, 1), 0)
    pos = jnp.where(d == 0, row, L - 1 - row).astype(F32)
    q = q_ref[0]
    k = k_ref[0]
    v = v_ref[0]
    for h in range(RET_HEADS):
        lg = lg_ref[d, h]
        qh = q[:, h * RET_DK:(h + 1) * RET_DK]
        kh = k[:, h * RET_DK:(h + 1) * RET_DK]
        vh = v[:, h * RET_DV:(h + 1) * RET_DV].astype(BF16)
        dm = jnp.where(causal, jnp.exp(lg * distf), 0.0)
        qd = (qh * jnp.exp(lg * (pos + 1.0))).astype(BF16)
        kd = (kh * jnp.exp(lg * (L - 1.0 - pos))).astype(BF16)
        att = _dot_nt(qh.astype(BF16), kh.astype(BF16)) * dm
        s = s_ref[h]
        o = _dot(qd, s.astype(BF16)) + _dot(att.astype(BF16), vh)
        o_ref[0, 0, :, h * RET_DV:(h + 1) * RET_DV] = o
        s_ref[h] = jnp.exp(lg * L) * s + _dot_tn(kd, vh)


def retention_scan(q, k, v, log_gamma, n_ctx_chunks):
    b, t, _ = q.shape
    n_chunks = t // CHUNK
    idx = functools.partial(_chunk_index, n_ctx_chunks=n_ctx_chunks, n_chunks=n_chunks)
    in_map = lambda bi, d, c, lg: (bi, idx(d, c), 0)
    return pl.pallas_call(
        _ret_kernel,
        grid_spec=pltpu.PrefetchScalarGridSpec(
            num_scalar_prefetch=1,
            grid=(b, 2, n_chunks),
            in_specs=[pl.BlockSpec((1, CHUNK, RET_QK), in_map),
                      pl.BlockSpec((1, CHUNK, RET_QK), in_map),
                      pl.BlockSpec((1, CHUNK, RET_W), in_map)],
            out_specs=pl.BlockSpec((1, 1, CHUNK, RET_W), lambda bi, d, c, lg: (d, bi, idx(d, c), 0)),
            scratch_shapes=[pltpu.VMEM((RET_HEADS, RET_DK, RET_DV), F32)]),
        out_shape=jax.ShapeDtypeStruct((2, b, t, RET_W), F32),
        compiler_params=pltpu.CompilerParams(
            dimension_semantics=("parallel", "parallel", "arbitrary"), vmem_limit_bytes=VMEM_LIMIT),
        name="retention_scan",
    )(log_gamma, q, k, v)


def _mlstm_kernel(q_ref, k_ref, v_ref, gc_ref, gr_ref, o_ref, c_ref, m_ref):
    d = pl.program_id(1)
    c = pl.program_id(2)

    @pl.when(c == 0)
    def _():
        c_ref[...] = jnp.zeros_like(c_ref)
        m_ref[...] = jnp.zeros_like(m_ref)

    L = CHUNK
    ti = lax.broadcasted_iota(jnp.int32, (L, L), 0)
    si = lax.broadcasted_iota(jnp.int32, (L, L), 1)
    dist = jnp.where(d == 0, ti - si, si - ti)
    le = dist >= 0
    le_t = dist <= 0
    lane = lax.broadcasted_iota(jnp.int32, (L, ML_DV), 1)
    ones_col = jnp.where(lane == 0, 1.0, 0.0).astype(BF16)
    q = q_ref[0]
    k = k_ref[0]
    v = v_ref[0]
    gc = gc_ref[0, 0]
    gr = gr_ref[0, 0, 0]
    H = ML_HEADS
    for h in range(H):
        i_col = gc[:, h:h + 1]
        f_col = gc[:, H + h:H + h + 1]
        i_row = gr[h:h + 1, :]
        f_row = gr[H + h:H + h + 1, :]
        b_col = jnp.sum(jnp.where(le, f_row, 0.0), axis=1, keepdims=True)
        b_row = jnp.sum(jnp.where(le_t, f_col, 0.0), axis=0, keepdims=True)
        b_last = jnp.sum(f_row, axis=1, keepdims=True)
        m = m_ref[h:h + 1, 0:1]
        log_d = jnp.where(le, b_col - b_row + i_row, -jnp.inf)
        inter = b_col + m
        m_row = jnp.maximum(inter, jnp.max(log_d, axis=1, keepdims=True))
        qh = q[:, h * ML_DK:(h + 1) * ML_DK].astype(BF16)
        kh = k[:, h * ML_DK:(h + 1) * ML_DK]
        vh = v[:, h * ML_DV:(h + 1) * ML_DV].astype(BF16)
        v_aug = jnp.concatenate([vh, ones_col], axis=1)
        s = _dot_nt(qh, kh.astype(BF16)) * jnp.exp(log_d - m_row)
        w_inter = jnp.exp(inter - m_row)
        c_aug = c_ref[h]
        num = w_inter * _dot(qh, c_aug.astype(BF16)) + _dot(s.astype(BF16), v_aug)
        den = num[:, ML_DV:ML_DV + 1]
        hh = num[:, :ML_DV] / jnp.maximum(jnp.abs(den), jnp.exp(-m_row))
        o_ref[0, 0, :, h * ML_DV:(h + 1) * ML_DV] = hh
        log_w_row = b_last - b_row + i_row
        m_new = jnp.maximum(b_last + m, jnp.max(log_w_row, axis=1, keepdims=True))
        w_col = jnp.exp(b_last - b_col + i_col - m_new)
        decay = jnp.exp(b_last + m - m_new)
        c_ref[h] = decay * c_aug + _dot_tn((kh * w_col).astype(BF16), v_aug)
        m_ref[h:h + 1, :] = jnp.broadcast_to(m_new, (1, 128))


def mlstm_scan(q, k, v, g_col, g_row, n_ctx_chunks):
    b, t, _ = q.shape
    n_chunks = t // CHUNK
    idx = functools.partial(_chunk_index, n_ctx_chunks=n_ctx_chunks, n_chunks=n_chunks)
    in_map = lambda bi, d, c: (bi, idx(d, c), 0)
    return pl.pallas_call(
        _mlstm_kernel,
        grid=(b, 2, n_chunks),
        in_specs=[pl.BlockSpec((1, CHUNK, ML_QK), in_map),
                  pl.BlockSpec((1, CHUNK, ML_QK), in_map),
                  pl.BlockSpec((1, CHUNK, ML_W), in_map),
                  pl.BlockSpec((1, 1, CHUNK, 2 * ML_HEADS), lambda bi, d, c: (d, bi, idx(d, c), 0)),
                  pl.BlockSpec((1, 1, 1, 2 * ML_HEADS, CHUNK), lambda bi, d, c: (d, bi, idx(d, c), 0, 0))],
        out_specs=pl.BlockSpec((1, 1, CHUNK, ML_W), lambda bi, d, c: (d, bi, idx(d, c), 0)),
        out_shape=jax.ShapeDtypeStruct((2, b, t, ML_W), F32),
        scratch_shapes=[pltpu.VMEM((ML_HEADS, ML_DK, 2 * ML_DV), F32), pltpu.VMEM((8, 128), F32)],
        compiler_params=pltpu.CompilerParams(
            dimension_semantics=("parallel", "parallel", "arbitrary"), vmem_limit_bytes=VMEM_LIMIT),
        name="mlstm_scan",
    )(q, k, v, g_col, g_row)


def _hgrn_masks():
    L = CHUNK
    stack = np.zeros((2, (1 + 2 * HG_LEVELS) * L, L), np.float32)
    pair = np.zeros((2, HG_LEVELS + 1, L, L), np.float32)
    for d in range(2):
        p = np.arange(L) if d == 0 else L - 1 - np.arange(L)
        pt, ps = p[:, None], p[None, :]
        stack[d, :L] = ps <= pt
        for l in range(HG_LEVELS):
            parent, half = p >> (l + 1), (p >> l) & 1
            split = parent * (2 << l) + (1 << l) - 1
            same = parent[:, None] == parent[None, :]
            e = same & (half[:, None] == 1) & (ps > split[:, None]) & (ps <= pt)
            f = same & (half[:, None] == 0) & (ps > pt) & (ps <= split[:, None])
            stack[d, (1 + 2 * l) * L:(2 + 2 * l) * L] = e
            stack[d, (2 + 2 * l) * L:(3 + 2 * l) * L] = f
            pair[d, l] = same & (half[:, None] == 1) & (half[None, :] == 0)
        pair[d, HG_LEVELS] = np.eye(L)
    return stack, pair


def _hgrn_kernel(q_ref, hf_ref, v_ref, lb_ref, stack_ref, pair_ref, o_ref, s_ref):
    c = pl.program_id(2)

    @pl.when(c == 0)
    def _():
        s_ref[...] = jnp.zeros_like(s_ref)

    L = CHUNK
    lb = lb_ref[...]
    f = lb + (1.0 - lb) * jax.nn.sigmoid(hf_ref[0])
    kk = 1.0 - f
    g = jnp.log(f)
    sums = _mask_dot(stack_ref[0], g)
    q = q_ref[0]
    v = v_ref[0]
    for h in range(HG_HEADS):
        hs = slice(h * HG_DK, (h + 1) * HG_DK)
        qh, kh, gh = q[:, hs], kk[:, hs], g[:, hs]
        vh = v[:, h * HG_DV:(h + 1) * HG_DV].astype(BF16)
        b = sums[0:L, hs]
        att = pair_ref[0, HG_LEVELS] * _dot_nt(qh.astype(BF16), kh.astype(BF16))
        for l in range(HG_LEVELS):
            e = sums[(1 + 2 * l) * L:(2 + 2 * l) * L, hs]
            fl = sums[(2 + 2 * l) * L:(3 + 2 * l) * L, hs]
            att = att + pair_ref[0, l] * _dot_nt((qh * jnp.exp(e)).astype(BF16), (kh * jnp.exp(fl)).astype(BF16))
        b_last = jnp.sum(gh, axis=0, keepdims=True)
        st = s_ref[h]
        o = _dot_nt((qh * jnp.exp(b)).astype(BF16), st.astype(BF16)) + _dot(att.astype(BF16), vh)
        o_ref[0, 0, :, h * HG_DV:(h + 1) * HG_DV] = o
        k_dec = (kh * jnp.exp(b_last - b)).astype(BF16)
        s_ref[h] = st * jnp.exp(b_last) + _dot_tn(vh, k_dec)


def hgrn_scan(q, hf, v, lb, n_ctx_chunks):
    b, t, _ = q.shape
    n_chunks = t // CHUNK
    stack, pair = _hgrn_masks()
    idx = functools.partial(_chunk_index, n_ctx_chunks=n_ctx_chunks, n_chunks=n_chunks)
    in_map = lambda bi, d, c: (bi, idx(d, c), 0)
    return pl.pallas_call(
        _hgrn_kernel,
        grid=(b, 2, n_chunks),
        in_specs=[pl.BlockSpec((1, CHUNK, HG_K), in_map),
                  pl.BlockSpec((1, CHUNK, HG_K), lambda bi, d, c: (bi, idx(d, c), d)),
                  pl.BlockSpec((1, CHUNK, HG_W), in_map),
                  pl.BlockSpec((1, HG_K), lambda bi, d, c: (0, 0)),
                  pl.BlockSpec((1,) + stack.shape[1:], lambda bi, d, c: (d, 0, 0)),
                  pl.BlockSpec((1,) + pair.shape[1:], lambda bi, d, c: (d, 0, 0, 0))],
        out_specs=pl.BlockSpec((1, 1, CHUNK, HG_W), lambda bi, d, c: (d, bi, idx(d, c), 0)),
        out_shape=jax.ShapeDtypeStruct((2, b, t, HG_W), F32),
        scratch_shapes=[pltpu.VMEM((HG_HEADS, HG_DV, HG_DK), F32)],
        compiler_params=pltpu.CompilerParams(
            dimension_semantics=("parallel", "parallel", "arbitrary"), vmem_limit_bytes=VMEM_LIMIT),
        name="hgrn_scan",
    )(q, hf, v, lb, jnp.asarray(stack, BF16), jnp.asarray(pair, F32))


def _rwkv_masks():
    L, G = CHUNK, RW_GROUP
    n = G * L
    head = np.arange(n) // L
    same = head[:, None] == head[None, :]
    bd = same.astype(np.float32)
    t = np.arange(n) % L
    strict = np.zeros((2, n, n), np.float32)
    incl = np.zeros((2, n, n), np.float32)
    cum = np.zeros((2, L, L), np.float32)
    n_lvl = int(np.log2(L // RW_BASE))
    solve = np.zeros((2, 2 + n_lvl, n, n), np.float32)
    for d in range(2):
        p = t if d == 0 else L - 1 - t
        strict[d] = same & (p[None, :] < p[:, None])
        incl[d] = same & (p[None, :] <= p[:, None])
        pc = p[:L]
        cum[d] = pc[None, :] <= pc[:, None]
        solve[d, 0] = np.eye(n)
        solve[d, 1] = same & ((p // RW_BASE)[:, None] == (p // RW_BASE)[None, :])
        for i in range(n_lvl):
            m = RW_BASE << i
            parent, half = p // (2 * m), (p // m) % 2
            solve[d, 2 + i] = same & (parent[:, None] == parent[None, :]) & (half[:, None] == 1) & (half[None, :] == 0)
    return bd, strict, incl, cum, solve


def _unit_lower_inverse(n_mat, solve_ref):
    dot16 = lambda a, b: _dot(a.astype(BF16), b.astype(BF16))
    eye = solve_ref[0, 0]
    nd = n_mat * solve_ref[0, 1]
    t_inv = eye - nd
    pw = nd
    for _ in range(int(np.log2(RW_BASE)) - 1):
        pw = dot16(pw, pw)
        t_inv = t_inv + dot16(t_inv, pw)
    for i in range(solve_ref.shape[1] - 2):
        t_inv = t_inv - dot16(dot16(t_inv, n_mat * solve_ref[0, 2 + i]), t_inv)
    return t_inv


def _rwkv_kernel(r_ref, k_ref, v_ref, kk_ref, a_ref, lw_ref, bd_ref, strict_ref, incl_ref, cum_ref,
                 solve_ref, o_ref, s_ref):
    c = pl.program_id(2)

    @pl.when(c == 0)
    def _():
        s_ref[...] = jnp.zeros_like(s_ref)

    G = RW_GROUP
    bd = bd_ref[...]
    strict = strict_ref[0]
    incl = incl_ref[0]
    cum = cum_ref[0]

    def expand(x):
        return (jnp.concatenate([x] * G, axis=0) * bd).astype(BF16)

    def tile(x):
        return jnp.concatenate([x] * G, axis=0).astype(BF16)

    for g in range(RW_HEADS // G):
        gs = slice(g * RW_GW, (g + 1) * RW_GW)
        r, k, v = r_ref[0, :, gs], k_ref[0, 0, :, gs], v_ref[0, :, gs]
        kk, a, lw = kk_ref[0, :, gs], a_ref[0, 0, :, gs], lw_ref[0, 0, :, gs]
        kb = kk * a
        cs = _mask_dot(cum, lw)
        c_last = jnp.sum(lw, axis=0, keepdims=True)
        p_inv = jnp.exp(-cs)
        p_to_end = jnp.exp(c_last - cs)
        kkd = expand(kk * jnp.exp(cs - lw))
        rp = expand(r * jnp.exp(cs))
        v_bd = expand(v)
        t_k = tile(k * p_inv)
        t_b = tile(kb * p_inv)
        n_mat = strict * _dot_nt(kkd, t_b)
        m_k = (strict * _dot_nt(kkd, t_k)).astype(BF16)
        m_rk = (incl * _dot_nt(rp, t_k)).astype(BF16)
        m_rb = (incl * _dot_nt(rp, t_b)).astype(BF16)
        st = s_ref[g]
        st16 = st.astype(BF16)
        t_inv = _unit_lower_inverse(n_mat, solve_ref).astype(BF16)
        u16 = _dot(t_inv, (_dot_nt(kkd, st16) + _dot(m_k, v_bd)).astype(BF16)).astype(BF16)
        y = _dot_nt(rp, st16) + _dot(m_rk, v_bd) - _dot(m_rb, u16)
        L = CHUNK
        o_ref[0, 0, :, gs] = y[0:L] + y[L:2 * L] + y[2 * L:3 * L] + y[3 * L:4 * L]
        s_ref[g] = (st * jnp.exp(c_last) + _dot_tn(v_bd, expand(k * p_to_end))
                    - _dot_tn(u16, expand(kb * p_to_end)))


def rwkv_scan(r, k_eff, v, kk, a, lw, n_ctx_chunks):
    b, t, _ = r.shape
    n_chunks = t // CHUNK
    bd, strict, incl, cum, solve = _rwkv_masks()
    idx = functools.partial(_chunk_index, n_ctx_chunks=n_ctx_chunks, n_chunks=n_chunks)
    shared = pl.BlockSpec((1, CHUNK, RW_W), lambda bi, d, c: (bi, idx(d, c), 0))
    per_dir = pl.BlockSpec((1, 1, CHUNK, RW_W), lambda bi, d, c: (d, bi, idx(d, c), 0))
    n = RW_GROUP * CHUNK
    return pl.pallas_call(
        _rwkv_kernel,
        grid=(b, 2, n_chunks),
        in_specs=[shared, per_dir, shared, shared, per_dir, per_dir,
                  pl.BlockSpec((n, RW_GW), lambda bi, d, c: (0, 0)),
                  pl.BlockSpec((1, n, n), lambda bi, d, c: (d, 0, 0)),
                  pl.BlockSpec((1, n, n), lambda bi, d, c: (d, 0, 0)),
                  pl.BlockSpec((1, CHUNK, CHUNK), lambda bi, d, c: (d, 0, 0)),
                  pl.BlockSpec((1,) + solve.shape[1:], lambda bi, d, c: (d, 0, 0, 0))],
        out_specs=per_dir,
        out_shape=jax.ShapeDtypeStruct((2, b, t, RW_W), F32),
        scratch_shapes=[pltpu.VMEM((RW_HEADS // RW_GROUP, RW_GW, RW_GW), F32)],
        compiler_params=pltpu.CompilerParams(
            dimension_semantics=("parallel", "parallel", "arbitrary"), vmem_limit_bytes=VMEM_LIMIT),
        name="rwkv7_scan",
    )(r, k_eff, v, kk, a, lw, jnp.asarray(bd, F32), jnp.asarray(strict, F32), jnp.asarray(incl, F32),
      jnp.asarray(cum, BF16), jnp.asarray(solve, F32))


def _moe_kernel(e_ref, x_ref, wgu_ref, bgu_ref, wdn_ref, bdn_ref, o_ref, wgu16_ref, wdn16_ref):
    i = pl.program_id(0)
    new_expert = jnp.logical_or(i == 0, e_ref[i] != e_ref[jnp.maximum(i - 1, 0)])

    @pl.when(new_expert)
    def _():
        wgu16_ref[...] = wgu_ref[0, 0].astype(BF16)
        wdn16_ref[...] = wdn_ref[0, 0].astype(BF16)

    d_ff = wdn16_ref.shape[0]
    gu = _dot(x_ref[...], wgu16_ref[...]) + bgu_ref[0, 0]
    glu = jnp.minimum(gu[:, :d_ff], SWIGLU_LIMIT)
    lin = jnp.clip(gu[:, d_ff:], -SWIGLU_LIMIT, SWIGLU_LIMIT)
    act = glu * jax.nn.sigmoid(SWIGLU_ALPHA * glu) * (lin + 1.0)
    o_ref[...] = _dot(act.astype(BF16), wdn16_ref[...]) + bdn_ref[0, 0]


def moe_experts(xb, blk_e, layer, w_gu, b_gu, w_dn, b_dn):
    n_rows, dm = xb.shape
    d_ff = w_dn.shape[2]
    n_blocks = n_rows // MOE_ROWS
    return pl.pallas_call(
        _moe_kernel,
        grid_spec=pltpu.PrefetchScalarGridSpec(
            num_scalar_prefetch=1,
            grid=(n_blocks,),
            in_specs=[pl.BlockSpec((MOE_ROWS, dm), lambda i, e: (i, 0)),
                      pl.BlockSpec((1, 1, dm, 2 * d_ff), lambda i, e: (layer, e[i], 0, 0)),
                      pl.BlockSpec((1, 1, 1, 2 * d_ff), lambda i, e: (layer, e[i], 0, 0)),
                      pl.BlockSpec((1, 1, d_ff, dm), lambda i, e: (layer, e[i], 0, 0)),
                      pl.BlockSpec((1, 1, 1, dm), lambda i, e: (layer, e[i], 0, 0))],
            out_specs=pl.BlockSpec((MOE_ROWS, dm), lambda i, e: (i, 0)),
            scratch_shapes=[pltpu.VMEM((dm, 2 * d_ff), BF16), pltpu.VMEM((d_ff, dm), BF16)]),
        out_shape=jax.ShapeDtypeStruct((n_rows, dm), F32),
        compiler_params=pltpu.CompilerParams(dimension_semantics=("arbitrary",),
                                             vmem_limit_bytes=VMEM_LIMIT),
        name="moe_experts",
    )(blk_e, xb, w_gu, b_gu, w_dn, b_dn)


def moe_ffn(t, router_w, router_b, layer, w_gu, b_gu, w_dn, b_dn):
    n, dm = t.shape
    logits = jnp.dot(t, router_w, precision=lax.Precision.HIGHEST) + router_b
    top_logit, top_e = lax.top_k(logits, TOP_K)
    top_w = jax.nn.softmax(top_logit, axis=-1)
    flat_e = top_e.reshape(-1).astype(jnp.int32)
    n_asg = n * TOP_K
    seg = 128
    assert n_asg % seg == 0
    onehot = (flat_e.reshape(n_asg // seg, seg, 1) == jnp.arange(N_EXPERTS, dtype=jnp.int32)).astype(F32)
    within = jnp.einsum('ts,bse->bte', jnp.tril(jnp.ones((seg, seg), F32)), onehot)
    seg_tot = within[:, -1, :]
    before = jnp.cumsum(seg_tot, axis=0) - seg_tot
    rank = (jnp.sum(onehot * (within + before[:, None, :]), axis=-1) - 1.0).astype(jnp.int32).reshape(-1)
    counts = jnp.sum(seg_tot, axis=0).astype(jnp.int32)
    padded = (counts + MOE_ROWS - 1) // MOE_ROWS * MOE_ROWS
    pad_end = jnp.cumsum(padded)
    pad_start = pad_end - padded
    grp_start = jnp.cumsum(counts) - counts
    n_blocks = -(-n_asg // MOE_ROWS) + N_EXPERTS
    n_rows = n_blocks * MOE_ROWS
    blk_e = jnp.minimum(jnp.searchsorted(pad_end, jnp.arange(n_blocks, dtype=jnp.int32) * MOE_ROWS, side='right'),
                        N_EXPERTS - 1).astype(jnp.int32)
    order = jnp.argsort(flat_e)
    e_row = jnp.repeat(blk_e, MOE_ROWS)
    j = jnp.arange(n_rows, dtype=jnp.int32) - pad_start[e_row]
    used = j < counts[e_row]
    row_tok = jnp.where(used, order[jnp.where(used, grp_start[e_row] + j, 0)] // TOP_K, 0)
    xb = t.astype(BF16)[row_tok]
    yb = moe_experts(xb, blk_e, layer, w_gu, b_gu[:, :, None, :], w_dn, b_dn[:, :, None, :])
    slot = (pad_start[flat_e] + rank).reshape(n, TOP_K)
    y = jnp.zeros((n, dm), F32)
    for j in range(TOP_K):
        y = y + yb[slot[:, j]] * top_w[:, j:j + 1]
    return y


def rms_norm(x, g):
    return x * lax.rsqrt(jnp.mean(x * x, axis=-1, keepdims=True) + EPS) * g


def head_rms_norm(o, g):
    return o * lax.rsqrt(jnp.mean(o * o, axis=-1, keepdims=True) + EPS) * g


def dwconv3(x, w, n_ctx):
    t = x.shape[1]
    xp = jnp.pad(x, ((0, 0), (1, 1), (0, 0)))
    pos = jnp.arange(t)[None, :, None]
    has_prev = (pos != 0) & (pos != n_ctx)
    has_next = (pos != n_ctx - 1) & (pos != t - 1)
    return (w[0] * jnp.where(has_prev, xp[:, :-2], 0.0) + w[1] * x
            + w[2] * jnp.where(has_next, xp[:, 2:], 0.0))


def rope_tables(n_ctx, seq):
    rows = seq // GRID_W
    pos_r = jnp.repeat(jnp.arange(rows, dtype=F32), GRID_W)
    pos_c = jnp.tile(jnp.arange(GRID_W, dtype=F32), rows)
    nf = RET_DK // 4
    inv = ROPE_BASE ** (-jnp.arange(nf, dtype=F32) / nf)
    ang = jnp.concatenate([pos_r[:, None] * inv, pos_c[:, None] * inv], axis=-1)
    cos = jnp.concatenate([jnp.ones((n_ctx, RET_DK // 2), F32), jnp.cos(ang)], axis=0)
    sin = jnp.concatenate([jnp.zeros((n_ctx, RET_DK // 2), F32), jnp.sin(ang)], axis=0)
    return cos, sin


def apply_rope(x, cos, sin):
    half = x.shape[-1] // 2
    x1, x2 = x[..., :half], x[..., half:]
    c, s = cos[None, :, None, :], sin[None, :, None, :]
    return jnp.concatenate([x1 * c - x2 * s, x1 * s + x2 * c], axis=-1)


def project(u, w):
    b, t, dm = u.shape
    m = w.shape[1]
    m_pad = -(-m // 128) * 128
    w16 = jnp.pad(w, ((0, 0), (0, m_pad - m))).astype(BF16)
    z = matmul(u.reshape(b * t, dm).astype(BF16), w16)
    return z[:, :m].reshape(b, t, m)


def mixer_ab(u, n_ctx, rope, w_in, w_out, ret_decay, ret_norm, ml_conv, ml_gate_b, ml_norm):
    b, t, _ = u.shape
    n_ctx_chunks = n_ctx // CHUNK
    rq, rk, rv, rg, mqk, mv, mo, mg = _split(project(u, w_in), AB_SPLITS)
    rq = apply_rope(rq.reshape(b, t, RET_HEADS, RET_DK), *rope).reshape(b, t, RET_QK)
    rk = apply_rope(rk.reshape(b, t, RET_HEADS, RET_DK) * RET_DK ** -0.5, *rope).reshape(b, t, RET_QK)
    log_gamma = jnp.log1p(-jnp.exp(ret_decay))
    ret = retention_scan(rq, rk, rv, log_gamma, n_ctx_chunks)
    ret = ret[0] + ret[1]

    mq, mk = _split(jax.nn.silu(dwconv3(mqk, ml_conv, n_ctx)), (ML_QK, ML_QK))
    mq = mq * ML_DK ** -0.5
    gates = (mg + ml_gate_b).reshape(b, t, 2, 2, ML_HEADS)
    gates = jnp.stack([gates[:, :, :, 0], jax.nn.log_sigmoid(gates[:, :, :, 1])], axis=3)
    g_col = jnp.moveaxis(gates, 2, 0).reshape(2, b, t, 2 * ML_HEADS)
    g_row = jnp.swapaxes(g_col.reshape(2, b, t // CHUNK, CHUNK, 2 * ML_HEADS), 3, 4)
    ml = mlstm_scan(mq, mk, mv, g_col, g_row, n_ctx_chunks)
    ml = ml[0] + ml[1]

    ret = head_rms_norm(ret.reshape(b, t, RET_HEADS, RET_DV), ret_norm).reshape(b, t, RET_W) * jax.nn.silu(rg)
    ml = head_rms_norm(ml.reshape(b, t, ML_HEADS, ML_DV), ml_norm).reshape(b, t, ML_W) * jax.nn.sigmoid(mo)
    return project(jnp.concatenate([ret, ml], axis=-1), w_out)


def mixer_cd(u, n_ctx, w_in, w_out, lb, hg_norm, rw_shift, rw_w0, rw_w2, rw_a0, rw_a2, rw_g2,
             rw_kk_scale, rw_k_a, rw_r_k, rw_norm):
    b, t, _ = u.shape
    n_ctx_chunks = n_ctx // CHUNK
    hq, hf, hi, hg, zr = _split(project(u, w_in), CD_SPLITS)
    hgo = hgrn_scan(hq, hf, hi, lb.reshape(1, HG_K), n_ctx_chunks)
    hgo = hgo[0] + hgo[1]

    r, k, v, wl, al, gl = _split(dwconv3(zr, rw_shift, n_ctx), RWKV_SPLITS)
    kk = (k * rw_kk_scale).reshape(b, t, RW_HEADS, RW_HD)
    kk = (kk * lax.rsqrt(jnp.sum(kk * kk, axis=-1, keepdims=True) + EPS)).reshape(b, t, RW_W)
    gate = jax.nn.sigmoid(gl) @ rw_g2
    k_eff, a_all, lw_all = [], [], []
    for d in range(2):
        wl_d = wl[..., d * RW_W_RANK:(d + 1) * RW_W_RANK]
        al_d = al[..., d * RW_A_RANK:(d + 1) * RW_A_RANK]
        w_log = -jax.nn.softplus(-(rw_w0[d] + jnp.tanh(wl_d) @ rw_w2[d])) - 0.5
        lw_all.append(-jnp.exp(w_log))
        a = jax.nn.sigmoid(rw_a0[d] + al_d @ rw_a2[d])
        a_all.append(a)
        k_eff.append(k * (1.0 + (a - 1.0) * rw_k_a))
    k_eff, a_all, lw_all = jnp.stack(k_eff), jnp.stack(a_all), jnp.stack(lw_all)
    rwo = rwkv_scan(r, k_eff, v, kk, a_all, lw_all, n_ctx_chunks)
    rwo = (rwo[0] + rwo[1]).reshape(b, t, RW_HEADS, RW_HD)

    r4, v4 = r.reshape(b, t, RW_HEADS, RW_HD), v.reshape(b, t, RW_HEADS, RW_HD)
    bonus = sum(jnp.sum(r4 * k_eff[d].reshape(b, t, RW_HEADS, RW_HD) * rw_r_k, axis=-1, keepdims=True) * v4
                for d in range(2))
    hgo = head_rms_norm(hgo.reshape(b, t, HG_HEADS, HG_DV), hg_norm).reshape(b, t, HG_W) * jax.nn.sigmoid(hg)
    rwo = (head_rms_norm(rwo, rw_norm) + bonus).reshape(b, t, RW_W) * gate
    return project(jnp.concatenate([hgo, rwo], axis=-1), w_out)


def hgrn_lower_bound(p, layer):
    sm = jax.nn.softmax(p, axis=0)
    return jnp.cumsum(sm, axis=0)[layer] - sm[0]


def kernel(x, c, ctx, c_ctx, mod_w, mod_b, norm_mix, norm_ffn, norm_final, ab_w_in, ab_w_out, ret_decay, ret_norm, mlstm_conv, mlstm_gate_b, mlstm_norm, cd_w_in, cd_w_out, hgrn_lb, hgrn_norm, rwkv_shift, rwkv_w0, rwkv_w2, rwkv_a0, rwkv_a2, rwkv_g2, rwkv_kk_scale, rwkv_k_a, rwkv_r_k, rwkv_norm, router_w, router_b, exp_w_gate_up, exp_b_gate_up, exp_w_down, exp_b_down):
    bsz, seq, dm = x.shape
    n_ctx = ctx.shape[1]
    depth = mod_w.shape[0]
    t = n_ctx + seq
    rope = rope_tables(n_ctx, seq)
    cond_l = jax.nn.silu(c)
    cond_c = jax.nn.silu(c_ctx)[None, :]
    h = jnp.concatenate([ctx, x], axis=1)
    hp = lax.Precision.HIGHEST

    is_ctx = (jnp.arange(t) < n_ctx)[None, :, None]

    def per_token(m_c, m_l):
        return jnp.where(is_ctx, m_c[:, None, :], m_l[:, None, :])

    for i in range(depth):
        j = i // 2
        m_l = jnp.split(jnp.dot(cond_l, mod_w[i], precision=hp) + mod_b[i], 6, axis=-1)
        m_c = jnp.split(jnp.dot(cond_c, mod_w[i], precision=hp) + mod_b[i], 6, axis=-1)
        m = [per_token(a, b_) for a, b_ in zip(m_c, m_l)]
        u = rms_norm(h, norm_mix[i]) * (1.0 + m[1]) + m[0]
        if i % 2 == 0:
            y = mixer_ab(u, n_ctx, rope, ab_w_in[j], ab_w_out[j], ret_decay[j], ret_norm[j],
                         mlstm_conv[j], mlstm_gate_b[j], mlstm_norm[j])
        else:
            y = mixer_cd(u, n_ctx, cd_w_in[j], cd_w_out[j], hgrn_lower_bound(hgrn_lb, i), hgrn_norm[j],
                         rwkv_shift[j], rwkv_w0[j], rwkv_w2[j], rwkv_a0[j], rwkv_a2[j], rwkv_g2[j],
                         rwkv_kk_scale[j], rwkv_k_a[j], rwkv_r_k[j], rwkv_norm[j])
        h = h + m[2] * y
        vv = rms_norm(h, norm_ffn[i]) * (1.0 + m[4]) + m[3]
        f = moe_ffn(vv.reshape(-1, dm), router_w[i], router_b[i], i, exp_w_gate_up, exp_b_gate_up,
                    exp_w_down, exp_b_down).reshape(vv.shape)
        h = h + m[5] * f
    return rms_norm(h[:, n_ctx:], norm_final)
```

```python
import functools

import numpy as np
import jax
import jax.numpy as jnp
from jax import lax
from jax.experimental import pallas as pl
from jax.experimental.pallas import tpu as pltpu

F32 = jnp.float32
BF16 = jnp.bfloat16

CHUNK = 64
GRID_W = 64
EPS = 1e-6
ROPE_BASE = 10000.0
RET_HEADS, RET_DK, RET_DV = 4, 64, 128
ML_HEADS, ML_DK, ML_DV = 4, 64, 128
HG_HEADS, HG_DK, HG_DV = 4, 128, 128
RW_HEADS, RW_HD = 8, 64
RW_W_RANK, RW_A_RANK, RW_G_RANK = 64, 64, 128
N_EXPERTS, TOP_K = 32, 4
SWIGLU_LIMIT, SWIGLU_ALPHA = 7.0, 1.702

RET_QK, RET_W = RET_HEADS * RET_DK, RET_HEADS * RET_DV
ML_QK, ML_W = ML_HEADS * ML_DK, ML_HEADS * ML_DV
HG_K, HG_W = HG_HEADS * HG_DK, HG_HEADS * HG_DV
RW_W = RW_HEADS * RW_HD
AB_SPLITS = (RET_QK, RET_QK, RET_W, RET_W, 2 * ML_QK, ML_W, ML_W, 4 * ML_HEADS)
RWKV_SPLITS = (RW_W, RW_W, RW_W, 2 * RW_W_RANK, 2 * RW_A_RANK, RW_G_RANK)
RWKV_IN = sum(RWKV_SPLITS)
CD_SPLITS = (HG_K, 2 * HG_K, HG_W, HG_W, RWKV_IN)

RW_GROUP = 4
RW_GW = RW_GROUP * RW_HD
RW_BASE = 8
HG_LEVELS = 6

MM_ROWS = 256
MOE_ROWS = 512
VMEM_LIMIT = 56 * 1024 * 1024


def _split(z, sizes):
    return jnp.split(z, [int(s) for s in np.cumsum(sizes)[:-1]], axis=-1)


def _dot(a, b):
    return jnp.dot(a, b, preferred_element_type=F32)


def _dot_nt(a, b):
    return lax.dot_general(a, b, (((1,), (1,)), ((), ())), preferred_element_type=F32)


def _dot_tn(a, b):
    return lax.dot_general(a, b, (((0,), (0,)), ((), ())), preferred_element_type=F32)


def _mask_dot(m16, x):
    hi = x.astype(BF16)
    lo = (x - hi.astype(F32)).astype(BF16)
    return _dot(m16, hi) + _dot(m16, lo)


def _chunk_index(d, c, n_ctx_chunks, n_chunks):
    rev = jnp.where(c < n_ctx_chunks, n_ctx_chunks - 1 - c, n_chunks + n_ctx_chunks - 1 - c)
    return jnp.where(d == 0, c, rev)


def _mm_kernel(x_ref, w_ref, o_ref):
    o_ref[...] = _dot(x_ref[...], w_ref[...])


def matmul(x, w):
    n, k = x.shape
    m = w.shape[1]
    assert n % MM_ROWS == 0 and m % 128 == 0
    return pl.pallas_call(
        _mm_kernel,
        grid=(n // MM_ROWS,),
        in_specs=[pl.BlockSpec((MM_ROWS, k), lambda i: (i, 0)),
                  pl.BlockSpec((k, m), lambda i: (0, 0))],
        out_specs=pl.BlockSpec((MM_ROWS, m), lambda i: (i, 0)),
        out_shape=jax.ShapeDtypeStruct((n, m), F32),
        compiler_params=pltpu.CompilerParams(dimension_semantics=("parallel",),
                                             vmem_limit_bytes=VMEM_LIMIT),
        name="dense_proj",
    )(x, w)


def _ret_kernel(lg_ref, q_ref, k_ref, v_ref, o_ref, s_ref):
    d = pl.program_id(1)
    c = pl.program_id(2)

    @pl.when(c == 0)
    def _():
        s_ref[...] = jnp.zeros_like(s_ref)

    L = CHUNK
    ti = lax.broadcasted_iota(jnp.int32, (L, L), 0)
    si = lax.broadcasted_iota(jnp.int32, (L, L), 1)
    dist = jnp.where(d == 0, ti - si, si - ti)
    causal = dist >= 0
    distf = jnp.maximum(dist, 0).astype(F32)
    row = lax.broadcasted_iota(jnp.int32, (L, 1), 0)
    pos = jnp.where(d == 0, row, L - 1 - row).astype(F32)
    q = q_ref[0]
    k = k_ref[0]
    v = v_ref[0]
    for h in range(RET_HEADS):
        lg = lg_ref[d, h]
        qh = q[:, h * RET_DK:(h + 1) * RET_DK]
        kh = k[:, h * RET_DK:(h + 1) * RET_DK]
        vh = v[:, h * RET_DV:(h + 1) * RET_DV].astype(BF16)
        dm = jnp.where(causal, jnp.exp(lg * distf), 0.0)
        qd = (qh * jnp.exp(lg * (pos + 1.0))).astype(BF16)
        kd = (kh * jnp.exp(lg * (L - 1.0 - pos))).astype(BF16)
        att = _dot_nt(qh.astype(BF16), kh.astype(BF16)) * dm
        s = s_ref[h]
        o = _dot(qd, s.astype(BF16)) + _dot(att.astype(BF16), vh)
        o_ref[0, 0, :, h * RET_DV:(h + 1) * RET_DV] = o
        s_ref[h] = jnp.exp(lg * L) * s + _dot_tn(kd, vh)


def retention_scan(q, k, v, log_gamma, n_ctx_chunks):
    b, t, _ = q.shape
    n_chunks = t // CHUNK
    idx = functools.partial(_chunk_index, n_ctx_chunks=n_ctx_chunks, n_chunks=n_chunks)
    in_map = lambda bi, d, c, lg: (bi, idx(d, c), 0)
    return pl.pallas_call(
        _ret_kernel,
        grid_spec=pltpu.PrefetchScalarGridSpec(
            num_scalar_prefetch=1,
            grid=(b, 2, n_chunks),
            in_specs=[pl.BlockSpec((1, CHUNK, RET_QK), in_map),
                      pl.BlockSpec((1, CHUNK, RET_QK), in_map),
                      pl.BlockSpec((1, CHUNK, RET_W), in_map)],
            out_specs=pl.BlockSpec((1, 1, CHUNK, RET_W), lambda bi, d, c, lg: (d, bi, idx(d, c), 0)),
            scratch_shapes=[pltpu.VMEM((RET_HEADS, RET_DK, RET_DV), F32)]),
        out_shape=jax.ShapeDtypeStruct((2, b, t, RET_W), F32),
        compiler_params=pltpu.CompilerParams(
            dimension_semantics=("parallel", "parallel", "arbitrary"), vmem_limit_bytes=VMEM_LIMIT),
        name="retention_scan",
    )(log_gamma, q, k, v)


def _mlstm_kernel(q_ref, k_ref, v_ref, gc_ref, gr_ref, o_ref, c_ref, m_ref):
    d = pl.program_id(1)
    c = pl.program_id(2)

    @pl.when(c == 0)
    def _():
        c_ref[...] = jnp.zeros_like(c_ref)
        m_ref[...] = jnp.zeros_like(m_ref)

    L = CHUNK
    ti = lax.broadcasted_iota(jnp.int32, (L, L), 0)
    si = lax.broadcasted_iota(jnp.int32, (L, L), 1)
    dist = jnp.where(d == 0, ti - si, si - ti)
    le = dist >= 0
    le_t = dist <= 0
    lane = lax.broadcasted_iota(jnp.int32, (L, ML_DV), 1)
    ones_col = jnp.where(lane == 0, 1.0, 0.0).astype(BF16)
    q = q_ref[0]
    k = k_ref[0]
    v = v_ref[0]
    gc = gc_ref[0, 0]
    gr = gr_ref[0, 0, 0]
    H = ML_HEADS
    for h in range(H):
        i_col = gc[:, h:h + 1]
        f_col = gc[:, H + h:H + h + 1]
        i_row = gr[h:h + 1, :]
        f_row = gr[H + h:H + h + 1, :]
        b_col = jnp.sum(jnp.where(le, f_row, 0.0), axis=1, keepdims=True)
        b_row = jnp.sum(jnp.where(le_t, f_col, 0.0), axis=0, keepdims=True)
        b_last = jnp.sum(f_row, axis=1, keepdims=True)
        m = m_ref[h:h + 1, 0:1]
        log_d = jnp.where(le, b_col - b_row + i_row, -jnp.inf)
        inter = b_col + m
        m_row = jnp.maximum(inter, jnp.max(log_d, axis=1, keepdims=True))
        qh = q[:, h * ML_DK:(h + 1) * ML_DK].astype(BF16)
        kh = k[:, h * ML_DK:(h + 1) * ML_DK]
        vh = v[:, h * ML_DV:(h + 1) * ML_DV].astype(BF16)
        v_aug = jnp.concatenate([vh, ones_col], axis=1)
        s = _dot_nt(qh, kh.astype(BF16)) * jnp.exp(log_d - m_row)
        w_inter = jnp.exp(inter - m_row)
        c_aug = c_ref[h]
        num = w_inter * _dot(qh, c_aug.astype(BF16)) + _dot(s.astype(BF16), v_aug)
        den = num[:, ML_DV:ML_DV + 1]
        hh = num[:, :ML_DV] / jnp.maximum(jnp.abs(den), jnp.exp(-m_row))
        o_ref[0, 0, :, h * ML_DV:(h + 1) * ML_DV] = hh
        log_w_row = b_last - b_row + i_row
        m_new = jnp.maximum(b_last + m, jnp.max(log_w_row, axis=1, keepdims=True))
        w_col = jnp.exp(b_last - b_col + i_col - m_new)
        decay = jnp.exp(b_last + m - m_new)
        c_ref[h] = decay * c_aug + _dot_tn((kh * w_col).astype(BF16), v_aug)
        m_ref[h:h + 1, :] = jnp.broadcast_to(m_new, (1, 128))


def mlstm_scan(q, k, v, g_col, g_row, n_ctx_chunks):
    b, t, _ = q.shape
    n_chunks = t // CHUNK
    idx = functools.partial(_chunk_index, n_ctx_chunks=n_ctx_chunks, n_chunks=n_chunks)
    in_map = lambda bi, d, c: (bi, idx(d, c), 0)
    return pl.pallas_call(
        _mlstm_kernel,
        grid=(b, 2, n_chunks),
        in_specs=[pl.BlockSpec((1, CHUNK, ML_QK), in_map),
                  pl.BlockSpec((1, CHUNK, ML_QK), in_map),
                  pl.BlockSpec((1, CHUNK, ML_W), in_map),
                  pl.BlockSpec((1, 1, CHUNK, 2 * ML_HEADS), lambda bi, d, c: (d, bi, idx(d, c), 0)),
                  pl.BlockSpec((1, 1, 1, 2 * ML_HEADS, CHUNK), lambda bi, d, c: (d, bi, idx(d, c), 0, 0))],
        out_specs=pl.BlockSpec((1, 1, CHUNK, ML_W), lambda bi, d, c: (d, bi, idx(d, c), 0)),
        out_shape=jax.ShapeDtypeStruct((2, b, t, ML_W), F32),
        scratch_shapes=[pltpu.VMEM((ML_HEADS, ML_DK, 2 * ML_DV), F32), pltpu.VMEM((8, 128), F32)],
        compiler_params=pltpu.CompilerParams(
            dimension_semantics=("parallel", "parallel", "arbitrary"), vmem_limit_bytes=VMEM_LIMIT),
        name="mlstm_scan",
    )(q, k, v, g_col, g_row)


def _hgrn_masks():
    L = CHUNK
    stack = np.zeros((2, (1 + 2 * HG_LEVELS) * L, L), np.float32)
    pair = np.zeros((2, HG_LEVELS + 1, L, L), np.float32)
    for d in range(2):
        p = np.arange(L) if d == 0 else L - 1 - np.arange(L)
        pt, ps = p[:, None], p[None, :]
        stack[d, :L] = ps <= pt
        for l in range(HG_LEVELS):
            parent, half = p >> (l + 1), (p >> l) & 1
            split = parent * (2 << l) + (1 << l) - 1
            same = parent[:, None] == parent[None, :]
            e = same & (half[:, None] == 1) & (ps > split[:, None]) & (ps <= pt)
            f = same & (half[:, None] == 0) & (ps > pt) & (ps <= split[:, None])
            stack[d, (1 + 2 * l) * L:(2 + 2 * l) * L] = e
            stack[d, (2 + 2 * l) * L:(3 + 2 * l) * L] = f
            pair[d, l] = same & (half[:, None] == 1) & (half[None, :] == 0)
        pair[d, HG_LEVELS] = np.eye(L)
    return stack, pair


def _hgrn_kernel(q_ref, hf_ref, v_ref, lb_ref, stack_ref, pair_ref, o_ref, s_ref):
    c = pl.program_id(2)

    @pl.when(c == 0)
    def _():
        s_ref[...] = jnp.zeros_like(s_ref)

    L = CHUNK
    lb = lb_ref[...]
    f = lb + (1.0 - lb) * jax.nn.sigmoid(hf_ref[0])
    kk = 1.0 - f
    g = jnp.log(f)
    sums = _mask_dot(stack_ref[0], g)
    q = q_ref[0]
    v = v_ref[0]
    for h in range(HG_HEADS):
        hs = slice(h * HG_DK, (h + 1) * HG_DK)
        qh, kh, gh = q[:, hs], kk[:, hs], g[:, hs]
        vh = v[:, h * HG_DV:(h + 1) * HG_DV].astype(BF16)
        b = sums[0:L, hs]
        att = pair_ref[0, HG_LEVELS] * _dot_nt(qh.astype(BF16), kh.astype(BF16))
        for l in range(HG_LEVELS):
            e = sums[(1 + 2 * l) * L:(2 + 2 * l) * L, hs]
            fl = sums[(2 + 2 * l) * L:(3 + 2 * l) * L, hs]
            att = att + pair_ref[0, l] * _dot_nt((qh * jnp.exp(e)).astype(BF16), (kh * jnp.exp(fl)).astype(BF16))
        b_last = jnp.sum(gh, axis=0, keepdims=True)
        st = s_ref[h]
        o = _dot_nt((qh * jnp.exp(b)).astype(BF16), st.astype(BF16)) + _dot(att.astype(BF16), vh)
        o_ref[0, 0, :, h * HG_DV:(h + 1) * HG_DV] = o
        k_dec = (kh * jnp.exp(b_last - b)).astype(BF16)
        s_ref[h] = st * jnp.exp(b_last) + _dot_tn(vh, k_dec)


def hgrn_scan(q, hf, v, lb, n_ctx_chunks):
    b, t, _ = q.shape
    n_chunks = t // CHUNK
    stack, pair = _hgrn_masks()
    idx = functools.partial(_chunk_index, n_ctx_chunks=n_ctx_chunks, n_chunks=n_chunks)
    in_map = lambda bi, d, c: (bi, idx(d, c), 0)
    return pl.pallas_call(
        _hgrn_kernel,
        grid=(b, 2, n_chunks),
        in_specs=[pl.BlockSpec((1, CHUNK, HG_K), in_map),
                  pl.BlockSpec((1, CHUNK, HG_K), lambda bi, d, c: (bi, idx(d, c), d)),
                  pl.BlockSpec((1, CHUNK, HG_W), in_map),
                  pl.BlockSpec((1, HG_K), lambda bi, d, c: (0, 0)),
                  pl.BlockSpec((1,) + stack.shape[1:], lambda bi, d, c: (d, 0, 0)),
                  pl.BlockSpec((1,) + pair.shape[1:], lambda bi, d, c: (d, 0, 0, 0))],
        out_specs=pl.BlockSpec((1, 1, CHUNK, HG_W), lambda bi, d, c: (d, bi, idx(d, c), 0)),
        out_shape=jax.ShapeDtypeStruct((2, b, t, HG_W), F32),
        scratch_shapes=[pltpu.VMEM((HG_HEADS, HG_DV, HG_DK), F32)],
        compiler_params=pltpu.CompilerParams(
            dimension_semantics=("parallel", "parallel", "arbitrary"), vmem_limit_bytes=VMEM_LIMIT),
        name="hgrn_scan",
    )(q, hf, v, lb, jnp.asarray(stack, BF16), jnp.asarray(pair, F32))


def _rwkv_masks():
    L, G = CHUNK, RW_GROUP
    n = G * L
    head = np.arange(n) // L
    bd = (head[:, None] == head[None, :]).astype(np.float32)
    cum = np.zeros((2, L, L), np.float32)
    n_lvl = int(np.log2(L // RW_BASE))
    wide = np.zeros((2, 4 + n_lvl, L, n), np.float32)
    for d in range(2):
        p = np.arange(L) if d == 0 else L - 1 - np.arange(L)
        pt, ps = p[:, None], p[None, :]
        cum[d] = ps <= pt
        m = [ps < pt, ps <= pt, np.eye(L, dtype=bool), (pt // RW_BASE) == (ps // RW_BASE)]
        for i in range(n_lvl):
            blk = RW_BASE << i
            parent, half = p // (2 * blk), (p // blk) % 2
            m.append((parent[:, None] == parent[None, :]) & (half[:, None] == 1) & (half[None, :] == 0))
        for i, mi in enumerate(m):
            wide[d, i] = np.tile(mi, (1, G))
    return bd, cum, wide


def _unit_lower_inverse(n_w, wide_ref, expand):
    mul = lambda a_w, b_w: _dot(a_w.astype(BF16), expand(b_w))
    nd = n_w * wide_ref[0, 3]
    t_inv = wide_ref[0, 2] - nd
    pw = nd
    for _ in range(int(np.log2(RW_BASE)) - 1):
        pw = mul(pw, pw)
        t_inv = t_inv + mul(t_inv, pw)
    for i in range(wide_ref.shape[1] - 4):
        t_inv = t_inv - mul(mul(t_inv, n_w * wide_ref[0, 4 + i]), t_inv)
    return t_inv


def _rwkv_kernel(r_ref, k_ref, v_ref, kk_ref, a_ref, lw_ref, bd_ref, cum_ref, wide_ref, o_ref, s_ref):
    c = pl.program_id(2)

    @pl.when(c == 0)
    def _():
        s_ref[...] = jnp.zeros_like(s_ref)

    G = RW_GROUP
    bd = bd_ref[...]
    strict = wide_ref[0, 0]
    incl = wide_ref[0, 1]
    cum = cum_ref[0]

    def expand(x):
        return (jnp.concatenate([x] * G, axis=0) * bd).astype(BF16)

    for g in range(RW_HEADS // G):
        gs = slice(g * RW_GW, (g + 1) * RW_GW)
        r, k, v = r_ref[0, :, gs], k_ref[0, 0, :, gs], v_ref[0, :, gs]
        kk, a, lw = kk_ref[0, :, gs], a_ref[0, 0, :, gs], lw_ref[0, 0, :, gs]
        kb = kk * a
        cs = _mask_dot(cum, lw)
        c_last = jnp.sum(lw, axis=0, keepdims=True)
        p_inv = jnp.exp(-cs)
        p_to_end = jnp.exp(c_last - cs)
        kkd = (kk * jnp.exp(cs - lw)).astype(BF16)
        rp = (r * jnp.exp(cs)).astype(BF16)
        v_bd = expand(v)
        k_bd = expand(k * p_inv)
        b_bd = expand(kb * p_inv)
        n_w = strict * _dot_nt(kkd, b_bd)
        mk_w = (strict * _dot_nt(kkd, k_bd)).astype(BF16)
        mrk_w = (incl * _dot_nt(rp, k_bd)).astype(BF16)
        mrb_w = (incl * _dot_nt(rp, b_bd)).astype(BF16)
        st = s_ref[g]
        st16 = st.astype(BF16)
        t_inv = _unit_lower_inverse(n_w, wide_ref, expand).astype(BF16)
        u = _dot(t_inv, expand(_dot_nt(kkd, st16) + _dot(mk_w, v_bd)))
        o_ref[0, 0, :, gs] = _dot_nt(rp, st16) + _dot(mrk_w, v_bd) - _dot(mrb_w, expand(u))
        vu = jnp.concatenate([v, -u], axis=0).astype(BF16)
        kb_end = jnp.concatenate([k * p_to_end, kb * p_to_end], axis=0).astype(BF16)
        s_ref[g] = st * jnp.exp(c_last) + bd * _dot_tn(vu, kb_end)


def rwkv_scan(r, k_eff, v, kk, a, lw, n_ctx_chunks):
    b, t, _ = r.shape
    n_chunks = t // CHUNK
    bd, cum, wide = _rwkv_masks()
    idx = functools.partial(_chunk_index, n_ctx_chunks=n_ctx_chunks, n_chunks=n_chunks)
    shared = pl.BlockSpec((1, CHUNK, RW_W), lambda bi, d, c: (bi, idx(d, c), 0))
    per_dir = pl.BlockSpec((1, 1, CHUNK, RW_W), lambda bi, d, c: (d, bi, idx(d, c), 0))
    n = RW_GROUP * CHUNK
    return pl.pallas_call(
        _rwkv_kernel,
        grid=(b, 2, n_chunks),
        in_specs=[shared, per_dir, shared, shared, per_dir, per_dir,
                  pl.BlockSpec((n, RW_GW), lambda bi, d, c: (0, 0)),
                  pl.BlockSpec((1, CHUNK, CHUNK), lambda bi, d, c: (d, 0, 0)),
                  pl.BlockSpec((1,) + wide.shape[1:], lambda bi, d, c: (d, 0, 0, 0))],
        out_specs=per_dir,
        out_shape=jax.ShapeDtypeStruct((2, b, t, RW_W), F32),
        scratch_shapes=[pltpu.VMEM((RW_HEADS // RW_GROUP, RW_GW, RW_GW), F32)],
        compiler_params=pltpu.CompilerParams(
            dimension_semantics=("parallel", "parallel", "arbitrary"), vmem_limit_bytes=VMEM_LIMIT),
        name="rwkv7_scan",
    )(r, k_eff, v, kk, a, lw, jnp.asarray(bd, F32), jnp.asarray(cum, BF16), jnp.asarray(wide, F32))


def _moe_kernel(e_ref, x_ref, wgu_ref, bgu_ref, wdn_ref, bdn_ref, o_ref, wgu16_ref, wdn16_ref):
    i = pl.program_id(0)
    new_expert = jnp.logical_or(i == 0, e_ref[i] != e_ref[jnp.maximum(i - 1, 0)])

    @pl.when(new_expert)
    def _():
        wgu16_ref[...] = wgu_ref[0, 0].astype(BF16)
        wdn16_ref[...] = wdn_ref[0, 0].astype(BF16)

    d_ff = wdn16_ref.shape[0]
    gu = _dot(x_ref[...], wgu16_ref[...]) + bgu_ref[0, 0]
    glu = jnp.minimum(gu[:, :d_ff], SWIGLU_LIMIT)
    lin = jnp.clip(gu[:, d_ff:], -SWIGLU_LIMIT, SWIGLU_LIMIT)
    act = glu * jax.nn.sigmoid(SWIGLU_ALPHA * glu) * (lin + 1.0)
    o_ref[...] = _dot(act.astype(BF16), wdn16_ref[...]) + bdn_ref[0, 0]


def moe_experts(xb, blk_e, layer, w_gu, b_gu, w_dn, b_dn):
    n_rows, dm = xb.shape
    d_ff = w_dn.shape[2]
    n_blocks = n_rows // MOE_ROWS
    return pl.pallas_call(
        _moe_kernel,
        grid_spec=pltpu.PrefetchScalarGridSpec(
            num_scalar_prefetch=1,
            grid=(n_blocks,),
            in_specs=[pl.BlockSpec((MOE_ROWS, dm), lambda i, e: (i, 0)),
                      pl.BlockSpec((1, 1, dm, 2 * d_ff), lambda i, e: (layer, e[i], 0, 0)),
                      pl.BlockSpec((1, 1, 1, 2 * d_ff), lambda i, e: (layer, e[i], 0, 0)),
                      pl.BlockSpec((1, 1, d_ff, dm), lambda i, e: (layer, e[i], 0, 0)),
                      pl.BlockSpec((1, 1, 1, dm), lambda i, e: (layer, e[i], 0, 0))],
            out_specs=pl.BlockSpec((MOE_ROWS, dm), lambda i, e: (i, 0)),
            scratch_shapes=[pltpu.VMEM((dm, 2 * d_ff), BF16), pltpu.VMEM((d_ff, dm), BF16)]),
        out_shape=jax.ShapeDtypeStruct((n_rows, dm), F32),
        compiler_params=pltpu.CompilerParams(dimension_semantics=("arbitrary",),
                                             vmem_limit_bytes=VMEM_LIMIT),
        name="moe_experts",
    )(blk_e, xb, w_gu, b_gu, w_dn, b_dn)


def moe_ffn(t, router_w, router_b, layer, w_gu, b_gu, w_dn, b_dn):
    n, dm = t.shape
    logits = jnp.dot(t, router_w, precision=lax.Precision.HIGHEST) + router_b
    top_logit, top_e = lax.top_k(logits, TOP_K)
    top_w = jax.nn.softmax(top_logit, axis=-1)
    flat_e = top_e.reshape(-1).astype(jnp.int32)
    n_asg = n * TOP_K
    seg = 128
    assert n_asg % seg == 0
    onehot = (flat_e.reshape(n_asg // seg, seg, 1) == jnp.arange(N_EXPERTS, dtype=jnp.int32)).astype(F32)
    within = jnp.einsum('ts,bse->bte', jnp.tril(jnp.ones((seg, seg), F32)), onehot)
    seg_tot = within[:, -1, :]
    before = jnp.cumsum(seg_tot, axis=0) - seg_tot
    rank = (jnp.sum(onehot * (within + before[:, None, :]), axis=-1) - 1.0).astype(jnp.int32).reshape(-1)
    counts = jnp.sum(seg_tot, axis=0).astype(jnp.int32)
    padded = (counts + MOE_ROWS - 1) // MOE_ROWS * MOE_ROWS
    pad_end = jnp.cumsum(padded)
    pad_start = pad_end - padded
    grp_start = jnp.cumsum(counts) - counts
    n_blocks = -(-n_asg // MOE_ROWS) + N_EXPERTS
    n_rows = n_blocks * MOE_ROWS
    blk_e = jnp.minimum(jnp.searchsorted(pad_end, jnp.arange(n_blocks, dtype=jnp.int32) * MOE_ROWS, side='right'),
                        N_EXPERTS - 1).astype(jnp.int32)
    order = jnp.argsort(flat_e)
    e_row = jnp.repeat(blk_e, MOE_ROWS)
    j = jnp.arange(n_rows, dtype=jnp.int32) - pad_start[e_row]
    used = j < counts[e_row]
    row_tok = jnp.where(used, order[jnp.where(used, grp_start[e_row] + j, 0)] // TOP_K, 0)
    xb = t.astype(BF16)[row_tok]
    yb = moe_experts(xb, blk_e, layer, w_gu, b_gu[:, :, None, :], w_dn, b_dn[:, :, None, :])
    slot = (pad_start[flat_e] + rank).reshape(n, TOP_K)
    y = jnp.zeros((n, dm), F32)
    for j in range(TOP_K):
        y = y + yb[slot[:, j]] * top_w[:, j:j + 1]
    return y


def rms_norm(x, g):
    return x * lax.rsqrt(jnp.mean(x * x, axis=-1, keepdims=True) + EPS) * g


def head_rms_norm(o, g):
    return o * lax.rsqrt(jnp.mean(o * o, axis=-1, keepdims=True) + EPS) * g


def dwconv3(x, w, n_ctx):
    t = x.shape[1]
    xp = jnp.pad(x, ((0, 0), (1, 1), (0, 0)))
    pos = jnp.arange(t)[None, :, None]
    has_prev = (pos != 0) & (pos != n_ctx)
    has_next = (pos != n_ctx - 1) & (pos != t - 1)
    return (w[0] * jnp.where(has_prev, xp[:, :-2], 0.0) + w[1] * x
            + w[2] * jnp.where(has_next, xp[:, 2:], 0.0))


def rope_tables(n_ctx, seq):
    rows = seq // GRID_W
    pos_r = jnp.repeat(jnp.arange(rows, dtype=F32), GRID_W)
    pos_c = jnp.tile(jnp.arange(GRID_W, dtype=F32), rows)
    nf = RET_DK // 4
    inv = ROPE_BASE ** (-jnp.arange(nf, dtype=F32) / nf)
    ang = jnp.concatenate([pos_r[:, None] * inv, pos_c[:, None] * inv], axis=-1)
    cos = jnp.concatenate([jnp.ones((n_ctx, RET_DK // 2), F32), jnp.cos(ang)], axis=0)
    sin = jnp.concatenate([jnp.zeros((n_ctx, RET_DK // 2), F32), jnp.sin(ang)], axis=0)
    return cos, sin


def apply_rope(x, cos, sin):
    half = x.shape[-1] // 2
    x1, x2 = x[..., :half], x[..., half:]
    c, s = cos[None, :, None, :], sin[None, :, None, :]
    return jnp.concatenate([x1 * c - x2 * s, x1 * s + x2 * c], axis=-1)


def project(u, w):
    b, t, dm = u.shape
    m = w.shape[1]
    m_pad = -(-m // 128) * 128
    w16 = jnp.pad(w, ((0, 0), (0, m_pad - m))).astype(BF16)
    z = matmul(u.reshape(b * t, dm).astype(BF16), w16)
    return z[:, :m].reshape(b, t, m)


def mixer_ab(u, n_ctx, rope, w_in, w_out, ret_decay, ret_norm, ml_conv, ml_gate_b, ml_norm):
    b, t, _ = u.shape
    n_ctx_chunks = n_ctx // CHUNK
    rq, rk, rv, rg, mqk, mv, mo, mg = _split(project(u, w_in), AB_SPLITS)
    rq = apply_rope(rq.reshape(b, t, RET_HEADS, RET_DK), *rope).reshape(b, t, RET_QK)
    rk = apply_rope(rk.reshape(b, t, RET_HEADS, RET_DK) * RET_DK ** -0.5, *rope).reshape(b, t, RET_QK)
    log_gamma = jnp.log1p(-jnp.exp(ret_decay))
    ret = retention_scan(rq, rk, rv, log_gamma, n_ctx_chunks)
    ret = ret[0] + ret[1]

    mq, mk = _split(jax.nn.silu(dwconv3(mqk, ml_conv, n_ctx)), (ML_QK, ML_QK))
    mq = mq * ML_DK ** -0.5
    gates = (mg + ml_gate_b).reshape(b, t, 2, 2, ML_HEADS)
    gates = jnp.stack([gates[:, :, :, 0], jax.nn.log_sigmoid(gates[:, :, :, 1])], axis=3)
    g_col = jnp.moveaxis(gates, 2, 0).reshape(2, b, t, 2 * ML_HEADS)
    g_row = jnp.swapaxes(g_col.reshape(2, b, t // CHUNK, CHUNK, 2 * ML_HEADS), 3, 4)
    ml = mlstm_scan(mq, mk, mv, g_col, g_row, n_ctx_chunks)
    ml = ml[0] + ml[1]

    ret = head_rms_norm(ret.reshape(b, t, RET_HEADS, RET_DV), ret_norm).reshape(b, t, RET_W) * jax.nn.silu(rg)
    ml = head_rms_norm(ml.reshape(b, t, ML_HEADS, ML_DV), ml_norm).reshape(b, t, ML_W) * jax.nn.sigmoid(mo)
    return project(jnp.concatenate([ret, ml], axis=-1), w_out)


def mixer_cd(u, n_ctx, w_in, w_out, lb, hg_norm, rw_shift, rw_w0, rw_w2, rw_a0, rw_a2, rw_g2,
             rw_kk_scale, rw_k_a, rw_r_k, rw_norm):
    b, t, _ = u.shape
    n_ctx_chunks = n_ctx // CHUNK
    hq, hf, hi, hg, zr = _split(project(u, w_in), CD_SPLITS)
    hgo = hgrn_scan(hq, hf, hi, lb.reshape(1, HG_K), n_ctx_chunks)
    hgo = hgo[0] + hgo[1]

    r, k, v, wl, al, gl = _split(dwconv3(zr, rw_shift, n_ctx), RWKV_SPLITS)
    kk = (k * rw_kk_scale).reshape(b, t, RW_HEADS, RW_HD)
    kk = (kk * lax.rsqrt(jnp.sum(kk * kk, axis=-1, keepdims=True) + EPS)).reshape(b, t, RW_W)
    gate = jax.nn.sigmoid(gl) @ rw_g2
    k_eff, a_all, lw_all = [], [], []
    for d in range(2):
        wl_d = wl[..., d * RW_W_RANK:(d + 1) * RW_W_RANK]
        al_d = al[..., d * RW_A_RANK:(d + 1) * RW_A_RANK]
        w_log = -jax.nn.softplus(-(rw_w0[d] + jnp.tanh(wl_d) @ rw_w2[d])) - 0.5
        lw_all.append(-jnp.exp(w_log))
        a = jax.nn.sigmoid(rw_a0[d] + al_d @ rw_a2[d])
        a_all.append(a)
        k_eff.append(k * (1.0 + (a - 1.0) * rw_k_a))
    k_eff, a_all, lw_all = jnp.stack(k_eff), jnp.stack(a_all), jnp.stack(lw_all)
    rwo = rwkv_scan(r, k_eff, v, kk, a_all, lw_all, n_ctx_chunks)
    rwo = (rwo[0] + rwo[1]).reshape(b, t, RW_HEADS, RW_HD)

    r4, v4 = r.reshape(b, t, RW_HEADS, RW_HD), v.reshape(b, t, RW_HEADS, RW_HD)
    bonus = sum(jnp.sum(r4 * k_eff[d].reshape(b, t, RW_HEADS, RW_HD) * rw_r_k, axis=-1, keepdims=True) * v4
                for d in range(2))
    hgo = head_rms_norm(hgo.reshape(b, t, HG_HEADS, HG_DV), hg_norm).reshape(b, t, HG_W) * jax.nn.sigmoid(hg)
    rwo = (head_rms_norm(rwo, rw_norm) + bonus).reshape(b, t, RW_W) * gate
    return project(jnp.concatenate([hgo, rwo], axis=-1), w_out)


def hgrn_lower_bound(p, layer):
    sm = jax.nn.softmax(p, axis=0)
    return jnp.cumsum(sm, axis=0)[layer] - sm[0]


def kernel(x, c, ctx, c_ctx, mod_w, mod_b, norm_mix, norm_ffn, norm_final, ab_w_in, ab_w_out, ret_decay, ret_norm, mlstm_conv, mlstm_gate_b, mlstm_norm, cd_w_in, cd_w_out, hgrn_lb, hgrn_norm, rwkv_shift, rwkv_w0, rwkv_w2, rwkv_a0, rwkv_a2, rwkv_g2, rwkv_kk_scale, rwkv_k_a, rwkv_r_k, rwkv_norm, router_w, router_b, exp_w_gate_up, exp_b_gate_up, exp_w_down, exp_b_down):
    bsz, seq, dm = x.shape
    n_ctx = ctx.shape[1]
    depth = mod_w.shape[0]
    t = n_ctx + seq
    rope = rope_tables(n_ctx, seq)
    cond_l = jax.nn.silu(c)
    cond_c = jax.nn.silu(c_ctx)[None, :]
    h = jnp.concatenate([ctx, x], axis=1)
    hp = lax.Precision.HIGHEST

    is_ctx = (jnp.arange(t) < n_ctx)[None, :, None]

    def per_token(m_c, m_l):
        return jnp.where(is_ctx, m_c[:, None, :], m_l[:, None, :])

    for i in range(depth):
        j = i // 2
        m_l = jnp.split(jnp.dot(cond_l, mod_w[i], precision=hp) + mod_b[i], 6, axis=-1)
        m_c = jnp.split(jnp.dot(cond_c, mod_w[i], precision=hp) + mod_b[i], 6, axis=-1)
        m = [per_token(a, b_) for a, b_ in zip(m_c, m_l)]
        u = rms_norm(h, norm_mix[i]) * (1.0 + m[1]) + m[0]
        if i % 2 == 0:
            y = mixer_ab(u, n_ctx, rope, ab_w_in[j], ab_w_out[j], ret_decay[j], ret_norm[j],
                         mlstm_conv[j], mlstm_gate_b[j], mlstm_norm[j])
        else:
            y = mixer_cd(u, n_ctx, cd_w_in[j], cd_w_out[j], hgrn_lower_bound(hgrn_lb, i), hgrn_norm[j],
                         rwkv_shift[j], rwkv_w0[j], rwkv_w2[j], rwkv_a0[j], rwkv_a2[j], rwkv_g2[j],
                         rwkv_kk_scale[j], rwkv_k_a[j], rwkv_r_k[j], rwkv_norm[j])
        h = h + m[2] * y
        vv = rms_norm(h, norm_ffn[i]) * (1.0 + m[4]) + m[3]
        f = moe_ffn(vv.reshape(-1, dm), router_w[i], router_b[i], i, exp_w_gate_up, exp_b_gate_up,
                    exp_w_down, exp_b_down).reshape(vv.shape)
        h = h + m[5] * f
    return rms_norm(h[:, n_ctx:], norm_final)
```

```python
import functools

import numpy as np
import jax
import jax.numpy as jnp
from jax import lax
from jax.experimental import pallas as pl
from jax.experimental.pallas import tpu as pltpu

F32 = jnp.float32
BF16 = jnp.bfloat16

CHUNK = 64
GRID_W = 64
EPS = 1e-6
ROPE_BASE = 10000.0
RET_HEADS, RET_DK, RET_DV = 4, 64, 128
ML_HEADS, ML_DK, ML_DV = 4, 64, 128
HG_HEADS, HG_DK, HG_DV = 4, 128, 128
RW_HEADS, RW_HD = 8, 64
RW_W_RANK, RW_A_RANK, RW_G_RANK = 64, 64, 128
N_EXPERTS, TOP_K = 32, 4
SWIGLU_LIMIT, SWIGLU_ALPHA = 7.0, 1.702

RET_QK, RET_W = RET_HEADS * RET_DK, RET_HEADS * RET_DV
ML_QK, ML_W = ML_HEADS * ML_DK, ML_HEADS * ML_DV
HG_K, HG_W = HG_HEADS * HG_DK, HG_HEADS * HG_DV
RW_W = RW_HEADS * RW_HD
AB_SPLITS = (RET_QK, RET_QK, RET_W, RET_W, 2 * ML_QK, ML_W, ML_W, 4 * ML_HEADS)
RWKV_SPLITS = (RW_W, RW_W, RW_W, 2 * RW_W_RANK, 2 * RW_A_RANK, RW_G_RANK)
RWKV_IN = sum(RWKV_SPLITS)
CD_SPLITS = (HG_K, 2 * HG_K, HG_W, HG_W, RWKV_IN)

RW_GROUP = 4
RW_GW = RW_GROUP * RW_HD
RW_BASE = 8
HG_LEVELS = 6

SCAN_BATCH = 2
MM_ROWS = 256
MOE_ROWS = 512
VMEM_LIMIT = 56 * 1024 * 1024


def _split(z, sizes):
    return jnp.split(z, [int(s) for s in np.cumsum(sizes)[:-1]], axis=-1)


def _dot(a, b):
    return jnp.dot(a, b, preferred_element_type=F32)


def _dot_nt(a, b):
    return lax.dot_general(a, b, (((1,), (1,)), ((), ())), preferred_element_type=F32)


def _dot_tn(a, b):
    return lax.dot_general(a, b, (((0,), (0,)), ((), ())), preferred_element_type=F32)


def _mask_dot(m16, x):
    hi = x.astype(BF16)
    lo = (x - hi.astype(F32)).astype(BF16)
    return _dot(m16, hi) + _dot(m16, lo)


def _chunk_index(d, c, n_ctx_chunks, n_chunks):
    rev = jnp.where(c < n_ctx_chunks, n_ctx_chunks - 1 - c, n_chunks + n_ctx_chunks - 1 - c)
    return jnp.where(d == 0, c, rev)


def _mm_kernel(x_ref, w_ref, o_ref):
    o_ref[...] = _dot(x_ref[...], w_ref[...])


def matmul(x, w):
    n, k = x.shape
    m = w.shape[1]
    assert n % MM_ROWS == 0 and m % 128 == 0
    return pl.pallas_call(
        _mm_kernel,
        grid=(n // MM_ROWS,),
        in_specs=[pl.BlockSpec((MM_ROWS, k), lambda i: (i, 0)),
                  pl.BlockSpec((k, m), lambda i: (0, 0))],
        out_specs=pl.BlockSpec((MM_ROWS, m), lambda i: (i, 0)),
        out_shape=jax.ShapeDtypeStruct((n, m), F32),
        compiler_params=pltpu.CompilerParams(dimension_semantics=("parallel",),
                                             vmem_limit_bytes=VMEM_LIMIT),
        name="dense_proj",
    )(x, w)


def _ret_kernel(lg_ref, q_ref, k_ref, v_ref, o_ref, s_ref):
    d = pl.program_id(1)
    c = pl.program_id(2)

    @pl.when(c == 0)
    def _():
        s_ref[...] = jnp.zeros_like(s_ref)

    L = CHUNK
    ti = lax.broadcasted_iota(jnp.int32, (L, L), 0)
    si = lax.broadcasted_iota(jnp.int32, (L, L), 1)
    dist = jnp.where(d == 0, ti - si, si - ti)
    causal = dist >= 0
    distf = jnp.maximum(dist, 0).astype(F32)
    row = lax.broadcasted_iota(jnp.int32, (L, 1), 0)
    pos = jnp.where(d == 0, row, L - 1 - row).astype(F32)
    for h in range(RET_HEADS):
        lg = lg_ref[d, h]
        dm = jnp.where(causal, jnp.exp(lg * distf), 0.0)
        q_dec = jnp.exp(lg * (pos + 1.0))
        k_dec = jnp.exp(lg * (L - 1.0 - pos))
        for bi in range(SCAN_BATCH):
            qh = q_ref[bi, :, h * RET_DK:(h + 1) * RET_DK]
            kh = k_ref[bi, :, h * RET_DK:(h + 1) * RET_DK]
            vh = v_ref[bi, :, h * RET_DV:(h + 1) * RET_DV].astype(BF16)
            att = _dot_nt(qh.astype(BF16), kh.astype(BF16)) * dm
            s = s_ref[bi, h]
            o = _dot((qh * q_dec).astype(BF16), s.astype(BF16)) + _dot(att.astype(BF16), vh)
            o_ref[0, bi, :, h * RET_DV:(h + 1) * RET_DV] = o
            s_ref[bi, h] = jnp.exp(lg * L) * s + _dot_tn((kh * k_dec).astype(BF16), vh)


def retention_scan(q, k, v, log_gamma, n_ctx_chunks):
    b, t, _ = q.shape
    n_chunks = t // CHUNK
    idx = functools.partial(_chunk_index, n_ctx_chunks=n_ctx_chunks, n_chunks=n_chunks)
    in_map = lambda bi, d, c, lg: (bi, idx(d, c), 0)
    return pl.pallas_call(
        _ret_kernel,
        grid_spec=pltpu.PrefetchScalarGridSpec(
            num_scalar_prefetch=1,
            grid=(b // SCAN_BATCH, 2, n_chunks),
            in_specs=[pl.BlockSpec((SCAN_BATCH, CHUNK, RET_QK), in_map),
                      pl.BlockSpec((SCAN_BATCH, CHUNK, RET_QK), in_map),
                      pl.BlockSpec((SCAN_BATCH, CHUNK, RET_W), in_map)],
            out_specs=pl.BlockSpec((1, SCAN_BATCH, CHUNK, RET_W), lambda bi, d, c, lg: (d, bi, idx(d, c), 0)),
            scratch_shapes=[pltpu.VMEM((SCAN_BATCH, RET_HEADS, RET_DK, RET_DV), F32)]),
        out_shape=jax.ShapeDtypeStruct((2, b, t, RET_W), F32),
        compiler_params=pltpu.CompilerParams(
            dimension_semantics=("parallel", "parallel", "arbitrary"), vmem_limit_bytes=VMEM_LIMIT),
        name="retention_scan",
    )(log_gamma, q, k, v)


def _mlstm_kernel(q_ref, k_ref, v_ref, gc_ref, gr_ref, o_ref, c_ref, m_ref):
    d = pl.program_id(1)
    c = pl.program_id(2)

    @pl.when(c == 0)
    def _():
        c_ref[...] = jnp.zeros_like(c_ref)
        m_ref[...] = jnp.zeros_like(m_ref)

    L = CHUNK
    ti = lax.broadcasted_iota(jnp.int32, (L, L), 0)
    si = lax.broadcasted_iota(jnp.int32, (L, L), 1)
    dist = jnp.where(d == 0, ti - si, si - ti)
    le = dist >= 0
    le_t = dist <= 0
    lane = lax.broadcasted_iota(jnp.int32, (L, ML_DV), 1)
    ones_col = jnp.where(lane == 0, 1.0, 0.0).astype(BF16)
    H = ML_HEADS
    for bi, h in [(bi, h) for bi in range(SCAN_BATCH) for h in range(H)]:
        q = q_ref[bi]
        k = k_ref[bi]
        v = v_ref[bi]
        gc = gc_ref[0, bi]
        gr = gr_ref[0, bi, 0]
        i_col = gc[:, h:h + 1]
        f_col = gc[:, H + h:H + h + 1]
        i_row = gr[h:h + 1, :]
        f_row = gr[H + h:H + h + 1, :]
        b_col = jnp.sum(jnp.where(le, f_row, 0.0), axis=1, keepdims=True)
        b_row = jnp.sum(jnp.where(le_t, f_col, 0.0), axis=0, keepdims=True)
        b_last = jnp.sum(f_row, axis=1, keepdims=True)
        mrow = bi * H + h
        m = m_ref[mrow:mrow + 1, 0:1]
        log_d = jnp.where(le, b_col - b_row + i_row, -jnp.inf)
        inter = b_col + m
        m_row = jnp.maximum(inter, jnp.max(log_d, axis=1, keepdims=True))
        qh = q[:, h * ML_DK:(h + 1) * ML_DK].astype(BF16)
        kh = k[:, h * ML_DK:(h + 1) * ML_DK]
        vh = v[:, h * ML_DV:(h + 1) * ML_DV].astype(BF16)
        v_aug = jnp.concatenate([vh, ones_col], axis=1)
        s = _dot_nt(qh, kh.astype(BF16)) * jnp.exp(log_d - m_row)
        w_inter = jnp.exp(inter - m_row)
        c_aug = c_ref[bi, h]
        num = w_inter * _dot(qh, c_aug.astype(BF16)) + _dot(s.astype(BF16), v_aug)
        den = num[:, ML_DV:ML_DV + 1]
        hh = num[:, :ML_DV] / jnp.maximum(jnp.abs(den), jnp.exp(-m_row))
        o_ref[0, bi, :, h * ML_DV:(h + 1) * ML_DV] = hh
        log_w_row = b_last - b_row + i_row
        m_new = jnp.maximum(b_last + m, jnp.max(log_w_row, axis=1, keepdims=True))
        w_col = jnp.exp(b_last - b_col + i_col - m_new)
        decay = jnp.exp(b_last + m - m_new)
        c_ref[bi, h] = decay * c_aug + _dot_tn((kh * w_col).astype(BF16), v_aug)
        m_ref[mrow:mrow + 1, :] = jnp.broadcast_to(m_new, (1, 128))


def mlstm_scan(q, k, v, g_col, g_row, n_ctx_chunks):
    b, t, _ = q.shape
    n_chunks = t // CHUNK
    idx = functools.partial(_chunk_index, n_ctx_chunks=n_ctx_chunks, n_chunks=n_chunks)
    in_map = lambda bi, d, c: (bi, idx(d, c), 0)
    return pl.pallas_call(
        _mlstm_kernel,
        grid=(b // SCAN_BATCH, 2, n_chunks),
        in_specs=[pl.BlockSpec((SCAN_BATCH, CHUNK, ML_QK), in_map),
                  pl.BlockSpec((SCAN_BATCH, CHUNK, ML_QK), in_map),
                  pl.BlockSpec((SCAN_BATCH, CHUNK, ML_W), in_map),
                  pl.BlockSpec((1, SCAN_BATCH, CHUNK, 2 * ML_HEADS), lambda bi, d, c: (d, bi, idx(d, c), 0)),
                  pl.BlockSpec((1, SCAN_BATCH, 1, 2 * ML_HEADS, CHUNK), lambda bi, d, c: (d, bi, idx(d, c), 0, 0))],
        out_specs=pl.BlockSpec((1, SCAN_BATCH, CHUNK, ML_W), lambda bi, d, c: (d, bi, idx(d, c), 0)),
        out_shape=jax.ShapeDtypeStruct((2, b, t, ML_W), F32),
        scratch_shapes=[pltpu.VMEM((SCAN_BATCH, ML_HEADS, ML_DK, 2 * ML_DV), F32),
                        pltpu.VMEM((SCAN_BATCH * ML_HEADS, 128), F32)],
        compiler_params=pltpu.CompilerParams(
            dimension_semantics=("parallel", "parallel", "arbitrary"), vmem_limit_bytes=VMEM_LIMIT),
        name="mlstm_scan",
    )(q, k, v, g_col, g_row)


def _hgrn_masks():
    L = CHUNK
    stack = np.zeros((2, (1 + 2 * HG_LEVELS) * L, L), np.float32)
    pair = np.zeros((2, HG_LEVELS + 1, L, L), np.float32)
    for d in range(2):
        p = np.arange(L) if d == 0 else L - 1 - np.arange(L)
        pt, ps = p[:, None], p[None, :]
        stack[d, :L] = ps <= pt
        for l in range(HG_LEVELS):
            parent, half = p >> (l + 1), (p >> l) & 1
            split = parent * (2 << l) + (1 << l) - 1
            same = parent[:, None] == parent[None, :]
            e = same & (half[:, None] == 1) & (ps > split[:, None]) & (ps <= pt)
            f = same & (half[:, None] == 0) & (ps > pt) & (ps <= split[:, None])
            stack[d, (1 + 2 * l) * L:(2 + 2 * l) * L] = e
            stack[d, (2 + 2 * l) * L:(3 + 2 * l) * L] = f
            pair[d, l] = same & (half[:, None] == 1) & (half[None, :] == 0)
        pair[d, HG_LEVELS] = np.eye(L)
    return stack, pair


def _hgrn_kernel(q_ref, hf_ref, v_ref, lb_ref, stack_ref, pair_ref, o_ref, s_ref):
    c = pl.program_id(2)

    @pl.when(c == 0)
    def _():
        s_ref[...] = jnp.zeros_like(s_ref)

    L = CHUNK
    lb = lb_ref[...]
    f = lb + (1.0 - lb) * jax.nn.sigmoid(jnp.concatenate([hf_ref[bi] for bi in range(SCAN_BATCH)], axis=1))
    kk = 1.0 - f
    g = jnp.log(f)
    sums = _mask_dot(stack_ref[0], g)
    for bi, h in [(bi, h) for bi in range(SCAN_BATCH) for h in range(HG_HEADS)]:
        hs = slice(bi * HG_K + h * HG_DK, bi * HG_K + (h + 1) * HG_DK)
        qh, kh, gh = q_ref[bi, :, h * HG_DK:(h + 1) * HG_DK], kk[:, hs], g[:, hs]
        vh = v_ref[bi, :, h * HG_DV:(h + 1) * HG_DV].astype(BF16)
        b = sums[0:L, hs]
        att = pair_ref[0, HG_LEVELS] * _dot_nt(qh.astype(BF16), kh.astype(BF16))
        for l in range(HG_LEVELS):
            e = sums[(1 + 2 * l) * L:(2 + 2 * l) * L, hs]
            fl = sums[(2 + 2 * l) * L:(3 + 2 * l) * L, hs]
            att = att + pair_ref[0, l] * _dot_nt((qh * jnp.exp(e)).astype(BF16), (kh * jnp.exp(fl)).astype(BF16))
        b_last = jnp.sum(gh, axis=0, keepdims=True)
        st = s_ref[bi, h]
        o = _dot_nt((qh * jnp.exp(b)).astype(BF16), st.astype(BF16)) + _dot(att.astype(BF16), vh)
        o_ref[0, bi, :, h * HG_DV:(h + 1) * HG_DV] = o
        k_dec = (kh * jnp.exp(b_last - b)).astype(BF16)
        s_ref[bi, h] = st * jnp.exp(b_last) + _dot_tn(vh, k_dec)


def hgrn_scan(q, hf, v, lb, n_ctx_chunks):
    b, t, _ = q.shape
    n_chunks = t // CHUNK
    stack, pair = _hgrn_masks()
    idx = functools.partial(_chunk_index, n_ctx_chunks=n_ctx_chunks, n_chunks=n_chunks)
    in_map = lambda bi, d, c: (bi, idx(d, c), 0)
    return pl.pallas_call(
        _hgrn_kernel,
        grid=(b // SCAN_BATCH, 2, n_chunks),
        in_specs=[pl.BlockSpec((SCAN_BATCH, CHUNK, HG_K), in_map),
                  pl.BlockSpec((SCAN_BATCH, CHUNK, HG_K), lambda bi, d, c: (bi, idx(d, c), d)),
                  pl.BlockSpec((SCAN_BATCH, CHUNK, HG_W), in_map),
                  pl.BlockSpec((1, SCAN_BATCH * HG_K), lambda bi, d, c: (0, 0)),
                  pl.BlockSpec((1,) + stack.shape[1:], lambda bi, d, c: (d, 0, 0)),
                  pl.BlockSpec((1,) + pair.shape[1:], lambda bi, d, c: (d, 0, 0, 0))],
        out_specs=pl.BlockSpec((1, SCAN_BATCH, CHUNK, HG_W), lambda bi, d, c: (d, bi, idx(d, c), 0)),
        out_shape=jax.ShapeDtypeStruct((2, b, t, HG_W), F32),
        scratch_shapes=[pltpu.VMEM((SCAN_BATCH, HG_HEADS, HG_DV, HG_DK), F32)],
        compiler_params=pltpu.CompilerParams(
            dimension_semantics=("parallel", "parallel", "arbitrary"), vmem_limit_bytes=VMEM_LIMIT),
        name="hgrn_scan",
    )(q, hf, v, jnp.tile(lb, (1, SCAN_BATCH)), jnp.asarray(stack, BF16), jnp.asarray(pair, F32))


def _rwkv_masks():
    L, G = CHUNK, RW_GROUP
    n = G * L
    head = np.arange(n) // L
    bd = (head[:, None] == head[None, :]).astype(np.float32)
    cum = np.zeros((2, L, L), np.float32)
    n_lvl = int(np.log2(L // RW_BASE))
    wide = np.zeros((2, 4 + n_lvl, L, n), np.float32)
    for d in range(2):
        p = np.arange(L) if d == 0 else L - 1 - np.arange(L)
        pt, ps = p[:, None], p[None, :]
        cum[d] = ps <= pt
        m = [ps < pt, ps <= pt, np.eye(L, dtype=bool), (pt // RW_BASE) == (ps // RW_BASE)]
        for i in range(n_lvl):
            blk = RW_BASE << i
            parent, half = p // (2 * blk), (p // blk) % 2
            m.append((parent[:, None] == parent[None, :]) & (half[:, None] == 1) & (half[None, :] == 0))
        for i, mi in enumerate(m):
            wide[d, i] = np.tile(mi, (1, G))
    return bd, cum, wide


def _unit_lower_inverse(n_w, wide_ref, expand):
    mul = lambda a_w, b_w: _dot(a_w.astype(BF16), expand(b_w))
    nd = n_w * wide_ref[0, 3]
    t_inv = wide_ref[0, 2] - nd
    pw = nd
    for _ in range(int(np.log2(RW_BASE)) - 1):
        pw = mul(pw, pw)
        t_inv = t_inv + mul(t_inv, pw)
    for i in range(wide_ref.shape[1] - 4):
        t_inv = t_inv - mul(mul(t_inv, n_w * wide_ref[0, 4 + i]), t_inv)
    return t_inv


def _rwkv_kernel(r_ref, k_ref, v_ref, kk_ref, a_ref, lw_ref, bd_ref, cum_ref, wide_ref, o_ref, s_ref):
    c = pl.program_id(2)

    @pl.when(c == 0)
    def _():
        s_ref[...] = jnp.zeros_like(s_ref)

    G = RW_GROUP
    bd = bd_ref[...]
    strict = wide_ref[0, 0]
    incl = wide_ref[0, 1]
    cum = cum_ref[0]

    def expand(x):
        return (jnp.concatenate([x] * G, axis=0) * bd).astype(BF16)

    L = CHUNK
    for bi, g in [(bi, g) for bi in range(SCAN_BATCH) for g in range(RW_HEADS // G)]:
        gs = slice(g * RW_GW, (g + 1) * RW_GW)
        r, k, v = r_ref[bi, :, gs], k_ref[0, bi, :, gs], v_ref[bi, :, gs]
        kk, a, lw = kk_ref[bi, :, gs], a_ref[0, bi, :, gs], lw_ref[0, bi, :, gs]
        kb = kk * a
        cs = _mask_dot(cum, lw)
        c_last = jnp.sum(lw, axis=0, keepdims=True)
        p_inv = jnp.exp(-cs)
        p_to_end = jnp.exp(c_last - cs)
        kkd_rp = jnp.concatenate([kk * jnp.exp(cs - lw), r * jnp.exp(cs)], axis=0).astype(BF16)
        v_bd = expand(v)
        sc_k = _dot_nt(kkd_rp, expand(k * p_inv))
        sc_b = _dot_nt(kkd_rp, expand(kb * p_inv))
        n_w = strict * sc_b[:L]
        m_w = jnp.concatenate([strict * sc_k[:L], incl * sc_k[L:]], axis=0).astype(BF16)
        mrb_w = (incl * sc_b[L:]).astype(BF16)
        st = s_ref[bi, g]
        from_state = _dot_nt(kkd_rp, st.astype(BF16)) + _dot(m_w, v_bd)
        t_inv = _unit_lower_inverse(n_w, wide_ref, expand).astype(BF16)
        u = _dot(t_inv, expand(from_state[:L]))
        o_ref[0, bi, :, gs] = from_state[L:] - _dot(mrb_w, expand(u))
        vu = jnp.concatenate([v, -u], axis=0).astype(BF16)
        kb_end = jnp.concatenate([k * p_to_end, kb * p_to_end], axis=0).astype(BF16)
        s_ref[bi, g] = st * jnp.exp(c_last) + bd * _dot_tn(vu, kb_end)


def rwkv_scan(r, k_eff, v, kk, a, lw, n_ctx_chunks):
    b, t, _ = r.shape
    n_chunks = t // CHUNK
    assert b % SCAN_BATCH == 0
    bd, cum, wide = _rwkv_masks()
    idx = functools.partial(_chunk_index, n_ctx_chunks=n_ctx_chunks, n_chunks=n_chunks)
    shared = pl.BlockSpec((SCAN_BATCH, CHUNK, RW_W), lambda bi, d, c: (bi, idx(d, c), 0))
    per_dir = pl.BlockSpec((1, SCAN_BATCH, CHUNK, RW_W), lambda bi, d, c: (d, bi, idx(d, c), 0))
    n = RW_GROUP * CHUNK
    return pl.pallas_call(
        _rwkv_kernel,
        grid=(b // SCAN_BATCH, 2, n_chunks),
        in_specs=[shared, per_dir, shared, shared, per_dir, per_dir,
                  pl.BlockSpec((n, RW_GW), lambda bi, d, c: (0, 0)),
                  pl.BlockSpec((1, CHUNK, CHUNK), lambda bi, d, c: (d, 0, 0)),
                  pl.BlockSpec((1,) + wide.shape[1:], lambda bi, d, c: (d, 0, 0, 0))],
        out_specs=per_dir,
        out_shape=jax.ShapeDtypeStruct((2, b, t, RW_W), F32),
        scratch_shapes=[pltpu.VMEM((SCAN_BATCH, RW_HEADS // RW_GROUP, RW_GW, RW_GW), F32)],
        compiler_params=pltpu.CompilerParams(
            dimension_semantics=("parallel", "parallel", "arbitrary"), vmem_limit_bytes=VMEM_LIMIT),
        name="rwkv7_scan",
    )(r, k_eff, v, kk, a, lw, jnp.asarray(bd, F32), jnp.asarray(cum, BF16), jnp.asarray(wide, F32))


def _moe_kernel(e_ref, x_ref, wgu_ref, bgu_ref, wdn_ref, bdn_ref, o_ref, wgu16_ref, wdn16_ref):
    i = pl.program_id(0)
    new_expert = jnp.logical_or(i == 0, e_ref[i] != e_ref[jnp.maximum(i - 1, 0)])

    @pl.when(new_expert)
    def _():
        wgu16_ref[...] = wgu_ref[0, 0].astype(BF16)
        wdn16_ref[...] = wdn_ref[0, 0].astype(BF16)

    d_ff = wdn16_ref.shape[0]
    gu = _dot(x_ref[...], wgu16_ref[...]) + bgu_ref[0, 0]
    glu = jnp.minimum(gu[:, :d_ff], SWIGLU_LIMIT)
    lin = jnp.clip(gu[:, d_ff:], -SWIGLU_LIMIT, SWIGLU_LIMIT)
    act = glu * jax.nn.sigmoid(SWIGLU_ALPHA * glu) * (lin + 1.0)
    o_ref[...] = _dot(act.astype(BF16), wdn16_ref[...]) + bdn_ref[0, 0]


def moe_experts(xb, blk_e, layer, w_gu, b_gu, w_dn, b_dn):
    n_rows, dm = xb.shape
    d_ff = w_dn.shape[2]
    n_blocks = n_rows // MOE_ROWS
    return pl.pallas_call(
        _moe_kernel,
        grid_spec=pltpu.PrefetchScalarGridSpec(
            num_scalar_prefetch=1,
            grid=(n_blocks,),
            in_specs=[pl.BlockSpec((MOE_ROWS, dm), lambda i, e: (i, 0)),
                      pl.BlockSpec((1, 1, dm, 2 * d_ff), lambda i, e: (layer, e[i], 0, 0)),
                      pl.BlockSpec((1, 1, 1, 2 * d_ff), lambda i, e: (layer, e[i], 0, 0)),
                      pl.BlockSpec((1, 1, d_ff, dm), lambda i, e: (layer, e[i], 0, 0)),
                      pl.BlockSpec((1, 1, 1, dm), lambda i, e: (layer, e[i], 0, 0))],
            out_specs=pl.BlockSpec((MOE_ROWS, dm), lambda i, e: (i, 0)),
            scratch_shapes=[pltpu.VMEM((dm, 2 * d_ff), BF16), pltpu.VMEM((d_ff, dm), BF16)]),
        out_shape=jax.ShapeDtypeStruct((n_rows, dm), F32),
        compiler_params=pltpu.CompilerParams(dimension_semantics=("arbitrary",),
                                             vmem_limit_bytes=VMEM_LIMIT),
        name="moe_experts",
    )(blk_e, xb, w_gu, b_gu, w_dn, b_dn)


def moe_ffn(t, router_w, router_b, layer, w_gu, b_gu, w_dn, b_dn):
    n, dm = t.shape
    logits = jnp.dot(t, router_w, precision=lax.Precision.HIGHEST) + router_b
    top_logit, top_e = lax.top_k(logits, TOP_K)
    top_w = jax.nn.softmax(top_logit, axis=-1)
    flat_e = top_e.reshape(-1).astype(jnp.int32)
    n_asg = n * TOP_K
    seg = 128
    assert n_asg % seg == 0
    onehot = (flat_e.reshape(n_asg // seg, seg, 1) == jnp.arange(N_EXPERTS, dtype=jnp.int32)).astype(F32)
    within = jnp.einsum('ts,bse->bte', jnp.tril(jnp.ones((seg, seg), F32)), onehot)
    seg_tot = within[:, -1, :]
    before = jnp.cumsum(seg_tot, axis=0) - seg_tot
    rank = (jnp.sum(onehot * (within + before[:, None, :]), axis=-1) - 1.0).astype(jnp.int32).reshape(-1)
    counts = jnp.sum(seg_tot, axis=0).astype(jnp.int32)
    padded = (counts + MOE_ROWS - 1) // MOE_ROWS * MOE_ROWS
    pad_end = jnp.cumsum(padded)
    pad_start = pad_end - padded
    grp_start = jnp.cumsum(counts) - counts
    n_blocks = -(-n_asg // MOE_ROWS) + N_EXPERTS
    n_rows = n_blocks * MOE_ROWS
    blk_e = jnp.minimum(jnp.searchsorted(pad_end, jnp.arange(n_blocks, dtype=jnp.int32) * MOE_ROWS, side='right'),
                        N_EXPERTS - 1).astype(jnp.int32)
    order = jnp.argsort(flat_e)
    e_row = jnp.repeat(blk_e, MOE_ROWS)
    j = jnp.arange(n_rows, dtype=jnp.int32) - pad_start[e_row]
    used = j < counts[e_row]
    row_tok = jnp.where(used, order[jnp.where(used, grp_start[e_row] + j, 0)] // TOP_K, 0)
    xb = t.astype(BF16)[row_tok]
    yb = moe_experts(xb, blk_e, layer, w_gu, b_gu[:, :, None, :], w_dn, b_dn[:, :, None, :])
    slot = (pad_start[flat_e] + rank).reshape(n, TOP_K)
    y = jnp.zeros((n, dm), F32)
    for j in range(TOP_K):
        y = y + yb[slot[:, j]] * top_w[:, j:j + 1]
    return y


def rms_norm(x, g):
    return x * lax.rsqrt(jnp.mean(x * x, axis=-1, keepdims=True) + EPS) * g


def head_rms_norm(o, g):
    return o * lax.rsqrt(jnp.mean(o * o, axis=-1, keepdims=True) + EPS) * g


def dwconv3(x, w, n_ctx):
    t = x.shape[1]
    xp = jnp.pad(x, ((0, 0), (1, 1), (0, 0)))
    pos = jnp.arange(t)[None, :, None]
    has_prev = (pos != 0) & (pos != n_ctx)
    has_next = (pos != n_ctx - 1) & (pos != t - 1)
    return (w[0] * jnp.where(has_prev, xp[:, :-2], 0.0) + w[1] * x
            + w[2] * jnp.where(has_next, xp[:, 2:], 0.0))


def rope_tables(n_ctx, seq):
    rows = seq // GRID_W
    pos_r = jnp.repeat(jnp.arange(rows, dtype=F32), GRID_W)
    pos_c = jnp.tile(jnp.arange(GRID_W, dtype=F32), rows)
    nf = RET_DK // 4
    inv = ROPE_BASE ** (-jnp.arange(nf, dtype=F32) / nf)
    ang = jnp.concatenate([pos_r[:, None] * inv, pos_c[:, None] * inv], axis=-1)
    cos = jnp.concatenate([jnp.ones((n_ctx, RET_DK // 2), F32), jnp.cos(ang)], axis=0)
    sin = jnp.concatenate([jnp.zeros((n_ctx, RET_DK // 2), F32), jnp.sin(ang)], axis=0)
    return cos, sin


def apply_rope(x, cos, sin):
    half = x.shape[-1] // 2
    x1, x2 = x[..., :half], x[..., half:]
    c, s = cos[None, :, None, :], sin[None, :, None, :]
    return jnp.concatenate([x1 * c - x2 * s, x1 * s + x2 * c], axis=-1)


def project(u, w):
    b, t, dm = u.shape
    m = w.shape[1]
    m_pad = -(-m // 128) * 128
    w16 = jnp.pad(w, ((0, 0), (0, m_pad - m))).astype(BF16)
    z = matmul(u.reshape(b * t, dm).astype(BF16), w16)
    return z[:, :m].reshape(b, t, m)


def mixer_ab(u, n_ctx, rope, w_in, w_out, ret_decay, ret_norm, ml_conv, ml_gate_b, ml_norm):
    b, t, _ = u.shape
    n_ctx_chunks = n_ctx // CHUNK
    rq, rk, rv, rg, mqk, mv, mo, mg = _split(project(u, w_in), AB_SPLITS)
    rq = apply_rope(rq.reshape(b, t, RET_HEADS, RET_DK), *rope).reshape(b, t, RET_QK)
    rk = apply_rope(rk.reshape(b, t, RET_HEADS, RET_DK) * RET_DK ** -0.5, *rope).reshape(b, t, RET_QK)
    log_gamma = jnp.log1p(-jnp.exp(ret_decay))
    ret = retention_scan(rq, rk, rv, log_gamma, n_ctx_chunks)
    ret = ret[0] + ret[1]

    mq, mk = _split(jax.nn.silu(dwconv3(mqk, ml_conv, n_ctx)), (ML_QK, ML_QK))
    mq = mq * ML_DK ** -0.5
    gates = (mg + ml_gate_b).reshape(b, t, 2, 2, ML_HEADS)
    gates = jnp.stack([gates[:, :, :, 0], jax.nn.log_sigmoid(gates[:, :, :, 1])], axis=3)
    g_col = jnp.moveaxis(gates, 2, 0).reshape(2, b, t, 2 * ML_HEADS)
    g_row = jnp.swapaxes(g_col.reshape(2, b, t // CHUNK, CHUNK, 2 * ML_HEADS), 3, 4)
    ml = mlstm_scan(mq, mk, mv, g_col, g_row, n_ctx_chunks)
    ml = ml[0] + ml[1]

    ret = head_rms_norm(ret.reshape(b, t, RET_HEADS, RET_DV), ret_norm).reshape(b, t, RET_W) * jax.nn.silu(rg)
    ml = head_rms_norm(ml.reshape(b, t, ML_HEADS, ML_DV), ml_norm).reshape(b, t, ML_W) * jax.nn.sigmoid(mo)
    return project(jnp.concatenate([ret, ml], axis=-1), w_out)


def mixer_cd(u, n_ctx, w_in, w_out, lb, hg_norm, rw_shift, rw_w0, rw_w2, rw_a0, rw_a2, rw_g2,
             rw_kk_scale, rw_k_a, rw_r_k, rw_norm):
    b, t, _ = u.shape
    n_ctx_chunks = n_ctx // CHUNK
    hq, hf, hi, hg, zr = _split(project(u, w_in), CD_SPLITS)
    hgo = hgrn_scan(hq, hf, hi, lb.reshape(1, HG_K), n_ctx_chunks)
    hgo = hgo[0] + hgo[1]

    r, k, v, wl, al, gl = _split(dwconv3(zr, rw_shift, n_ctx), RWKV_SPLITS)
    kk = (k * rw_kk_scale).reshape(b, t, RW_HEADS, RW_HD)
    kk = (kk * lax.rsqrt(jnp.sum(kk * kk, axis=-1, keepdims=True) + EPS)).reshape(b, t, RW_W)
    gate = jax.nn.sigmoid(gl) @ rw_g2
    k_eff, a_all, lw_all = [], [], []
    for d in range(2):
        wl_d = wl[..., d * RW_W_RANK:(d + 1) * RW_W_RANK]
        al_d = al[..., d * RW_A_RANK:(d + 1) * RW_A_RANK]
        w_log = -jax.nn.softplus(-(rw_w0[d] + jnp.tanh(wl_d) @ rw_w2[d])) - 0.5
        lw_all.append(-jnp.exp(w_log))
        a = jax.nn.sigmoid(rw_a0[d] + al_d @ rw_a2[d])
        a_all.append(a)
        k_eff.append(k * (1.0 + (a - 1.0) * rw_k_a))
    k_eff, a_all, lw_all = jnp.stack(k_eff), jnp.stack(a_all), jnp.stack(lw_all)
    rwo = rwkv_scan(r, k_eff, v, kk, a_all, lw_all, n_ctx_chunks)
    rwo = (rwo[0] + rwo[1]).reshape(b, t, RW_HEADS, RW_HD)

    r4, v4 = r.reshape(b, t, RW_HEADS, RW_HD), v.reshape(b, t, RW_HEADS, RW_HD)
    bonus = sum(jnp.sum(r4 * k_eff[d].reshape(b, t, RW_HEADS, RW_HD) * rw_r_k, axis=-1, keepdims=True) * v4
                for d in range(2))
    hgo = head_rms_norm(hgo.reshape(b, t, HG_HEADS, HG_DV), hg_norm).reshape(b, t, HG_W) * jax.nn.sigmoid(hg)
    rwo = (head_rms_norm(rwo, rw_norm) + bonus).reshape(b, t, RW_W) * gate
    return project(jnp.concatenate([hgo, rwo], axis=-1), w_out)


def hgrn_lower_bound(p, layer):
    sm = jax.nn.softmax(p, axis=0)
    return jnp.cumsum(sm, axis=0)[layer] - sm[0]


def kernel(x, c, ctx, c_ctx, mod_w, mod_b, norm_mix, norm_ffn, norm_final, ab_w_in, ab_w_out, ret_decay, ret_norm, mlstm_conv, mlstm_gate_b, mlstm_norm, cd_w_in, cd_w_out, hgrn_lb, hgrn_norm, rwkv_shift, rwkv_w0, rwkv_w2, rwkv_a0, rwkv_a2, rwkv_g2, rwkv_kk_scale, rwkv_k_a, rwkv_r_k, rwkv_norm, router_w, router_b, exp_w_gate_up, exp_b_gate_up, exp_w_down, exp_b_down):
    bsz, seq, dm = x.shape
    n_ctx = ctx.shape[1]
    depth = mod_w.shape[0]
    t = n_ctx + seq
    rope = rope_tables(n_ctx, seq)
    cond_l = jax.nn.silu(c)
    cond_c = jax.nn.silu(c_ctx)[None, :]
    h = jnp.concatenate([ctx, x], axis=1)
    hp = lax.Precision.HIGHEST

    is_ctx = (jnp.arange(t) < n_ctx)[None, :, None]

    def per_token(m_c, m_l):
        return jnp.where(is_ctx, m_c[:, None, :], m_l[:, None, :])

    for i in range(depth):
        j = i // 2
        m_l = jnp.split(jnp.dot(cond_l, mod_w[i], precision=hp) + mod_b[i], 6, axis=-1)
        m_c = jnp.split(jnp.dot(cond_c, mod_w[i], precision=hp) + mod_b[i], 6, axis=-1)
        m = [per_token(a, b_) for a, b_ in zip(m_c, m_l)]
        u = rms_norm(h, norm_mix[i]) * (1.0 + m[1]) + m[0]
        if i % 2 == 0:
            y = mixer_ab(u, n_ctx, rope, ab_w_in[j], ab_w_out[j], ret_decay[j], ret_norm[j],
                         mlstm_conv[j], mlstm_gate_b[j], mlstm_norm[j])
        else:
            y = mixer_cd(u, n_ctx, cd_w_in[j], cd_w_out[j], hgrn_lower_bound(hgrn_lb, i), hgrn_norm[j],
                         rwkv_shift[j], rwkv_w0[j], rwkv_w2[j], rwkv_a0[j], rwkv_a2[j], rwkv_g2[j],
                         rwkv_kk_scale[j], rwkv_k_a[j], rwkv_r_k[j], rwkv_norm[j])
        h = h + m[2] * y
        vv = rms_norm(h, norm_ffn[i]) * (1.0 + m[4]) + m[3]
        f = moe_ffn(vv.reshape(-1, dm), router_w[i], router_b[i], i, exp_w_gate_up, exp_b_gate_up,
                    exp_w_down, exp_b_down).reshape(vv.shape)
        h = h + m[5] * f
    return rms_norm(h[:, n_ctx:], norm_final)
```

```python
import functools

import numpy as np
import jax
import jax.numpy as jnp
from jax import lax
from jax.experimental import pallas as pl
from jax.experimental.pallas import tpu as pltpu

F32 = jnp.float32
BF16 = jnp.bfloat16

CHUNK = 64
GRID_W = 64
EPS = 1e-6
ROPE_BASE = 10000.0
RET_HEADS, RET_DK, RET_DV = 4, 64, 128
ML_HEADS, ML_DK, ML_DV = 4, 64, 128
HG_HEADS, HG_DK, HG_DV = 4, 128, 128
RW_HEADS, RW_HD = 8, 64
RW_W_RANK, RW_A_RANK, RW_G_RANK = 64, 64, 128
N_EXPERTS, TOP_K = 32, 4
SWIGLU_LIMIT, SWIGLU_ALPHA = 7.0, 1.702

RET_QK, RET_W = RET_HEADS * RET_DK, RET_HEADS * RET_DV
ML_QK, ML_W = ML_HEADS * ML_DK, ML_HEADS * ML_DV
HG_K, HG_W = HG_HEADS * HG_DK, HG_HEADS * HG_DV
RW_W = RW_HEADS * RW_HD
AB_SPLITS = (RET_QK, RET_QK, RET_W, RET_W, 2 * ML_QK, ML_W, ML_W, 4 * ML_HEADS)
RWKV_SPLITS = (RW_W, RW_W, RW_W, 2 * RW_W_RANK, 2 * RW_A_RANK, RW_G_RANK)
RWKV_IN = sum(RWKV_SPLITS)
CD_SPLITS = (HG_K, 2 * HG_K, HG_W, HG_W, RWKV_IN)

RW_GROUP = 4
RW_GW = RW_GROUP * RW_HD
RW_BASE = 8
HG_LEVELS = 6

SCAN_BATCH = 2
MM_ROWS = 256
MOE_ROWS = 512
VMEM_LIMIT = 56 * 1024 * 1024


def _split(z, sizes):
    return jnp.split(z, [int(s) for s in np.cumsum(sizes)[:-1]], axis=-1)


def _dot(a, b):
    return jnp.dot(a, b, preferred_element_type=F32)


def _dot_nt(a, b):
    return lax.dot_general(a, b, (((1,), (1,)), ((), ())), preferred_element_type=F32)


def _dot_tn(a, b):
    return lax.dot_general(a, b, (((0,), (0,)), ((), ())), preferred_element_type=F32)


def _mask_dot(m16, x):
    hi = x.astype(BF16)
    lo = (x - hi.astype(F32)).astype(BF16)
    return _dot(m16, hi) + _dot(m16, lo)


def _chunk_index(d, c, n_ctx_chunks, n_chunks):
    rev = jnp.where(c < n_ctx_chunks, n_ctx_chunks - 1 - c, n_chunks + n_ctx_chunks - 1 - c)
    return jnp.where(d == 0, c, rev)


def _mm_kernel(x_ref, w_ref, o_ref):
    o_ref[...] = _dot(x_ref[...], w_ref[...])


def matmul(x, w):
    n, k = x.shape
    m = w.shape[1]
    assert n % MM_ROWS == 0 and m % 128 == 0
    return pl.pallas_call(
        _mm_kernel,
        grid=(n // MM_ROWS,),
        in_specs=[pl.BlockSpec((MM_ROWS, k), lambda i: (i, 0)),
                  pl.BlockSpec((k, m), lambda i: (0, 0))],
        out_specs=pl.BlockSpec((MM_ROWS, m), lambda i: (i, 0)),
        out_shape=jax.ShapeDtypeStruct((n, m), F32),
        compiler_params=pltpu.CompilerParams(dimension_semantics=("parallel",),
                                             vmem_limit_bytes=VMEM_LIMIT),
        name="dense_proj",
    )(x, w)


def _ret_kernel(lg_ref, q_ref, k_ref, v_ref, o_ref, s_ref):
    d = pl.program_id(1)
    c = pl.program_id(2)

    @pl.when(c == 0)
    def _():
        s_ref[...] = jnp.zeros_like(s_ref)

    L = CHUNK
    ti = lax.broadcasted_iota(jnp.int32, (L, L), 0)
    si = lax.broadcasted_iota(jnp.int32, (L, L), 1)
    dist = jnp.where(d == 0, ti - si, si - ti)
    causal = dist >= 0
    distf = jnp.maximum(dist, 0).astype(F32)
    row = lax.broadcasted_iota(jnp.int32, (L, 1), 0)
    pos = jnp.where(d == 0, row, L - 1 - row).astype(F32)
    for h in range(RET_HEADS):
        lg = lg_ref[d, h]
        dm = jnp.where(causal, jnp.exp(lg * distf), 0.0)
        q_dec = jnp.exp(lg * (pos + 1.0))
        k_dec = jnp.exp(lg * (L - 1.0 - pos))
        for bi in range(SCAN_BATCH):
            qh = q_ref[bi, :, h * RET_DK:(h + 1) * RET_DK]
            kh = k_ref[bi, :, h * RET_DK:(h + 1) * RET_DK]
            vh = v_ref[bi, :, h * RET_DV:(h + 1) * RET_DV].astype(BF16)
            att = _dot_nt(qh.astype(BF16), kh.astype(BF16)) * dm
            s = s_ref[bi, h]
            o = _dot((qh * q_dec).astype(BF16), s.astype(BF16)) + _dot(att.astype(BF16), vh)
            o_ref[0, bi, :, h * RET_DV:(h + 1) * RET_DV] = o
            s_ref[bi, h] = jnp.exp(lg * L) * s + _dot_tn((kh * k_dec).astype(BF16), vh)


def retention_scan(q, k, v, log_gamma, n_ctx_chunks):
    b, t, _ = q.shape
    n_chunks = t // CHUNK
    idx = functools.partial(_chunk_index, n_ctx_chunks=n_ctx_chunks, n_chunks=n_chunks)
    in_map = lambda bi, d, c, lg: (bi, idx(d, c), 0)
    return pl.pallas_call(
        _ret_kernel,
        grid_spec=pltpu.PrefetchScalarGridSpec(
            num_scalar_prefetch=1,
            grid=(b // SCAN_BATCH, 2, n_chunks),
            in_specs=[pl.BlockSpec((SCAN_BATCH, CHUNK, RET_QK), in_map),
                      pl.BlockSpec((SCAN_BATCH, CHUNK, RET_QK), in_map),
                      pl.BlockSpec((SCAN_BATCH, CHUNK, RET_W), in_map)],
            out_specs=pl.BlockSpec((1, SCAN_BATCH, CHUNK, RET_W), lambda bi, d, c, lg: (d, bi, idx(d, c), 0)),
            scratch_shapes=[pltpu.VMEM((SCAN_BATCH, RET_HEADS, RET_DK, RET_DV), F32)]),
        out_shape=jax.ShapeDtypeStruct((2, b, t, RET_W), F32),
        compiler_params=pltpu.CompilerParams(
            dimension_semantics=("parallel", "parallel", "arbitrary"), vmem_limit_bytes=VMEM_LIMIT),
        name="retention_scan",
    )(log_gamma, q, k, v)


def _mlstm_kernel(q_ref, k_ref, v_ref, gc_ref, gr_ref, o_ref, c_ref, m_ref):
    d = pl.program_id(1)
    c = pl.program_id(2)

    @pl.when(c == 0)
    def _():
        c_ref[...] = jnp.zeros_like(c_ref)
        m_ref[...] = jnp.zeros_like(m_ref)

    L = CHUNK
    ti = lax.broadcasted_iota(jnp.int32, (L, L), 0)
    si = lax.broadcasted_iota(jnp.int32, (L, L), 1)
    dist = jnp.where(d == 0, ti - si, si - ti)
    le = dist >= 0
    le_t = dist <= 0
    lane = lax.broadcasted_iota(jnp.int32, (L, ML_DV), 1)
    ones_col = jnp.where(lane == 0, 1.0, 0.0).astype(BF16)
    H = ML_HEADS
    for bi, h in [(bi, h) for bi in range(SCAN_BATCH) for h in range(H)]:
        q = q_ref[bi]
        k = k_ref[bi]
        v = v_ref[bi]
        gc = gc_ref[0, bi]
        gr = gr_ref[0, bi, 0]
        i_col = gc[:, h:h + 1]
        f_col = gc[:, H + h:H + h + 1]
        i_row = gr[h:h + 1, :]
        f_row = gr[H + h:H + h + 1, :]
        b_col = jnp.sum(jnp.where(le, f_row, 0.0), axis=1, keepdims=True)
        b_row = jnp.sum(jnp.where(le_t, f_col, 0.0), axis=0, keepdims=True)
        b_last = jnp.sum(f_row, axis=1, keepdims=True)
        mrow = bi * H + h
        m = m_ref[mrow:mrow + 1, 0:1]
        log_d = jnp.where(le, b_col - b_row + i_row, -jnp.inf)
        inter = b_col + m
        m_row = jnp.maximum(inter, jnp.max(log_d, axis=1, keepdims=True))
        qh = q[:, h * ML_DK:(h + 1) * ML_DK].astype(BF16)
        kh = k[:, h * ML_DK:(h + 1) * ML_DK]
        vh = v[:, h * ML_DV:(h + 1) * ML_DV].astype(BF16)
        v_aug = jnp.concatenate([vh, ones_col], axis=1)
        s = _dot_nt(qh, kh.astype(BF16)) * jnp.exp(log_d - m_row)
        w_inter = jnp.exp(inter - m_row)
        c_aug = c_ref[bi, h]
        num = w_inter * _dot(qh, c_aug.astype(BF16)) + _dot(s.astype(BF16), v_aug)
        den = num[:, ML_DV:ML_DV + 1]
        hh = num[:, :ML_DV] / jnp.maximum(jnp.abs(den), jnp.exp(-m_row))
        o_ref[0, bi, :, h * ML_DV:(h + 1) * ML_DV] = hh
        log_w_row = b_last - b_row + i_row
        m_new = jnp.maximum(b_last + m, jnp.max(log_w_row, axis=1, keepdims=True))
        w_col = jnp.exp(b_last - b_col + i_col - m_new)
        decay = jnp.exp(b_last + m - m_new)
        c_ref[bi, h] = decay * c_aug + _dot_tn((kh * w_col).astype(BF16), v_aug)
        m_ref[mrow:mrow + 1, :] = jnp.broadcast_to(m_new, (1, 128))


def mlstm_scan(q, k, v, g_col, g_row, n_ctx_chunks):
    b, t, _ = q.shape
    n_chunks = t // CHUNK
    idx = functools.partial(_chunk_index, n_ctx_chunks=n_ctx_chunks, n_chunks=n_chunks)
    in_map = lambda bi, d, c: (bi, idx(d, c), 0)
    return pl.pallas_call(
        _mlstm_kernel,
        grid=(b // SCAN_BATCH, 2, n_chunks),
        in_specs=[pl.BlockSpec((SCAN_BATCH, CHUNK, ML_QK), in_map),
                  pl.BlockSpec((SCAN_BATCH, CHUNK, ML_QK), in_map),
                  pl.BlockSpec((SCAN_BATCH, CHUNK, ML_W), in_map),
                  pl.BlockSpec((1, SCAN_BATCH, CHUNK, 2 * ML_HEADS), lambda bi, d, c: (d, bi, idx(d, c), 0)),
                  pl.BlockSpec((1, SCAN_BATCH, 1, 2 * ML_HEADS, CHUNK), lambda bi, d, c: (d, bi, idx(d, c), 0, 0))],
        out_specs=pl.BlockSpec((1, SCAN_BATCH, CHUNK, ML_W), lambda bi, d, c: (d, bi, idx(d, c), 0)),
        out_shape=jax.ShapeDtypeStruct((2, b, t, ML_W), F32),
        scratch_shapes=[pltpu.VMEM((SCAN_BATCH, ML_HEADS, ML_DK, 2 * ML_DV), F32),
                        pltpu.VMEM((SCAN_BATCH * ML_HEADS, 128), F32)],
        compiler_params=pltpu.CompilerParams(
            dimension_semantics=("parallel", "parallel", "arbitrary"), vmem_limit_bytes=VMEM_LIMIT),
        name="mlstm_scan",
    )(q, k, v, g_col, g_row)


def _hgrn_masks():
    L = CHUNK
    stack = np.zeros((2, (1 + 2 * HG_LEVELS) * L, L), np.float32)
    pair = np.zeros((2, HG_LEVELS + 1, L, L), np.float32)
    for d in range(2):
        p = np.arange(L) if d == 0 else L - 1 - np.arange(L)
        pt, ps = p[:, None], p[None, :]
        stack[d, :L] = ps <= pt
        for l in range(HG_LEVELS):
            parent, half = p >> (l + 1), (p >> l) & 1
            split = parent * (2 << l) + (1 << l) - 1
            same = parent[:, None] == parent[None, :]
            e = same & (half[:, None] == 1) & (ps > split[:, None]) & (ps <= pt)
            f = same & (half[:, None] == 0) & (ps > pt) & (ps <= split[:, None])
            stack[d, (1 + 2 * l) * L:(2 + 2 * l) * L] = e
            stack[d, (2 + 2 * l) * L:(3 + 2 * l) * L] = f
            pair[d, l] = same & (half[:, None] == 1) & (half[None, :] == 0)
        pair[d, HG_LEVELS] = np.eye(L)
    return stack, pair


def _hgrn_kernel(q_ref, hf_ref, v_ref, lb_ref, stack_ref, pair_ref, o_ref, s_ref):
    c = pl.program_id(2)

    @pl.when(c == 0)
    def _():
        s_ref[...] = jnp.zeros_like(s_ref)

    L = CHUNK
    lb = lb_ref[...]
    f = lb + (1.0 - lb) * jax.nn.sigmoid(jnp.concatenate([hf_ref[bi] for bi in range(SCAN_BATCH)], axis=1))
    kk = 1.0 - f
    g = jnp.log(f)
    sums = _mask_dot(stack_ref[0], g)
    for bi, h in [(bi, h) for bi in range(SCAN_BATCH) for h in range(HG_HEADS)]:
        hs = slice(bi * HG_K + h * HG_DK, bi * HG_K + (h + 1) * HG_DK)
        qh, kh, gh = q_ref[bi, :, h * HG_DK:(h + 1) * HG_DK], kk[:, hs], g[:, hs]
        vh = v_ref[bi, :, h * HG_DV:(h + 1) * HG_DV].astype(BF16)
        b = sums[0:L, hs]
        att = pair_ref[0, HG_LEVELS] * _dot_nt(qh.astype(BF16), kh.astype(BF16))
        for l in range(HG_LEVELS):
            e = sums[(1 + 2 * l) * L:(2 + 2 * l) * L, hs]
            fl = sums[(2 + 2 * l) * L:(3 + 2 * l) * L, hs]
            att = att + pair_ref[0, l] * _dot_nt((qh * jnp.exp(e)).astype(BF16), (kh * jnp.exp(fl)).astype(BF16))
        b_last = jnp.sum(gh, axis=0, keepdims=True)
        st = s_ref[bi, h]
        o = _dot_nt((qh * jnp.exp(b)).astype(BF16), st.astype(BF16)) + _dot(att.astype(BF16), vh)
        o_ref[0, bi, :, h * HG_DV:(h + 1) * HG_DV] = o
        k_dec = (kh * jnp.exp(b_last - b)).astype(BF16)
        s_ref[bi, h] = st * jnp.exp(b_last) + _dot_tn(vh, k_dec)


def hgrn_scan(q, hf, v, lb, n_ctx_chunks):
    b, t, _ = q.shape
    n_chunks = t // CHUNK
    stack, pair = _hgrn_masks()
    idx = functools.partial(_chunk_index, n_ctx_chunks=n_ctx_chunks, n_chunks=n_chunks)
    in_map = lambda bi, d, c: (bi, idx(d, c), 0)
    return pl.pallas_call(
        _hgrn_kernel,
        grid=(b // SCAN_BATCH, 2, n_chunks),
        in_specs=[pl.BlockSpec((SCAN_BATCH, CHUNK, HG_K), in_map),
                  pl.BlockSpec((SCAN_BATCH, CHUNK, HG_K), lambda bi, d, c: (bi, idx(d, c), d)),
                  pl.BlockSpec((SCAN_BATCH, CHUNK, HG_W), in_map),
                  pl.BlockSpec((1, SCAN_BATCH * HG_K), lambda bi, d, c: (0, 0)),
                  pl.BlockSpec((1,) + stack.shape[1:], lambda bi, d, c: (d, 0, 0)),
                  pl.BlockSpec((1,) + pair.shape[1:], lambda bi, d, c: (d, 0, 0, 0))],
        out_specs=pl.BlockSpec((1, SCAN_BATCH, CHUNK, HG_W), lambda bi, d, c: (d, bi, idx(d, c), 0)),
        out_shape=jax.ShapeDtypeStruct((2, b, t, HG_W), F32),
        scratch_shapes=[pltpu.VMEM((SCAN_BATCH, HG_HEADS, HG_DV, HG_DK), F32)],
        compiler_params=pltpu.CompilerParams(
            dimension_semantics=("parallel", "parallel", "arbitrary"), vmem_limit_bytes=VMEM_LIMIT),
        name="hgrn_scan",
    )(q, hf, v, jnp.tile(lb, (1, SCAN_BATCH)), jnp.asarray(stack, BF16), jnp.asarray(pair, F32))


def _rwkv_masks():
    L, G = CHUNK, RW_GROUP
    n = G * L
    head = np.arange(n) // L
    bd = (head[:, None] == head[None, :]).astype(np.float32)
    cum = np.zeros((2, L, L), np.float32)
    n_lvl = int(np.log2(L // RW_BASE))
    wide = np.zeros((2, 4 + n_lvl, L, n), np.float32)
    for d in range(2):
        p = np.arange(L) if d == 0 else L - 1 - np.arange(L)
        pt, ps = p[:, None], p[None, :]
        cum[d] = ps <= pt
        m = [ps < pt, ps <= pt, np.eye(L, dtype=bool), (pt // RW_BASE) == (ps // RW_BASE)]
        for i in range(n_lvl):
            blk = RW_BASE << i
            parent, half = p // (2 * blk), (p // blk) % 2
            m.append((parent[:, None] == parent[None, :]) & (half[:, None] == 1) & (half[None, :] == 0))
        for i, mi in enumerate(m):
            wide[d, i] = np.tile(mi, (1, G))
    return bd, cum, wide


def _unit_lower_inverse(n_w, wide_ref, expand):
    mul = lambda a_w, b_w: [_dot(x.astype(BF16), expand(y)) for x, y in zip(a_w, b_w)]
    nd = [n * wide_ref[0, 3] for n in n_w]
    t_inv = [wide_ref[0, 2] - x for x in nd]
    pw = nd
    for _ in range(int(np.log2(RW_BASE)) - 1):
        pw = mul(pw, pw)
        t_inv = [t + x for t, x in zip(t_inv, mul(t_inv, pw))]
    for i in range(wide_ref.shape[1] - 4):
        tc = mul(t_inv, [n * wide_ref[0, 4 + i] for n in n_w])
        t_inv = [t - x for t, x in zip(t_inv, mul(tc, t_inv))]
    return t_inv


def _rwkv_kernel(r_ref, k_ref, v_ref, kk_ref, a_ref, lw_ref, bd_ref, cum_ref, wide_ref, o_ref, s_ref):
    c = pl.program_id(2)

    @pl.when(c == 0)
    def _():
        s_ref[...] = jnp.zeros_like(s_ref)

    G = RW_GROUP
    bd = bd_ref[...]
    strict = wide_ref[0, 0]
    incl = wide_ref[0, 1]
    cum = cum_ref[0]

    def expand(x):
        return (jnp.concatenate([x] * G, axis=0) * bd).astype(BF16)

    L = CHUNK
    chains = [(bi, g) for bi in range(SCAN_BATCH) for g in range(RW_HEADS // G)]
    cols = [slice(g * RW_GW, (g + 1) * RW_GW) for _, g in chains]
    each = lambda fn, *lists: [fn(*xs) for xs in zip(*lists)]
    r = [r_ref[bi, :, cs_] for (bi, _), cs_ in zip(chains, cols)]
    k = [k_ref[0, bi, :, cs_] for (bi, _), cs_ in zip(chains, cols)]
    v = [v_ref[bi, :, cs_] for (bi, _), cs_ in zip(chains, cols)]
    kk = [kk_ref[bi, :, cs_] for (bi, _), cs_ in zip(chains, cols)]
    lw = [lw_ref[0, bi, :, cs_] for (bi, _), cs_ in zip(chains, cols)]
    kb = [x * a_ref[0, bi, :, cs_] for x, (bi, _), cs_ in zip(kk, chains, cols)]
    cs = each(lambda x: _mask_dot(cum, x), lw)
    c_last = each(lambda x: jnp.sum(x, axis=0, keepdims=True), lw)
    p_inv = each(lambda x: jnp.exp(-x), cs)
    p_to_end = each(lambda cl, x: jnp.exp(cl - x), c_last, cs)
    kkd_rp = each(lambda kk_, r_, cs_, lw_: jnp.concatenate(
        [kk_ * jnp.exp(cs_ - lw_), r_ * jnp.exp(cs_)], axis=0).astype(BF16), kk, r, cs, lw)
    v_bd = each(expand, v)
    sc_k = each(lambda q_, k_, p_: _dot_nt(q_, expand(k_ * p_)), kkd_rp, k, p_inv)
    sc_b = each(lambda q_, b_, p_: _dot_nt(q_, expand(b_ * p_)), kkd_rp, kb, p_inv)
    n_w = each(lambda x: strict * x[:L], sc_b)
    m_w = each(lambda x: jnp.concatenate([strict * x[:L], incl * x[L:]], axis=0).astype(BF16), sc_k)
    mrb_w = each(lambda x: (incl * x[L:]).astype(BF16), sc_b)
    st = [s_ref[bi, g] for bi, g in chains]
    from_state = each(lambda q_, s_, m_, vb_: _dot_nt(q_, s_.astype(BF16)) + _dot(m_, vb_),
                      kkd_rp, st, m_w, v_bd)
    t_inv = _unit_lower_inverse(n_w, wide_ref, expand)
    u = each(lambda t_, f_: _dot(t_.astype(BF16), expand(f_[:L])), t_inv, from_state)
    y = each(lambda f_, m_, u_: f_[L:] - _dot(m_, expand(u_)), from_state, mrb_w, u)
    for (bi, g), cs_, y_ in zip(chains, cols, y):
        o_ref[0, bi, :, cs_] = y_
    new_st = each(lambda s_, cl, v_, u_, k_, b_, p_: s_ * jnp.exp(cl) + bd * _dot_tn(
        jnp.concatenate([v_, -u_], axis=0).astype(BF16),
        jnp.concatenate([k_ * p_, b_ * p_], axis=0).astype(BF16)), st, c_last, v, u, k, kb, p_to_end)
    for (bi, g), s_ in zip(chains, new_st):
        s_ref[bi, g] = s_


def rwkv_scan(r, k_eff, v, kk, a, lw, n_ctx_chunks):
    b, t, _ = r.shape
    n_chunks = t // CHUNK
    assert b % SCAN_BATCH == 0
    bd, cum, wide = _rwkv_masks()
    idx = functools.partial(_chunk_index, n_ctx_chunks=n_ctx_chunks, n_chunks=n_chunks)
    shared = pl.BlockSpec((SCAN_BATCH, CHUNK, RW_W), lambda bi, d, c: (bi, idx(d, c), 0))
    per_dir = pl.BlockSpec((1, SCAN_BATCH, CHUNK, RW_W), lambda bi, d, c: (d, bi, idx(d, c), 0))
    n = RW_GROUP * CHUNK
    return pl.pallas_call(
        _rwkv_kernel,
        grid=(b // SCAN_BATCH, 2, n_chunks),
        in_specs=[shared, per_dir, shared, shared, per_dir, per_dir,
                  pl.BlockSpec((n, RW_GW), lambda bi, d, c: (0, 0)),
                  pl.BlockSpec((1, CHUNK, CHUNK), lambda bi, d, c: (d, 0, 0)),
                  pl.BlockSpec((1,) + wide.shape[1:], lambda bi, d, c: (d, 0, 0, 0))],
        out_specs=per_dir,
        out_shape=jax.ShapeDtypeStruct((2, b, t, RW_W), F32),
        scratch_shapes=[pltpu.VMEM((SCAN_BATCH, RW_HEADS // RW_GROUP, RW_GW, RW_GW), F32)],
        compiler_params=pltpu.CompilerParams(
            dimension_semantics=("parallel", "parallel", "arbitrary"), vmem_limit_bytes=VMEM_LIMIT),
        name="rwkv7_scan",
    )(r, k_eff, v, kk, a, lw, jnp.asarray(bd, F32), jnp.asarray(cum, BF16), jnp.asarray(wide, F32))


def _moe_kernel(e_ref, x_ref, wgu_ref, bgu_ref, wdn_ref, bdn_ref, o_ref, wgu16_ref, wdn16_ref):
    i = pl.program_id(0)
    new_expert = jnp.logical_or(i == 0, e_ref[i] != e_ref[jnp.maximum(i - 1, 0)])

    @pl.when(new_expert)
    def _():
        wgu16_ref[...] = wgu_ref[0, 0].astype(BF16)
        wdn16_ref[...] = wdn_ref[0, 0].astype(BF16)

    d_ff = wdn16_ref.shape[0]
    gu = _dot(x_ref[...], wgu16_ref[...]) + bgu_ref[0, 0]
    glu = jnp.minimum(gu[:, :d_ff], SWIGLU_LIMIT)
    lin = jnp.clip(gu[:, d_ff:], -SWIGLU_LIMIT, SWIGLU_LIMIT)
    act = glu * jax.nn.sigmoid(SWIGLU_ALPHA * glu) * (lin + 1.0)
    o_ref[...] = _dot(act.astype(BF16), wdn16_ref[...]) + bdn_ref[0, 0]


def moe_experts(xb, blk_e, layer, w_gu, b_gu, w_dn, b_dn):
    n_rows, dm = xb.shape
    d_ff = w_dn.shape[2]
    n_blocks = n_rows // MOE_ROWS
    return pl.pallas_call(
        _moe_kernel,
        grid_spec=pltpu.PrefetchScalarGridSpec(
            num_scalar_prefetch=1,
            grid=(n_blocks,),
            in_specs=[pl.BlockSpec((MOE_ROWS, dm), lambda i, e: (i, 0)),
                      pl.BlockSpec((1, 1, dm, 2 * d_ff), lambda i, e: (layer, e[i], 0, 0)),
                      pl.BlockSpec((1, 1, 1, 2 * d_ff), lambda i, e: (layer, e[i], 0, 0)),
                      pl.BlockSpec((1, 1, d_ff, dm), lambda i, e: (layer, e[i], 0, 0)),
                      pl.BlockSpec((1, 1, 1, dm), lambda i, e: (layer, e[i], 0, 0))],
            out_specs=pl.BlockSpec((MOE_ROWS, dm), lambda i, e: (i, 0)),
            scratch_shapes=[pltpu.VMEM((dm, 2 * d_ff), BF16), pltpu.VMEM((d_ff, dm), BF16)]),
        out_shape=jax.ShapeDtypeStruct((n_rows, dm), F32),
        compiler_params=pltpu.CompilerParams(dimension_semantics=("arbitrary",),
                                             vmem_limit_bytes=VMEM_LIMIT),
        name="moe_experts",
    )(blk_e, xb, w_gu, b_gu, w_dn, b_dn)


def moe_ffn(t, router_w, router_b, layer, w_gu, b_gu, w_dn, b_dn):
    n, dm = t.shape
    logits = jnp.dot(t, router_w, precision=lax.Precision.HIGHEST) + router_b
    top_logit, top_e = lax.top_k(logits, TOP_K)
    top_w = jax.nn.softmax(top_logit, axis=-1)
    flat_e = top_e.reshape(-1).astype(jnp.int32)
    n_asg = n * TOP_K
    seg = 128
    assert n_asg % seg == 0
    onehot = (flat_e.reshape(n_asg // seg, seg, 1) == jnp.arange(N_EXPERTS, dtype=jnp.int32)).astype(F32)
    within = jnp.einsum('ts,bse->bte', jnp.tril(jnp.ones((seg, seg), F32)), onehot)
    seg_tot = within[:, -1, :]
    before = jnp.cumsum(seg_tot, axis=0) - seg_tot
    rank = (jnp.sum(onehot * (within + before[:, None, :]), axis=-1) - 1.0).astype(jnp.int32).reshape(-1)
    counts = jnp.sum(seg_tot, axis=0).astype(jnp.int32)
    padded = (counts + MOE_ROWS - 1) // MOE_ROWS * MOE_ROWS
    pad_end = jnp.cumsum(padded)
    pad_start = pad_end - padded
    grp_start = jnp.cumsum(counts) - counts
    n_blocks = -(-n_asg // MOE_ROWS) + N_EXPERTS
    n_rows = n_blocks * MOE_ROWS
    blk_e = jnp.minimum(jnp.searchsorted(pad_end, jnp.arange(n_blocks, dtype=jnp.int32) * MOE_ROWS, side='right'),
                        N_EXPERTS - 1).astype(jnp.int32)
    order = jnp.argsort(flat_e)
    e_row = jnp.repeat(blk_e, MOE_ROWS)
    j = jnp.arange(n_rows, dtype=jnp.int32) - pad_start[e_row]
    used = j < counts[e_row]
    row_tok = jnp.where(used, order[jnp.where(used, grp_start[e_row] + j, 0)] // TOP_K, 0)
    xb = t.astype(BF16)[row_tok]
    yb = moe_experts(xb, blk_e, layer, w_gu, b_gu[:, :, None, :], w_dn, b_dn[:, :, None, :])
    slot = (pad_start[flat_e] + rank).reshape(n, TOP_K)
    y = jnp.zeros((n, dm), F32)
    for j in range(TOP_K):
        y = y + yb[slot[:, j]] * top_w[:, j:j + 1]
    return y


def rms_norm(x, g):
    return x * lax.rsqrt(jnp.mean(x * x, axis=-1, keepdims=True) + EPS) * g


def head_rms_norm(o, g):
    return o * lax.rsqrt(jnp.mean(o * o, axis=-1, keepdims=True) + EPS) * g


def dwconv3(x, w, n_ctx):
    t = x.shape[1]
    xp = jnp.pad(x, ((0, 0), (1, 1), (0, 0)))
    pos = jnp.arange(t)[None, :, None]
    has_prev = (pos != 0) & (pos != n_ctx)
    has_next = (pos != n_ctx - 1) & (pos != t - 1)
    return (w[0] * jnp.where(has_prev, xp[:, :-2], 0.0) + w[1] * x
            + w[2] * jnp.where(has_next, xp[:, 2:], 0.0))


def rope_tables(n_ctx, seq):
    rows = seq // GRID_W
    pos_r = jnp.repeat(jnp.arange(rows, dtype=F32), GRID_W)
    pos_c = jnp.tile(jnp.arange(GRID_W, dtype=F32), rows)
    nf = RET_DK // 4
    inv = ROPE_BASE ** (-jnp.arange(nf, dtype=F32) / nf)
    ang = jnp.concatenate([pos_r[:, None] * inv, pos_c[:, None] * inv], axis=-1)
    cos = jnp.concatenate([jnp.ones((n_ctx, RET_DK // 2), F32), jnp.cos(ang)], axis=0)
    sin = jnp.concatenate([jnp.zeros((n_ctx, RET_DK // 2), F32), jnp.sin(ang)], axis=0)
    return cos, sin


def apply_rope(x, cos, sin):
    half = x.shape[-1] // 2
    x1, x2 = x[..., :half], x[..., half:]
    c, s = cos[None, :, None, :], sin[None, :, None, :]
    return jnp.concatenate([x1 * c - x2 * s, x1 * s + x2 * c], axis=-1)


def project(u, w):
    b, t, dm = u.shape
    m = w.shape[1]
    m_pad = -(-m // 128) * 128
    w16 = jnp.pad(w, ((0, 0), (0, m_pad - m))).astype(BF16)
    z = matmul(u.reshape(b * t, dm).astype(BF16), w16)
    return z[:, :m].reshape(b, t, m)


def mixer_ab(u, n_ctx, rope, w_in, w_out, ret_decay, ret_norm, ml_conv, ml_gate_b, ml_norm):
    b, t, _ = u.shape
    n_ctx_chunks = n_ctx // CHUNK
    rq, rk, rv, rg, mqk, mv, mo, mg = _split(project(u, w_in), AB_SPLITS)
    rq = apply_rope(rq.reshape(b, t, RET_HEADS, RET_DK), *rope).reshape(b, t, RET_QK)
    rk = apply_rope(rk.reshape(b, t, RET_HEADS, RET_DK) * RET_DK ** -0.5, *rope).reshape(b, t, RET_QK)
    log_gamma = jnp.log1p(-jnp.exp(ret_decay))
    ret = retention_scan(rq, rk, rv, log_gamma, n_ctx_chunks)
    ret = ret[0] + ret[1]

    mq, mk = _split(jax.nn.silu(dwconv3(mqk, ml_conv, n_ctx)), (ML_QK, ML_QK))
    mq = mq * ML_DK ** -0.5
    gates = (mg + ml_gate_b).reshape(b, t, 2, 2, ML_HEADS)
    gates = jnp.stack([gates[:, :, :, 0], jax.nn.log_sigmoid(gates[:, :, :, 1])], axis=3)
    g_col = jnp.moveaxis(gates, 2, 0).reshape(2, b, t, 2 * ML_HEADS)
    g_row = jnp.swapaxes(g_col.reshape(2, b, t // CHUNK, CHUNK, 2 * ML_HEADS), 3, 4)
    ml = mlstm_scan(mq, mk, mv, g_col, g_row, n_ctx_chunks)
    ml = ml[0] + ml[1]

    ret = head_rms_norm(ret.reshape(b, t, RET_HEADS, RET_DV), ret_norm).reshape(b, t, RET_W) * jax.nn.silu(rg)
    ml = head_rms_norm(ml.reshape(b, t, ML_HEADS, ML_DV), ml_norm).reshape(b, t, ML_W) * jax.nn.sigmoid(mo)
    return project(jnp.concatenate([ret, ml], axis=-1), w_out)


def mixer_cd(u, n_ctx, w_in, w_out, lb, hg_norm, rw_shift, rw_w0, rw_w2, rw_a0, rw_a2, rw_g2,
             rw_kk_scale, rw_k_a, rw_r_k, rw_norm):
    b, t, _ = u.shape
    n_ctx_chunks = n_ctx // CHUNK
    hq, hf, hi, hg, zr = _split(project(u, w_in), CD_SPLITS)
    hgo = hgrn_scan(hq, hf, hi, lb.reshape(1, HG_K), n_ctx_chunks)
    hgo = hgo[0] + hgo[1]

    r, k, v, wl, al, gl = _split(dwconv3(zr, rw_shift, n_ctx), RWKV_SPLITS)
    kk = (k * rw_kk_scale).reshape(b, t, RW_HEADS, RW_HD)
    kk = (kk * lax.rsqrt(jnp.sum(kk * kk, axis=-1, keepdims=True) + EPS)).reshape(b, t, RW_W)
    gate = jax.nn.sigmoid(gl) @ rw_g2
    k_eff, a_all, lw_all = [], [], []
    for d in range(2):
        wl_d = wl[..., d * RW_W_RANK:(d + 1) * RW_W_RANK]
        al_d = al[..., d * RW_A_RANK:(d + 1) * RW_A_RANK]
        w_log = -jax.nn.softplus(-(rw_w0[d] + jnp.tanh(wl_d) @ rw_w2[d])) - 0.5
        lw_all.append(-jnp.exp(w_log))
        a = jax.nn.sigmoid(rw_a0[d] + al_d @ rw_a2[d])
        a_all.append(a)
        k_eff.append(k * (1.0 + (a - 1.0) * rw_k_a))
    k_eff, a_all, lw_all = jnp.stack(k_eff), jnp.stack(a_all), jnp.stack(lw_all)
    rwo = rwkv_scan(r, k_eff, v, kk, a_all, lw_all, n_ctx_chunks)
    rwo = (rwo[0] + rwo[1]).reshape(b, t, RW_HEADS, RW_HD)

    r4, v4 = r.reshape(b, t, RW_HEADS, RW_HD), v.reshape(b, t, RW_HEADS, RW_HD)
    bonus = sum(jnp.sum(r4 * k_eff[d].reshape(b, t, RW_HEADS, RW_HD) * rw_r_k, axis=-1, keepdims=True) * v4
                for d in range(2))
    hgo = head_rms_norm(hgo.reshape(b, t, HG_HEADS, HG_DV), hg_norm).reshape(b, t, HG_W) * jax.nn.sigmoid(hg)
    rwo = (head_rms_norm(rwo, rw_norm) + bonus).reshape(b, t, RW_W) * gate
    return project(jnp.concatenate([hgo, rwo], axis=-1), w_out)


def hgrn_lower_bound(p, layer):
    sm = jax.nn.softmax(p, axis=0)
    return jnp.cumsum(sm, axis=0)[layer] - sm[0]


def kernel(x, c, ctx, c_ctx, mod_w, mod_b, norm_mix, norm_ffn, norm_final, ab_w_in, ab_w_out, ret_decay, ret_norm, mlstm_conv, mlstm_gate_b, mlstm_norm, cd_w_in, cd_w_out, hgrn_lb, hgrn_norm, rwkv_shift, rwkv_w0, rwkv_w2, rwkv_a0, rwkv_a2, rwkv_g2, rwkv_kk_scale, rwkv_k_a, rwkv_r_k, rwkv_norm, router_w, router_b, exp_w_gate_up, exp_b_gate_up, exp_w_down, exp_b_down):
    bsz, seq, dm = x.shape
    n_ctx = ctx.shape[1]
    depth = mod_w.shape[0]
    t = n_ctx + seq
    rope = rope_tables(n_ctx, seq)
    cond_l = jax.nn.silu(c)
    cond_c = jax.nn.silu(c_ctx)[None, :]
    h = jnp.concatenate([ctx, x], axis=1)
    hp = lax.Precision.HIGHEST

    is_ctx = (jnp.arange(t) < n_ctx)[None, :, None]

    def per_token(m_c, m_l):
        return jnp.where(is_ctx, m_c[:, None, :], m_l[:, None, :])

    for i in range(depth):
        j = i // 2
        m_l = jnp.split(jnp.dot(cond_l, mod_w[i], precision=hp) + mod_b[i], 6, axis=-1)
        m_c = jnp.split(jnp.dot(cond_c, mod_w[i], precision=hp) + mod_b[i], 6, axis=-1)
        m = [per_token(a, b_) for a, b_ in zip(m_c, m_l)]
        u = rms_norm(h, norm_mix[i]) * (1.0 + m[1]) + m[0]
        if i % 2 == 0:
            y = mixer_ab(u, n_ctx, rope, ab_w_in[j], ab_w_out[j], ret_decay[j], ret_norm[j],
                         mlstm_conv[j], mlstm_gate_b[j], mlstm_norm[j])
        else:
            y = mixer_cd(u, n_ctx, cd_w_in[j], cd_w_out[j], hgrn_lower_bound(hgrn_lb, i), hgrn_norm[j],
                         rwkv_shift[j], rwkv_w0[j], rwkv_w2[j], rwkv_a0[j], rwkv_a2[j], rwkv_g2[j],
                         rwkv_kk_scale[j], rwkv_k_a[j], rwkv_r_k[j], rwkv_norm[j])
        h = h + m[2] * y
        vv = rms_norm(h, norm_ffn[i]) * (1.0 + m[4]) + m[3]
        f = moe_ffn(vv.reshape(-1, dm), router_w[i], router_b[i], i, exp_w_gate_up, exp_b_gate_up,
                    exp_w_down, exp_b_down).reshape(vv.shape)
        h = h + m[5] * f
    return rms_norm(h[:, n_ctx:], norm_final)
```

```python
import functools

import numpy as np
import jax
import jax.numpy as jnp
from jax import lax
from jax.experimental import pallas as pl
from jax.experimental.pallas import tpu as pltpu

F32 = jnp.float32
BF16 = jnp.bfloat16

CHUNK = 64
GRID_W = 64
EPS = 1e-6
ROPE_BASE = 10000.0
RET_HEADS, RET_DK, RET_DV = 4, 64, 128
ML_HEADS, ML_DK, ML_DV = 4, 64, 128
HG_HEADS, HG_DK, HG_DV = 4, 128, 128
RW_HEADS, RW_HD = 8, 64
RW_W_RANK, RW_A_RANK, RW_G_RANK = 64, 64, 128
N_EXPERTS, TOP_K = 32, 4
SWIGLU_LIMIT, SWIGLU_ALPHA = 7.0, 1.702

RET_QK, RET_W = RET_HEADS * RET_DK, RET_HEADS * RET_DV
ML_QK, ML_W = ML_HEADS * ML_DK, ML_HEADS * ML_DV
HG_K, HG_W = HG_HEADS * HG_DK, HG_HEADS * HG_DV
RW_W = RW_HEADS * RW_HD
AB_SPLITS = (RET_QK, RET_QK, RET_W, RET_W, 2 * ML_QK, ML_W, ML_W, 4 * ML_HEADS)
RWKV_SPLITS = (RW_W, RW_W, RW_W, 2 * RW_W_RANK, 2 * RW_A_RANK, RW_G_RANK)
RWKV_IN = sum(RWKV_SPLITS)
CD_SPLITS = (HG_K, 2 * HG_K, HG_W, HG_W, RWKV_IN)

RW_GROUP = 4
RW_GW = RW_GROUP * RW_HD
RW_BASE = 8
HG_LEVELS = 6

SCAN_BATCH = 2
MM_ROWS = 256
MOE_ROWS = 512
VMEM_LIMIT = 56 * 1024 * 1024


def _split(z, sizes):
    return jnp.split(z, [int(s) for s in np.cumsum(sizes)[:-1]], axis=-1)


def _dot(a, b):
    return jnp.dot(a, b, preferred_element_type=F32)


def _dot_nt(a, b):
    return lax.dot_general(a, b, (((1,), (1,)), ((), ())), preferred_element_type=F32)


def _dot_tn(a, b):
    return lax.dot_general(a, b, (((0,), (0,)), ((), ())), preferred_element_type=F32)


def _mask_dot(m16, x):
    hi = x.astype(BF16)
    lo = (x - hi.astype(F32)).astype(BF16)
    return _dot(m16, hi) + _dot(m16, lo)


def _chunk_index(d, c, n_ctx_chunks, n_chunks):
    rev = jnp.where(c < n_ctx_chunks, n_ctx_chunks - 1 - c, n_chunks + n_ctx_chunks - 1 - c)
    return jnp.where(d == 0, c, rev)


def _mm_kernel(x_ref, w_ref, o_ref):
    o_ref[...] = _dot(x_ref[...], w_ref[...])


def matmul(x, w):
    n, k = x.shape
    m = w.shape[1]
    assert n % MM_ROWS == 0 and m % 128 == 0
    return pl.pallas_call(
        _mm_kernel,
        grid=(n // MM_ROWS,),
        in_specs=[pl.BlockSpec((MM_ROWS, k), lambda i: (i, 0)),
                  pl.BlockSpec((k, m), lambda i: (0, 0))],
        out_specs=pl.BlockSpec((MM_ROWS, m), lambda i: (i, 0)),
        out_shape=jax.ShapeDtypeStruct((n, m), F32),
        compiler_params=pltpu.CompilerParams(dimension_semantics=("parallel",),
                                             vmem_limit_bytes=VMEM_LIMIT),
        name="dense_proj",
    )(x, w)


def _ret_kernel(lg_ref, q_ref, k_ref, v_ref, o_ref, s_ref):
    d = pl.program_id(1)
    c = pl.program_id(2)

    @pl.when(c == 0)
    def _():
        s_ref[...] = jnp.zeros_like(s_ref)

    L = CHUNK
    ti = lax.broadcasted_iota(jnp.int32, (L, L), 0)
    si = lax.broadcasted_iota(jnp.int32, (L, L), 1)
    dist = jnp.where(d == 0, ti - si, si - ti)
    causal = dist >= 0
    distf = jnp.maximum(dist, 0).astype(F32)
    row = lax.broadcasted_iota(jnp.int32, (L, 1), 0)
    pos = jnp.where(d == 0, row, L - 1 - row).astype(F32)
    lg = [lg_ref[d, h] for h in range(RET_HEADS)]
    dm = [jnp.where(causal, jnp.exp(x * distf), 0.0) for x in lg]
    q_dec = [jnp.exp(x * (pos + 1.0)) for x in lg]
    k_dec = [jnp.exp(x * (L - 1.0 - pos)) for x in lg]
    chains = [(bi, h) for bi in range(SCAN_BATCH) for h in range(RET_HEADS)]
    qh = [q_ref[bi, :, h * RET_DK:(h + 1) * RET_DK] for bi, h in chains]
    kh = [k_ref[bi, :, h * RET_DK:(h + 1) * RET_DK] for bi, h in chains]
    vh = [v_ref[bi, :, h * RET_DV:(h + 1) * RET_DV].astype(BF16) for bi, h in chains]
    att = [_dot_nt(q_.astype(BF16), k_.astype(BF16)) * dm[h] for q_, k_, (_, h) in zip(qh, kh, chains)]
    s = [s_ref[bi, h] for bi, h in chains]
    o = [_dot((q_ * q_dec[h]).astype(BF16), s_.astype(BF16)) + _dot(a_.astype(BF16), v_)
         for q_, s_, a_, v_, (_, h) in zip(qh, s, att, vh, chains)]
    for (bi, h), o_ in zip(chains, o):
        o_ref[0, bi, :, h * RET_DV:(h + 1) * RET_DV] = o_
    new_s = [jnp.exp(lg[h] * L) * s_ + _dot_tn((k_ * k_dec[h]).astype(BF16), v_)
             for s_, k_, v_, (_, h) in zip(s, kh, vh, chains)]
    for (bi, h), s_ in zip(chains, new_s):
        s_ref[bi, h] = s_


def retention_scan(q, k, v, log_gamma, n_ctx_chunks):
    b, t, _ = q.shape
    n_chunks = t // CHUNK
    idx = functools.partial(_chunk_index, n_ctx_chunks=n_ctx_chunks, n_chunks=n_chunks)
    in_map = lambda bi, d, c, lg: (bi, idx(d, c), 0)
    return pl.pallas_call(
        _ret_kernel,
        grid_spec=pltpu.PrefetchScalarGridSpec(
            num_scalar_prefetch=1,
            grid=(b // SCAN_BATCH, 2, n_chunks),
            in_specs=[pl.BlockSpec((SCAN_BATCH, CHUNK, RET_QK), in_map),
                      pl.BlockSpec((SCAN_BATCH, CHUNK, RET_QK), in_map),
                      pl.BlockSpec((SCAN_BATCH, CHUNK, RET_W), in_map)],
            out_specs=pl.BlockSpec((1, SCAN_BATCH, CHUNK, RET_W), lambda bi, d, c, lg: (d, bi, idx(d, c), 0)),
            scratch_shapes=[pltpu.VMEM((SCAN_BATCH, RET_HEADS, RET_DK, RET_DV), F32)]),
        out_shape=jax.ShapeDtypeStruct((2, b, t, RET_W), F32),
        compiler_params=pltpu.CompilerParams(
            dimension_semantics=("parallel", "parallel", "arbitrary"), vmem_limit_bytes=VMEM_LIMIT),
        name="retention_scan",
    )(log_gamma, q, k, v)


def _mlstm_kernel(q_ref, k_ref, v_ref, gc_ref, gr_ref, o_ref, c_ref, m_ref):
    d = pl.program_id(1)
    c = pl.program_id(2)

    @pl.when(c == 0)
    def _():
        c_ref[...] = jnp.zeros_like(c_ref)
        m_ref[...] = jnp.zeros_like(m_ref)

    L = CHUNK
    ti = lax.broadcasted_iota(jnp.int32, (L, L), 0)
    si = lax.broadcasted_iota(jnp.int32, (L, L), 1)
    dist = jnp.where(d == 0, ti - si, si - ti)
    le = dist >= 0
    le_t = dist <= 0
    lane = lax.broadcasted_iota(jnp.int32, (L, ML_DV), 1)
    ones_col = jnp.where(lane == 0, 1.0, 0.0).astype(BF16)
    H = ML_HEADS
    chains = [(bi, h) for bi in range(SCAN_BATCH) for h in range(H)]
    each = lambda fn, *lists: [fn(*xs) for xs in zip(*lists)]
    gc = [gc_ref[0, bi] for bi in range(SCAN_BATCH)]
    gr = [gr_ref[0, bi, 0] for bi in range(SCAN_BATCH)]
    i_col = [gc[bi][:, h:h + 1] for bi, h in chains]
    f_col = [gc[bi][:, H + h:H + h + 1] for bi, h in chains]
    i_row = [gr[bi][h:h + 1, :] for bi, h in chains]
    f_row = [gr[bi][H + h:H + h + 1, :] for bi, h in chains]
    b_col = each(lambda f_: jnp.sum(jnp.where(le, f_, 0.0), axis=1, keepdims=True), f_row)
    b_row = each(lambda f_: jnp.sum(jnp.where(le_t, f_, 0.0), axis=0, keepdims=True), f_col)
    b_last = each(lambda f_: jnp.sum(f_, axis=1, keepdims=True), f_row)
    m = [m_ref[bi * H + h:bi * H + h + 1, 0:1] for bi, h in chains]
    log_d = each(lambda bc, br, ir: jnp.where(le, bc - br + ir, -jnp.inf), b_col, b_row, i_row)
    inter = each(lambda bc, m_: bc + m_, b_col, m)
    m_row = each(lambda in_, ld: jnp.maximum(in_, jnp.max(ld, axis=1, keepdims=True)), inter, log_d)
    qh = [q_ref[bi, :, h * ML_DK:(h + 1) * ML_DK].astype(BF16) for bi, h in chains]
    kh = [k_ref[bi, :, h * ML_DK:(h + 1) * ML_DK] for bi, h in chains]
    v_aug = [jnp.concatenate([v_ref[bi, :, h * ML_DV:(h + 1) * ML_DV].astype(BF16), ones_col], axis=1)
             for bi, h in chains]
    s = each(lambda q_, k_, ld, mr: _dot_nt(q_, k_.astype(BF16)) * jnp.exp(ld - mr), qh, kh, log_d, m_row)
    c_aug = [c_ref[bi, h] for bi, h in chains]
    num = each(lambda in_, mr, q_, c_, s_, v_: jnp.exp(in_ - mr) * _dot(q_, c_.astype(BF16))
               + _dot(s_.astype(BF16), v_), inter, m_row, qh, c_aug, s, v_aug)
    hh = each(lambda n_, mr: n_[:, :ML_DV] / jnp.maximum(jnp.abs(n_[:, ML_DV:ML_DV + 1]), jnp.exp(-mr)), num, m_row)
    for (bi, h), h_ in zip(chains, hh):
        o_ref[0, bi, :, h * ML_DV:(h + 1) * ML_DV] = h_
    m_new = each(lambda bl, m_, br, ir: jnp.maximum(bl + m_, jnp.max(bl - br + ir, axis=1, keepdims=True)),
                 b_last, m, b_row, i_row)
    new_c = each(lambda bl, m_, mn, c_, k_, bc, ic, v_: jnp.exp(bl + m_ - mn) * c_ + _dot_tn(
        (k_ * jnp.exp(bl - bc + ic - mn)).astype(BF16), v_), b_last, m, m_new, c_aug, kh, b_col, i_col, v_aug)
    for (bi, h), c_, mn in zip(chains, new_c, m_new):
        c_ref[bi, h] = c_
        m_ref[bi * H + h:bi * H + h + 1, :] = jnp.broadcast_to(mn, (1, 128))


def mlstm_scan(q, k, v, g_col, g_row, n_ctx_chunks):
    b, t, _ = q.shape
    n_chunks = t // CHUNK
    idx = functools.partial(_chunk_index, n_ctx_chunks=n_ctx_chunks, n_chunks=n_chunks)
    in_map = lambda bi, d, c: (bi, idx(d, c), 0)
    return pl.pallas_call(
        _mlstm_kernel,
        grid=(b // SCAN_BATCH, 2, n_chunks),
        in_specs=[pl.BlockSpec((SCAN_BATCH, CHUNK, ML_QK), in_map),
                  pl.BlockSpec((SCAN_BATCH, CHUNK, ML_QK), in_map),
                  pl.BlockSpec((SCAN_BATCH, CHUNK, ML_W), in_map),
                  pl.BlockSpec((1, SCAN_BATCH, CHUNK, 2 * ML_HEADS), lambda bi, d, c: (d, bi, idx(d, c), 0)),
                  pl.BlockSpec((1, SCAN_BATCH, 1, 2 * ML_HEADS, CHUNK), lambda bi, d, c: (d, bi, idx(d, c), 0, 0))],
        out_specs=pl.BlockSpec((1, SCAN_BATCH, CHUNK, ML_W), lambda bi, d, c: (d, bi, idx(d, c), 0)),
        out_shape=jax.ShapeDtypeStruct((2, b, t, ML_W), F32),
        scratch_shapes=[pltpu.VMEM((SCAN_BATCH, ML_HEADS, ML_DK, 2 * ML_DV), F32),
                        pltpu.VMEM((SCAN_BATCH * ML_HEADS, 128), F32)],
        compiler_params=pltpu.CompilerParams(
            dimension_semantics=("parallel", "parallel", "arbitrary"), vmem_limit_bytes=VMEM_LIMIT),
        name="mlstm_scan",
    )(q, k, v, g_col, g_row)


def _hgrn_masks():
    L = CHUNK
    stack = np.zeros((2, (1 + 2 * HG_LEVELS) * L, L), np.float32)
    pair = np.zeros((2, HG_LEVELS + 1, L, L), np.float32)
    for d in range(2):
        p = np.arange(L) if d == 0 else L - 1 - np.arange(L)
        pt, ps = p[:, None], p[None, :]
        stack[d, :L] = ps <= pt
        for l in range(HG_LEVELS):
            parent, half = p >> (l + 1), (p >> l) & 1
            split = parent * (2 << l) + (1 << l) - 1
            same = parent[:, None] == parent[None, :]
            e = same & (half[:, None] == 1) & (ps > split[:, None]) & (ps <= pt)
            f = same & (half[:, None] == 0) & (ps > pt) & (ps <= split[:, None])
            stack[d, (1 + 2 * l) * L:(2 + 2 * l) * L] = e
            stack[d, (2 + 2 * l) * L:(3 + 2 * l) * L] = f
            pair[d, l] = same & (half[:, None] == 1) & (half[None, :] == 0)
        pair[d, HG_LEVELS] = np.eye(L)
    return stack, pair


def _hgrn_kernel(q_ref, hf_ref, v_ref, lb_ref, stack_ref, pair_ref, o_ref, s_ref):
    c = pl.program_id(2)

    @pl.when(c == 0)
    def _():
        s_ref[...] = jnp.zeros_like(s_ref)

    L = CHUNK
    lb = lb_ref[...]
    f = lb + (1.0 - lb) * jax.nn.sigmoid(jnp.concatenate([hf_ref[bi] for bi in range(SCAN_BATCH)], axis=1))
    kk = 1.0 - f
    g = jnp.log(f)
    sums = _mask_dot(stack_ref[0], g)
    chains = [(bi, h) for bi in range(SCAN_BATCH) for h in range(HG_HEADS)]
    each = lambda fn, *lists: [fn(*xs) for xs in zip(*lists)]
    hs = [slice(bi * HG_K + h * HG_DK, bi * HG_K + (h + 1) * HG_DK) for bi, h in chains]
    qh = [q_ref[bi, :, h * HG_DK:(h + 1) * HG_DK] for bi, h in chains]
    kh = [kk[:, s_] for s_ in hs]
    vh = [v_ref[bi, :, h * HG_DV:(h + 1) * HG_DV].astype(BF16) for bi, h in chains]
    b = [sums[0:L, s_] for s_ in hs]
    att = each(lambda q_, k_: pair_ref[0, HG_LEVELS] * _dot_nt(q_.astype(BF16), k_.astype(BF16)), qh, kh)
    for l in range(HG_LEVELS):
        e = [sums[(1 + 2 * l) * L:(2 + 2 * l) * L, s_] for s_ in hs]
        fl = [sums[(2 + 2 * l) * L:(3 + 2 * l) * L, s_] for s_ in hs]
        att = each(lambda a_, q_, k_, e_, f_: a_ + pair_ref[0, l] * _dot_nt(
            (q_ * jnp.exp(e_)).astype(BF16), (k_ * jnp.exp(f_)).astype(BF16)), att, qh, kh, e, fl)
    b_last = [jnp.sum(g[:, s_], axis=0, keepdims=True) for s_ in hs]
    st = [s_ref[bi, h] for bi, h in chains]
    o = each(lambda q_, b_, s_, a_, v_: _dot_nt((q_ * jnp.exp(b_)).astype(BF16), s_.astype(BF16))
             + _dot(a_.astype(BF16), v_), qh, b, st, att, vh)
    for (bi, h), o_ in zip(chains, o):
        o_ref[0, bi, :, h * HG_DV:(h + 1) * HG_DV] = o_
    new_st = each(lambda s_, bl, v_, k_, b_: s_ * jnp.exp(bl) + _dot_tn(v_, (k_ * jnp.exp(bl - b_)).astype(BF16)),
                  st, b_last, vh, kh, b)
    for (bi, h), s_ in zip(chains, new_st):
        s_ref[bi, h] = s_


def hgrn_scan(q, hf, v, lb, n_ctx_chunks):
    b, t, _ = q.shape
    n_chunks = t // CHUNK
    stack, pair = _hgrn_masks()
    idx = functools.partial(_chunk_index, n_ctx_chunks=n_ctx_chunks, n_chunks=n_chunks)
    in_map = lambda bi, d, c: (bi, idx(d, c), 0)
    return pl.pallas_call(
        _hgrn_kernel,
        grid=(b // SCAN_BATCH, 2, n_chunks),
        in_specs=[pl.BlockSpec((SCAN_BATCH, CHUNK, HG_K), in_map),
                  pl.BlockSpec((SCAN_BATCH, CHUNK, HG_K), lambda bi, d, c: (bi, idx(d, c), d)),
                  pl.BlockSpec((SCAN_BATCH, CHUNK, HG_W), in_map),
                  pl.BlockSpec((1, SCAN_BATCH * HG_K), lambda bi, d, c: (0, 0)),
                  pl.BlockSpec((1,) + stack.shape[1:], lambda bi, d, c: (d, 0, 0)),
                  pl.BlockSpec((1,) + pair.shape[1:], lambda bi, d, c: (d, 0, 0, 0))],
        out_specs=pl.BlockSpec((1, SCAN_BATCH, CHUNK, HG_W), lambda bi, d, c: (d, bi, idx(d, c), 0)),
        out_shape=jax.ShapeDtypeStruct((2, b, t, HG_W), F32),
        scratch_shapes=[pltpu.VMEM((SCAN_BATCH, HG_HEADS, HG_DV, HG_DK), F32)],
        compiler_params=pltpu.CompilerParams(
            dimension_semantics=("parallel", "parallel", "arbitrary"), vmem_limit_bytes=VMEM_LIMIT),
        name="hgrn_scan",
    )(q, hf, v, jnp.tile(lb, (1, SCAN_BATCH)), jnp.asarray(stack, BF16), jnp.asarray(pair, F32))


def _rwkv_masks():
    L, G = CHUNK, RW_GROUP
    n = G * L
    head = np.arange(n) // L
    bd = (head[:, None] == head[None, :]).astype(np.float32)
    cum = np.zeros((2, L, L), np.float32)
    n_lvl = int(np.log2(L // RW_BASE))
    wide = np.zeros((2, 4 + n_lvl, L, n), np.float32)
    for d in range(2):
        p = np.arange(L) if d == 0 else L - 1 - np.arange(L)
        pt, ps = p[:, None], p[None, :]
        cum[d] = ps <= pt
        m = [ps < pt, ps <= pt, np.eye(L, dtype=bool), (pt // RW_BASE) == (ps // RW_BASE)]
        for i in range(n_lvl):
            blk = RW_BASE << i
            parent, half = p // (2 * blk), (p // blk) % 2
            m.append((parent[:, None] == parent[None, :]) & (half[:, None] == 1) & (half[None, :] == 0))
        for i, mi in enumerate(m):
            wide[d, i] = np.tile(mi, (1, G))
    return bd, cum, wide


def _unit_lower_inverse(n_w, wide_ref, expand):
    mul = lambda a_w, b_w: [_dot(x.astype(BF16), expand(y)) for x, y in zip(a_w, b_w)]
    nd = [n * wide_ref[0, 3] for n in n_w]
    t_inv = [wide_ref[0, 2] - x for x in nd]
    pw = nd
    for _ in range(int(np.log2(RW_BASE)) - 1):
        pw = mul(pw, pw)
        t_inv = [t + x for t, x in zip(t_inv, mul(t_inv, pw))]
    for i in range(wide_ref.shape[1] - 4):
        tc = mul(t_inv, [n * wide_ref[0, 4 + i] for n in n_w])
        t_inv = [t - x for t, x in zip(t_inv, mul(tc, t_inv))]
    return t_inv


def _rwkv_kernel(r_ref, k_ref, v_ref, kk_ref, a_ref, lw_ref, bd_ref, cum_ref, wide_ref, o_ref, s_ref):
    c = pl.program_id(2)

    @pl.when(c == 0)
    def _():
        s_ref[...] = jnp.zeros_like(s_ref)

    G = RW_GROUP
    bd = bd_ref[...]
    strict = wide_ref[0, 0]
    incl = wide_ref[0, 1]
    cum = cum_ref[0]

    def expand(x):
        return (jnp.concatenate([x] * G, axis=0) * bd).astype(BF16)

    L = CHUNK
    chains = [(bi, g) for bi in range(SCAN_BATCH) for g in range(RW_HEADS // G)]
    cols = [slice(g * RW_GW, (g + 1) * RW_GW) for _, g in chains]
    each = lambda fn, *lists: [fn(*xs) for xs in zip(*lists)]
    r = [r_ref[bi, :, cs_] for (bi, _), cs_ in zip(chains, cols)]
    k = [k_ref[0, bi, :, cs_] for (bi, _), cs_ in zip(chains, cols)]
    v = [v_ref[bi, :, cs_] for (bi, _), cs_ in zip(chains, cols)]
    kk = [kk_ref[bi, :, cs_] for (bi, _), cs_ in zip(chains, cols)]
    lw = [lw_ref[0, bi, :, cs_] for (bi, _), cs_ in zip(chains, cols)]
    kb = [x * a_ref[0, bi, :, cs_] for x, (bi, _), cs_ in zip(kk, chains, cols)]
    cs = each(lambda x: _mask_dot(cum, x), lw)
    c_last = each(lambda x: jnp.sum(x, axis=0, keepdims=True), lw)
    p_inv = each(lambda x: jnp.exp(-x), cs)
    p_to_end = each(lambda cl, x: jnp.exp(cl - x), c_last, cs)
    kkd_rp = each(lambda kk_, r_, cs_, lw_: jnp.concatenate(
        [kk_ * jnp.exp(cs_ - lw_), r_ * jnp.exp(cs_)], axis=0).astype(BF16), kk, r, cs, lw)
    v_bd = each(expand, v)
    sc_k = each(lambda q_, k_, p_: _dot_nt(q_, expand(k_ * p_)), kkd_rp, k, p_inv)
    sc_b = each(lambda q_, b_, p_: _dot_nt(q_, expand(b_ * p_)), kkd_rp, kb, p_inv)
    n_w = each(lambda x: strict * x[:L], sc_b)
    m_w = each(lambda x: jnp.concatenate([strict * x[:L], incl * x[L:]], axis=0).astype(BF16), sc_k)
    mrb_w = each(lambda x: (incl * x[L:]).astype(BF16), sc_b)
    st = [s_ref[bi, g] for bi, g in chains]
    from_state = each(lambda q_, s_, m_, vb_: _dot_nt(q_, s_.astype(BF16)) + _dot(m_, vb_),
                      kkd_rp, st, m_w, v_bd)
    t_inv = _unit_lower_inverse(n_w, wide_ref, expand)
    u = each(lambda t_, f_: _dot(t_.astype(BF16), expand(f_[:L])), t_inv, from_state)
    y = each(lambda f_, m_, u_: f_[L:] - _dot(m_, expand(u_)), from_state, mrb_w, u)
    for (bi, g), cs_, y_ in zip(chains, cols, y):
        o_ref[0, bi, :, cs_] = y_
    new_st = each(lambda s_, cl, v_, u_, k_, b_, p_: s_ * jnp.exp(cl) + bd * _dot_tn(
        jnp.concatenate([v_, -u_], axis=0).astype(BF16),
        jnp.concatenate([k_ * p_, b_ * p_], axis=0).astype(BF16)), st, c_last, v, u, k, kb, p_to_end)
    for (bi, g), s_ in zip(chains, new_st):
        s_ref[bi, g] = s_


def rwkv_scan(r, k_eff, v, kk, a, lw, n_ctx_chunks):
    b, t, _ = r.shape
    n_chunks = t // CHUNK
    assert b % SCAN_BATCH == 0
    bd, cum, wide = _rwkv_masks()
    idx = functools.partial(_chunk_index, n_ctx_chunks=n_ctx_chunks, n_chunks=n_chunks)
    shared = pl.BlockSpec((SCAN_BATCH, CHUNK, RW_W), lambda bi, d, c: (bi, idx(d, c), 0))
    per_dir = pl.BlockSpec((1, SCAN_BATCH, CHUNK, RW_W), lambda bi, d, c: (d, bi, idx(d, c), 0))
    n = RW_GROUP * CHUNK
    return pl.pallas_call(
        _rwkv_kernel,
        grid=(b // SCAN_BATCH, 2, n_chunks),
        in_specs=[shared, per_dir, shared, shared, per_dir, per_dir,
                  pl.BlockSpec((n, RW_GW), lambda bi, d, c: (0, 0)),
                  pl.BlockSpec((1, CHUNK, CHUNK), lambda bi, d, c: (d, 0, 0)),
                  pl.BlockSpec((1,) + wide.shape[1:], lambda bi, d, c: (d, 0, 0, 0))],
        out_specs=per_dir,
        out_shape=jax.ShapeDtypeStruct((2, b, t, RW_W), F32),
        scratch_shapes=[pltpu.VMEM((SCAN_BATCH, RW_HEADS // RW_GROUP, RW_GW, RW_GW), F32)],
        compiler_params=pltpu.CompilerParams(
            dimension_semantics=("parallel", "parallel", "arbitrary"), vmem_limit_bytes=VMEM_LIMIT),
        name="rwkv7_scan",
    )(r, k_eff, v, kk, a, lw, jnp.asarray(bd, F32), jnp.asarray(cum, BF16), jnp.asarray(wide, F32))


def _moe_kernel(e_ref, x_ref, wgu_ref, bgu_ref, wdn_ref, bdn_ref, o_ref, wgu16_ref, wdn16_ref):
    i = pl.program_id(0)
    new_expert = jnp.logical_or(i == 0, e_ref[i] != e_ref[jnp.maximum(i - 1, 0)])

    @pl.when(new_expert)
    def _():
        wgu16_ref[...] = wgu_ref[0, 0].astype(BF16)
        wdn16_ref[...] = wdn_ref[0, 0].astype(BF16)

    d_ff = wdn16_ref.shape[0]
    gu = _dot(x_ref[...], wgu16_ref[...]) + bgu_ref[0, 0]
    glu = jnp.minimum(gu[:, :d_ff], SWIGLU_LIMIT)
    lin = jnp.clip(gu[:, d_ff:], -SWIGLU_LIMIT, SWIGLU_LIMIT)
    act = glu * jax.nn.sigmoid(SWIGLU_ALPHA * glu) * (lin + 1.0)
    o_ref[...] = _dot(act.astype(BF16), wdn16_ref[...]) + bdn_ref[0, 0]


def moe_experts(xb, blk_e, layer, w_gu, b_gu, w_dn, b_dn):
    n_rows, dm = xb.shape
    d_ff = w_dn.shape[2]
    n_blocks = n_rows // MOE_ROWS
    return pl.pallas_call(
        _moe_kernel,
        grid_spec=pltpu.PrefetchScalarGridSpec(
            num_scalar_prefetch=1,
            grid=(n_blocks,),
            in_specs=[pl.BlockSpec((MOE_ROWS, dm), lambda i, e: (i, 0)),
                      pl.BlockSpec((1, 1, dm, 2 * d_ff), lambda i, e: (layer, e[i], 0, 0)),
                      pl.BlockSpec((1, 1, 1, 2 * d_ff), lambda i, e: (layer, e[i], 0, 0)),
                      pl.BlockSpec((1, 1, d_ff, dm), lambda i, e: (layer, e[i], 0, 0)),
                      pl.BlockSpec((1, 1, 1, dm), lambda i, e: (layer, e[i], 0, 0))],
            out_specs=pl.BlockSpec((MOE_ROWS, dm), lambda i, e: (i, 0)),
            scratch_shapes=[pltpu.VMEM((dm, 2 * d_ff), BF16), pltpu.VMEM((d_ff, dm), BF16)]),
        out_shape=jax.ShapeDtypeStruct((n_rows, dm), F32),
        compiler_params=pltpu.CompilerParams(dimension_semantics=("arbitrary",),
                                             vmem_limit_bytes=VMEM_LIMIT),
        name="moe_experts",
    )(blk_e, xb, w_gu, b_gu, w_dn, b_dn)


def moe_ffn(t, router_w, router_b, layer, w_gu, b_gu, w_dn, b_dn):
    n, dm = t.shape
    logits = jnp.dot(t, router_w, precision=lax.Precision.HIGHEST) + router_b
    top_logit, top_e = lax.top_k(logits, TOP_K)
    top_w = jax.nn.softmax(top_logit, axis=-1)
    flat_e = top_e.reshape(-1).astype(jnp.int32)
    n_asg = n * TOP_K
    seg = 128
    assert n_asg % seg == 0
    onehot = (flat_e.reshape(n_asg // seg, seg, 1) == jnp.arange(N_EXPERTS, dtype=jnp.int32)).astype(F32)
    within = jnp.einsum('ts,bse->bte', jnp.tril(jnp.ones((seg, seg), F32)), onehot)
    seg_tot = within[:, -1, :]
    before = jnp.cumsum(seg_tot, axis=0) - seg_tot
    rank = (jnp.sum(onehot * (within + before[:, None, :]), axis=-1) - 1.0).astype(jnp.int32).reshape(-1)
    counts = jnp.sum(seg_tot, axis=0).astype(jnp.int32)
    padded = (counts + MOE_ROWS - 1) // MOE_ROWS * MOE_ROWS
    pad_end = jnp.cumsum(padded)
    pad_start = pad_end - padded
    grp_start = jnp.cumsum(counts) - counts
    n_blocks = -(-n_asg // MOE_ROWS) + N_EXPERTS
    n_rows = n_blocks * MOE_ROWS
    blk_e = jnp.minimum(jnp.searchsorted(pad_end, jnp.arange(n_blocks, dtype=jnp.int32) * MOE_ROWS, side='right'),
                        N_EXPERTS - 1).astype(jnp.int32)
    order = jnp.argsort(flat_e)
    e_row = jnp.repeat(blk_e, MOE_ROWS)
    j = jnp.arange(n_rows, dtype=jnp.int32) - pad_start[e_row]
    used = j < counts[e_row]
    row_tok = jnp.where(used, order[jnp.where(used, grp_start[e_row] + j, 0)] // TOP_K, 0)
    xb = t.astype(BF16)[row_tok]
    yb = moe_experts(xb, blk_e, layer, w_gu, b_gu[:, :, None, :], w_dn, b_dn[:, :, None, :])
    slot = (pad_start[flat_e] + rank).reshape(n, TOP_K)
    y = jnp.zeros((n, dm), F32)
    for j in range(TOP_K):
        y = y + yb[slot[:, j]] * top_w[:, j:j + 1]
    return y


def rms_norm(x, g):
    return x * lax.rsqrt(jnp.mean(x * x, axis=-1, keepdims=True) + EPS) * g


def head_rms_norm(o, g):
    return o * lax.rsqrt(jnp.mean(o * o, axis=-1, keepdims=True) + EPS) * g


def dwconv3(x, w, n_ctx):
    t = x.shape[1]
    xp = jnp.pad(x, ((0, 0), (1, 1), (0, 0)))
    pos = jnp.arange(t)[None, :, None]
    has_prev = (pos != 0) & (pos != n_ctx)
    has_next = (pos != n_ctx - 1) & (pos != t - 1)
    return (w[0] * jnp.where(has_prev, xp[:, :-2], 0.0) + w[1] * x
            + w[2] * jnp.where(has_next, xp[:, 2:], 0.0))


def rope_tables(n_ctx, seq):
    rows = seq // GRID_W
    pos_r = jnp.repeat(jnp.arange(rows, dtype=F32), GRID_W)
    pos_c = jnp.tile(jnp.arange(GRID_W, dtype=F32), rows)
    nf = RET_DK // 4
    inv = ROPE_BASE ** (-jnp.arange(nf, dtype=F32) / nf)
    ang = jnp.concatenate([pos_r[:, None] * inv, pos_c[:, None] * inv], axis=-1)
    cos = jnp.concatenate([jnp.ones((n_ctx, RET_DK // 2), F32), jnp.cos(ang)], axis=0)
    sin = jnp.concatenate([jnp.zeros((n_ctx, RET_DK // 2), F32), jnp.sin(ang)], axis=0)
    return cos, sin


def apply_rope(x, cos, sin):
    half = x.shape[-1] // 2
    x1, x2 = x[..., :half], x[..., half:]
    c, s = cos[None, :, None, :], sin[None, :, None, :]
    return jnp.concatenate([x1 * c - x2 * s, x1 * s + x2 * c], axis=-1)


def project(u, w):
    b, t, dm = u.shape
    m = w.shape[1]
    m_pad = -(-m // 128) * 128
    w16 = jnp.pad(w, ((0, 0), (0, m_pad - m))).astype(BF16)
    z = matmul(u.reshape(b * t, dm).astype(BF16), w16)
    return z[:, :m].reshape(b, t, m)


def mixer_ab(u, n_ctx, rope, w_in, w_out, ret_decay, ret_norm, ml_conv, ml_gate_b, ml_norm):
    b, t, _ = u.shape
    n_ctx_chunks = n_ctx // CHUNK
    rq, rk, rv, rg, mqk, mv, mo, mg = _split(project(u, w_in), AB_SPLITS)
    rq = apply_rope(rq.reshape(b, t, RET_HEADS, RET_DK), *rope).reshape(b, t, RET_QK)
    rk = apply_rope(rk.reshape(b, t, RET_HEADS, RET_DK) * RET_DK ** -0.5, *rope).reshape(b, t, RET_QK)
    log_gamma = jnp.log1p(-jnp.exp(ret_decay))
    ret = retention_scan(rq, rk, rv, log_gamma, n_ctx_chunks)
    ret = ret[0] + ret[1]

    mq, mk = _split(jax.nn.silu(dwconv3(mqk, ml_conv, n_ctx)), (ML_QK, ML_QK))
    mq = mq * ML_DK ** -0.5
    gates = (mg + ml_gate_b).reshape(b, t, 2, 2, ML_HEADS)
    gates = jnp.stack([gates[:, :, :, 0], jax.nn.log_sigmoid(gates[:, :, :, 1])], axis=3)
    g_col = jnp.moveaxis(gates, 2, 0).reshape(2, b, t, 2 * ML_HEADS)
    g_row = jnp.swapaxes(g_col.reshape(2, b, t // CHUNK, CHUNK, 2 * ML_HEADS), 3, 4)
    ml = mlstm_scan(mq, mk, mv, g_col, g_row, n_ctx_chunks)
    ml = ml[0] + ml[1]

    ret = head_rms_norm(ret.reshape(b, t, RET_HEADS, RET_DV), ret_norm).reshape(b, t, RET_W) * jax.nn.silu(rg)
    ml = head_rms_norm(ml.reshape(b, t, ML_HEADS, ML_DV), ml_norm).reshape(b, t, ML_W) * jax.nn.sigmoid(mo)
    return project(jnp.concatenate([ret, ml], axis=-1), w_out)


def mixer_cd(u, n_ctx, w_in, w_out, lb, hg_norm, rw_shift, rw_w0, rw_w2, rw_a0, rw_a2, rw_g2,
             rw_kk_scale, rw_k_a, rw_r_k, rw_norm):
    b, t, _ = u.shape
    n_ctx_chunks = n_ctx // CHUNK
    hq, hf, hi, hg, zr = _split(project(u, w_in), CD_SPLITS)
    hgo = hgrn_scan(hq, hf, hi, lb.reshape(1, HG_K), n_ctx_chunks)
    hgo = hgo[0] + hgo[1]

    r, k, v, wl, al, gl = _split(dwconv3(zr, rw_shift, n_ctx), RWKV_SPLITS)
    kk = (k * rw_kk_scale).reshape(b, t, RW_HEADS, RW_HD)
    kk = (kk * lax.rsqrt(jnp.sum(kk * kk, axis=-1, keepdims=True) + EPS)).reshape(b, t, RW_W)
    gate = jax.nn.sigmoid(gl) @ rw_g2
    k_eff, a_all, lw_all = [], [], []
    for d in range(2):
        wl_d = wl[..., d * RW_W_RANK:(d + 1) * RW_W_RANK]
        al_d = al[..., d * RW_A_RANK:(d + 1) * RW_A_RANK]
        w_log = -jax.nn.softplus(-(rw_w0[d] + jnp.tanh(wl_d) @ rw_w2[d])) - 0.5
        lw_all.append(-jnp.exp(w_log))
        a = jax.nn.sigmoid(rw_a0[d] + al_d @ rw_a2[d])
        a_all.append(a)
        k_eff.append(k * (1.0 + (a - 1.0) * rw_k_a))
    k_eff, a_all, lw_all = jnp.stack(k_eff), jnp.stack(a_all), jnp.stack(lw_all)
    rwo = rwkv_scan(r, k_eff, v, kk, a_all, lw_all, n_ctx_chunks)
    rwo = (rwo[0] + rwo[1]).reshape(b, t, RW_HEADS, RW_HD)

    r4, v4 = r.reshape(b, t, RW_HEADS, RW_HD), v.reshape(b, t, RW_HEADS, RW_HD)
    bonus = sum(jnp.sum(r4 * k_eff[d].reshape(b, t, RW_HEADS, RW_HD) * rw_r_k, axis=-1, keepdims=True) * v4
                for d in range(2))
    hgo = head_rms_norm(hgo.reshape(b, t, HG_HEADS, HG_DV), hg_norm).reshape(b, t, HG_W) * jax.nn.sigmoid(hg)
    rwo = (head_rms_norm(rwo, rw_norm) + bonus).reshape(b, t, RW_W) * gate
    return project(jnp.concatenate([hgo, rwo], axis=-1), w_out)


def hgrn_lower_bound(p, layer):
    sm = jax.nn.softmax(p, axis=0)
    return jnp.cumsum(sm, axis=0)[layer] - sm[0]


def kernel(x, c, ctx, c_ctx, mod_w, mod_b, norm_mix, norm_ffn, norm_final, ab_w_in, ab_w_out, ret_decay, ret_norm, mlstm_conv, mlstm_gate_b, mlstm_norm, cd_w_in, cd_w_out, hgrn_lb, hgrn_norm, rwkv_shift, rwkv_w0, rwkv_w2, rwkv_a0, rwkv_a2, rwkv_g2, rwkv_kk_scale, rwkv_k_a, rwkv_r_k, rwkv_norm, router_w, router_b, exp_w_gate_up, exp_b_gate_up, exp_w_down, exp_b_down):
    bsz, seq, dm = x.shape
    n_ctx = ctx.shape[1]
    depth = mod_w.shape[0]
    t = n_ctx + seq
    rope = rope_tables(n_ctx, seq)
    cond_l = jax.nn.silu(c)
    cond_c = jax.nn.silu(c_ctx)[None, :]
    h = jnp.concatenate([ctx, x], axis=1)
    hp = lax.Precision.HIGHEST

    is_ctx = (jnp.arange(t) < n_ctx)[None, :, None]

    def per_token(m_c, m_l):
        return jnp.where(is_ctx, m_c[:, None, :], m_l[:, None, :])

    for i in range(depth):
        j = i // 2
        m_l = jnp.split(jnp.dot(cond_l, mod_w[i], precision=hp) + mod_b[i], 6, axis=-1)
        m_c = jnp.split(jnp.dot(cond_c, mod_w[i], precision=hp) + mod_b[i], 6, axis=-1)
        m = [per_token(a, b_) for a, b_ in zip(m_c, m_l)]
        u = rms_norm(h, norm_mix[i]) * (1.0 + m[1]) + m[0]
        if i % 2 == 0:
            y = mixer_ab(u, n_ctx, rope, ab_w_in[j], ab_w_out[j], ret_decay[j], ret_norm[j],
                         mlstm_conv[j], mlstm_gate_b[j], mlstm_norm[j])
        else:
            y = mixer_cd(u, n_ctx, cd_w_in[j], cd_w_out[j], hgrn_lower_bound(hgrn_lb, i), hgrn_norm[j],
                         rwkv_shift[j], rwkv_w0[j], rwkv_w2[j], rwkv_a0[j], rwkv_a2[j], rwkv_g2[j],
                         rwkv_kk_scale[j], rwkv_k_a[j], rwkv_r_k[j], rwkv_norm[j])
        h = h + m[2] * y
        vv = rms_norm(h, norm_ffn[i]) * (1.0 + m[4]) + m[3]
        f = moe_ffn(vv.reshape(-1, dm), router_w[i], router_b[i], i, exp_w_gate_up, exp_b_gate_up,
                    exp_w_down, exp_b_down).reshape(vv.shape)
        h = h + m[5] * f
    return rms_norm(h[:, n_ctx:], norm_final)
```

```python
import functools

import numpy as np
import jax
import jax.numpy as jnp
from jax import lax
from jax.experimental import pallas as pl
from jax.experimental.pallas import tpu as pltpu

F32 = jnp.float32
BF16 = jnp.bfloat16

CHUNK = 64
GRID_W = 64
EPS = 1e-6
ROPE_BASE = 10000.0
RET_HEADS, RET_DK, RET_DV = 4, 64, 128
ML_HEADS, ML_DK, ML_DV = 4, 64, 128
HG_HEADS, HG_DK, HG_DV = 4, 128, 128
RW_HEADS, RW_HD = 8, 64
RW_W_RANK, RW_A_RANK, RW_G_RANK = 64, 64, 128
N_EXPERTS, TOP_K = 32, 4
SWIGLU_LIMIT, SWIGLU_ALPHA = 7.0, 1.702

RET_QK, RET_W = RET_HEADS * RET_DK, RET_HEADS * RET_DV
ML_QK, ML_W = ML_HEADS * ML_DK, ML_HEADS * ML_DV
HG_K, HG_W = HG_HEADS * HG_DK, HG_HEADS * HG_DV
RW_W = RW_HEADS * RW_HD
AB_SPLITS = (RET_QK, RET_QK, RET_W, RET_W, 2 * ML_QK, ML_W, ML_W, 4 * ML_HEADS)
RWKV_SPLITS = (RW_W, RW_W, RW_W, 2 * RW_W_RANK, 2 * RW_A_RANK, RW_G_RANK)
RWKV_IN = sum(RWKV_SPLITS)
CD_SPLITS = (HG_K, 2 * HG_K, HG_W, HG_W, RWKV_IN)

RW_GROUP = 4
RW_GW = RW_GROUP * RW_HD
RW_BASE = 8
HG_LEVELS = 6

SCAN_BATCH = 4
MM_ROWS = 256
MOE_ROWS = 512
VMEM_LIMIT = 56 * 1024 * 1024


def _split(z, sizes):
    return jnp.split(z, [int(s) for s in np.cumsum(sizes)[:-1]], axis=-1)


def _dot(a, b):
    return jnp.dot(a, b, preferred_element_type=F32)


def _dot_nt(a, b):
    return lax.dot_general(a, b, (((1,), (1,)), ((), ())), preferred_element_type=F32)


def _dot_tn(a, b):
    return lax.dot_general(a, b, (((0,), (0,)), ((), ())), preferred_element_type=F32)


def _mask_dot(m16, x):
    hi = x.astype(BF16)
    lo = (x - hi.astype(F32)).astype(BF16)
    return _dot(m16, hi) + _dot(m16, lo)


def _chunk_index(d, c, n_ctx_chunks, n_chunks):
    rev = jnp.where(c < n_ctx_chunks, n_ctx_chunks - 1 - c, n_chunks + n_ctx_chunks - 1 - c)
    return jnp.where(d == 0, c, rev)


def _mm_kernel(x_ref, w_ref, o_ref):
    o_ref[...] = _dot(x_ref[...], w_ref[...])


def matmul(x, w):
    n, k = x.shape
    m = w.shape[1]
    assert n % MM_ROWS == 0 and m % 128 == 0
    return pl.pallas_call(
        _mm_kernel,
        grid=(n // MM_ROWS,),
        in_specs=[pl.BlockSpec((MM_ROWS, k), lambda i: (i, 0)),
                  pl.BlockSpec((k, m), lambda i: (0, 0))],
        out_specs=pl.BlockSpec((MM_ROWS, m), lambda i: (i, 0)),
        out_shape=jax.ShapeDtypeStruct((n, m), F32),
        compiler_params=pltpu.CompilerParams(dimension_semantics=("parallel",),
                                             vmem_limit_bytes=VMEM_LIMIT),
        name="dense_proj",
    )(x, w)


def _ret_kernel(lg_ref, q_ref, k_ref, v_ref, o_ref, s_ref):
    d = pl.program_id(1)
    c = pl.program_id(2)

    @pl.when(c == 0)
    def _():
        s_ref[...] = jnp.zeros_like(s_ref)

    L = CHUNK
    ti = lax.broadcasted_iota(jnp.int32, (L, L), 0)
    si = lax.broadcasted_iota(jnp.int32, (L, L), 1)
    dist = jnp.where(d == 0, ti - si, si - ti)
    causal = dist >= 0
    distf = jnp.maximum(dist, 0).astype(F32)
    row = lax.broadcasted_iota(jnp.int32, (L, 1), 0)
    pos = jnp.where(d == 0, row, L - 1 - row).astype(F32)
    lg = [lg_ref[d, h] for h in range(RET_HEADS)]
    dm = [jnp.where(causal, jnp.exp(x * distf), 0.0) for x in lg]
    q_dec = [jnp.exp(x * (pos + 1.0)) for x in lg]
    k_dec = [jnp.exp(x * (L - 1.0 - pos)) for x in lg]
    chains = [(bi, h) for bi in range(SCAN_BATCH) for h in range(RET_HEADS)]
    qh = [q_ref[bi, :, h * RET_DK:(h + 1) * RET_DK] for bi, h in chains]
    kh = [k_ref[bi, :, h * RET_DK:(h + 1) * RET_DK] for bi, h in chains]
    vh = [v_ref[bi, :, h * RET_DV:(h + 1) * RET_DV].astype(BF16) for bi, h in chains]
    att = [_dot_nt(q_.astype(BF16), k_.astype(BF16)) * dm[h] for q_, k_, (_, h) in zip(qh, kh, chains)]
    s = [s_ref[bi, h] for bi, h in chains]
    o = [_dot((q_ * q_dec[h]).astype(BF16), s_.astype(BF16)) + _dot(a_.astype(BF16), v_)
         for q_, s_, a_, v_, (_, h) in zip(qh, s, att, vh, chains)]
    for (bi, h), o_ in zip(chains, o):
        o_ref[0, bi, :, h * RET_DV:(h + 1) * RET_DV] = o_
    new_s = [jnp.exp(lg[h] * L) * s_ + _dot_tn((k_ * k_dec[h]).astype(BF16), v_)
             for s_, k_, v_, (_, h) in zip(s, kh, vh, chains)]
    for (bi, h), s_ in zip(chains, new_s):
        s_ref[bi, h] = s_


def retention_scan(q, k, v, log_gamma, n_ctx_chunks):
    b, t, _ = q.shape
    n_chunks = t // CHUNK
    idx = functools.partial(_chunk_index, n_ctx_chunks=n_ctx_chunks, n_chunks=n_chunks)
    in_map = lambda bi, d, c, lg: (bi, idx(d, c), 0)
    return pl.pallas_call(
        _ret_kernel,
        grid_spec=pltpu.PrefetchScalarGridSpec(
            num_scalar_prefetch=1,
            grid=(b // SCAN_BATCH, 2, n_chunks),
            in_specs=[pl.BlockSpec((SCAN_BATCH, CHUNK, RET_QK), in_map),
                      pl.BlockSpec((SCAN_BATCH, CHUNK, RET_QK), in_map),
                      pl.BlockSpec((SCAN_BATCH, CHUNK, RET_W), in_map)],
            out_specs=pl.BlockSpec((1, SCAN_BATCH, CHUNK, RET_W), lambda bi, d, c, lg: (d, bi, idx(d, c), 0)),
            scratch_shapes=[pltpu.VMEM((SCAN_BATCH, RET_HEADS, RET_DK, RET_DV), F32)]),
        out_shape=jax.ShapeDtypeStruct((2, b, t, RET_W), F32),
        compiler_params=pltpu.CompilerParams(
            dimension_semantics=("parallel", "parallel", "arbitrary"), vmem_limit_bytes=VMEM_LIMIT),
        name="retention_scan",
    )(log_gamma, q, k, v)


def _mlstm_kernel(q_ref, k_ref, v_ref, gc_ref, gr_ref, o_ref, c_ref, m_ref):
    d = pl.program_id(1)
    c = pl.program_id(2)

    @pl.when(c == 0)
    def _():
        c_ref[...] = jnp.zeros_like(c_ref)
        m_ref[...] = jnp.zeros_like(m_ref)

    L = CHUNK
    ti = lax.broadcasted_iota(jnp.int32, (L, L), 0)
    si = lax.broadcasted_iota(jnp.int32, (L, L), 1)
    dist = jnp.where(d == 0, ti - si, si - ti)
    le = dist >= 0
    le_t = dist <= 0
    lane = lax.broadcasted_iota(jnp.int32, (L, ML_DV), 1)
    ones_col = jnp.where(lane == 0, 1.0, 0.0).astype(BF16)
    H = ML_HEADS
    chains = [(bi, h) for bi in range(SCAN_BATCH) for h in range(H)]
    each = lambda fn, *lists: [fn(*xs) for xs in zip(*lists)]
    gc = [gc_ref[0, bi] for bi in range(SCAN_BATCH)]
    gr = [gr_ref[0, bi, 0] for bi in range(SCAN_BATCH)]
    i_col = [gc[bi][:, h:h + 1] for bi, h in chains]
    f_col = [gc[bi][:, H + h:H + h + 1] for bi, h in chains]
    i_row = [gr[bi][h:h + 1, :] for bi, h in chains]
    f_row = [gr[bi][H + h:H + h + 1, :] for bi, h in chains]
    b_col = each(lambda f_: jnp.sum(jnp.where(le, f_, 0.0), axis=1, keepdims=True), f_row)
    b_row = each(lambda f_: jnp.sum(jnp.where(le_t, f_, 0.0), axis=0, keepdims=True), f_col)
    b_last = each(lambda f_: jnp.sum(f_, axis=1, keepdims=True), f_row)
    m = [m_ref[bi * H + h:bi * H + h + 1, 0:1] for bi, h in chains]
    log_d = each(lambda bc, br, ir: jnp.where(le, bc - br + ir, -jnp.inf), b_col, b_row, i_row)
    inter = each(lambda bc, m_: bc + m_, b_col, m)
    m_row = each(lambda in_, ld: jnp.maximum(in_, jnp.max(ld, axis=1, keepdims=True)), inter, log_d)
    qh = [q_ref[bi, :, h * ML_DK:(h + 1) * ML_DK].astype(BF16) for bi, h in chains]
    kh = [k_ref[bi, :, h * ML_DK:(h + 1) * ML_DK] for bi, h in chains]
    v_aug = [jnp.concatenate([v_ref[bi, :, h * ML_DV:(h + 1) * ML_DV].astype(BF16), ones_col], axis=1)
             for bi, h in chains]
    s = each(lambda q_, k_, ld, mr: _dot_nt(q_, k_.astype(BF16)) * jnp.exp(ld - mr), qh, kh, log_d, m_row)
    c_aug = [c_ref[bi, h] for bi, h in chains]
    num = each(lambda in_, mr, q_, c_, s_, v_: jnp.exp(in_ - mr) * _dot(q_, c_.astype(BF16))
               + _dot(s_.astype(BF16), v_), inter, m_row, qh, c_aug, s, v_aug)
    hh = each(lambda n_, mr: n_[:, :ML_DV] / jnp.maximum(jnp.abs(n_[:, ML_DV:ML_DV + 1]), jnp.exp(-mr)), num, m_row)
    for (bi, h), h_ in zip(chains, hh):
        o_ref[0, bi, :, h * ML_DV:(h + 1) * ML_DV] = h_
    m_new = each(lambda bl, m_, br, ir: jnp.maximum(bl + m_, jnp.max(bl - br + ir, axis=1, keepdims=True)),
                 b_last, m, b_row, i_row)
    new_c = each(lambda bl, m_, mn, c_, k_, bc, ic, v_: jnp.exp(bl + m_ - mn) * c_ + _dot_tn(
        (k_ * jnp.exp(bl - bc + ic - mn)).astype(BF16), v_), b_last, m, m_new, c_aug, kh, b_col, i_col, v_aug)
    for (bi, h), c_, mn in zip(chains, new_c, m_new):
        c_ref[bi, h] = c_
        m_ref[bi * H + h:bi * H + h + 1, :] = jnp.broadcast_to(mn, (1, 128))


def mlstm_scan(q, k, v, g_col, g_row, n_ctx_chunks):
    b, t, _ = q.shape
    n_chunks = t // CHUNK
    idx = functools.partial(_chunk_index, n_ctx_chunks=n_ctx_chunks, n_chunks=n_chunks)
    in_map = lambda bi, d, c: (bi, idx(d, c), 0)
    return pl.pallas_call(
        _mlstm_kernel,
        grid=(b // SCAN_BATCH, 2, n_chunks),
        in_specs=[pl.BlockSpec((SCAN_BATCH, CHUNK, ML_QK), in_map),
                  pl.BlockSpec((SCAN_BATCH, CHUNK, ML_QK), in_map),
                  pl.BlockSpec((SCAN_BATCH, CHUNK, ML_W), in_map),
                  pl.BlockSpec((1, SCAN_BATCH, CHUNK, 2 * ML_HEADS), lambda bi, d, c: (d, bi, idx(d, c), 0)),
                  pl.BlockSpec((1, SCAN_BATCH, 1, 2 * ML_HEADS, CHUNK), lambda bi, d, c: (d, bi, idx(d, c), 0, 0))],
        out_specs=pl.BlockSpec((1, SCAN_BATCH, CHUNK, ML_W), lambda bi, d, c: (d, bi, idx(d, c), 0)),
        out_shape=jax.ShapeDtypeStruct((2, b, t, ML_W), F32),
        scratch_shapes=[pltpu.VMEM((SCAN_BATCH, ML_HEADS, ML_DK, 2 * ML_DV), F32),
                        pltpu.VMEM((SCAN_BATCH * ML_HEADS, 128), F32)],
        compiler_params=pltpu.CompilerParams(
            dimension_semantics=("parallel", "parallel", "arbitrary"), vmem_limit_bytes=VMEM_LIMIT),
        name="mlstm_scan",
    )(q, k, v, g_col, g_row)


def _hgrn_masks():
    L = CHUNK
    stack = np.zeros((2, (1 + 2 * HG_LEVELS) * L, L), np.float32)
    pair = np.zeros((2, HG_LEVELS + 1, L, L), np.float32)
    for d in range(2):
        p = np.arange(L) if d == 0 else L - 1 - np.arange(L)
        pt, ps = p[:, None], p[None, :]
        stack[d, :L] = ps <= pt
        for l in range(HG_LEVELS):
            parent, half = p >> (l + 1), (p >> l) & 1
            split = parent * (2 << l) + (1 << l) - 1
            same = parent[:, None] == parent[None, :]
            e = same & (half[:, None] == 1) & (ps > split[:, None]) & (ps <= pt)
            f = same & (half[:, None] == 0) & (ps > pt) & (ps <= split[:, None])
            stack[d, (1 + 2 * l) * L:(2 + 2 * l) * L] = e
            stack[d, (2 + 2 * l) * L:(3 + 2 * l) * L] = f
            pair[d, l] = same & (half[:, None] == 1) & (half[None, :] == 0)
        pair[d, HG_LEVELS] = np.eye(L)
    return stack, pair


def _hgrn_kernel(q_ref, hf_ref, v_ref, lb_ref, stack_ref, pair_ref, o_ref, s_ref):
    c = pl.program_id(2)

    @pl.when(c == 0)
    def _():
        s_ref[...] = jnp.zeros_like(s_ref)

    L = CHUNK
    lb = lb_ref[...]
    f = lb + (1.0 - lb) * jax.nn.sigmoid(jnp.concatenate([hf_ref[bi] for bi in range(SCAN_BATCH)], axis=1))
    kk = 1.0 - f
    g = jnp.log(f)
    sums = _mask_dot(stack_ref[0], g)
    chains = [(bi, h) for bi in range(SCAN_BATCH) for h in range(HG_HEADS)]
    each = lambda fn, *lists: [fn(*xs) for xs in zip(*lists)]
    hs = [slice(bi * HG_K + h * HG_DK, bi * HG_K + (h + 1) * HG_DK) for bi, h in chains]
    qh = [q_ref[bi, :, h * HG_DK:(h + 1) * HG_DK] for bi, h in chains]
    kh = [kk[:, s_] for s_ in hs]
    vh = [v_ref[bi, :, h * HG_DV:(h + 1) * HG_DV].astype(BF16) for bi, h in chains]
    b = [sums[0:L, s_] for s_ in hs]
    att = each(lambda q_, k_: pair_ref[0, HG_LEVELS] * _dot_nt(q_.astype(BF16), k_.astype(BF16)), qh, kh)
    for l in range(HG_LEVELS):
        e = [sums[(1 + 2 * l) * L:(2 + 2 * l) * L, s_] for s_ in hs]
        fl = [sums[(2 + 2 * l) * L:(3 + 2 * l) * L, s_] for s_ in hs]
        att = each(lambda a_, q_, k_, e_, f_: a_ + pair_ref[0, l] * _dot_nt(
            (q_ * jnp.exp(e_)).astype(BF16), (k_ * jnp.exp(f_)).astype(BF16)), att, qh, kh, e, fl)
    b_last = [jnp.sum(g[:, s_], axis=0, keepdims=True) for s_ in hs]
    st = [s_ref[bi, h] for bi, h in chains]
    o = each(lambda q_, b_, s_, a_, v_: _dot_nt((q_ * jnp.exp(b_)).astype(BF16), s_.astype(BF16))
             + _dot(a_.astype(BF16), v_), qh, b, st, att, vh)
    for (bi, h), o_ in zip(chains, o):
        o_ref[0, bi, :, h * HG_DV:(h + 1) * HG_DV] = o_
    new_st = each(lambda s_, bl, v_, k_, b_: s_ * jnp.exp(bl) + _dot_tn(v_, (k_ * jnp.exp(bl - b_)).astype(BF16)),
                  st, b_last, vh, kh, b)
    for (bi, h), s_ in zip(chains, new_st):
        s_ref[bi, h] = s_


def hgrn_scan(q, hf, v, lb, n_ctx_chunks):
    b, t, _ = q.shape
    n_chunks = t // CHUNK
    stack, pair = _hgrn_masks()
    idx = functools.partial(_chunk_index, n_ctx_chunks=n_ctx_chunks, n_chunks=n_chunks)
    in_map = lambda bi, d, c: (bi, idx(d, c), 0)
    return pl.pallas_call(
        _hgrn_kernel,
        grid=(b // SCAN_BATCH, 2, n_chunks),
        in_specs=[pl.BlockSpec((SCAN_BATCH, CHUNK, HG_K), in_map),
                  pl.BlockSpec((SCAN_BATCH, CHUNK, HG_K), lambda bi, d, c: (bi, idx(d, c), d)),
                  pl.BlockSpec((SCAN_BATCH, CHUNK, HG_W), in_map),
                  pl.BlockSpec((1, SCAN_BATCH * HG_K), lambda bi, d, c: (0, 0)),
                  pl.BlockSpec((1,) + stack.shape[1:], lambda bi, d, c: (d, 0, 0)),
                  pl.BlockSpec((1,) + pair.shape[1:], lambda bi, d, c: (d, 0, 0, 0))],
        out_specs=pl.BlockSpec((1, SCAN_BATCH, CHUNK, HG_W), lambda bi, d, c: (d, bi, idx(d, c), 0)),
        out_shape=jax.ShapeDtypeStruct((2, b, t, HG_W), F32),
        scratch_shapes=[pltpu.VMEM((SCAN_BATCH, HG_HEADS, HG_DV, HG_DK), F32)],
        compiler_params=pltpu.CompilerParams(
            dimension_semantics=("parallel", "parallel", "arbitrary"), vmem_limit_bytes=VMEM_LIMIT),
        name="hgrn_scan",
    )(q, hf, v, jnp.tile(lb, (1, SCAN_BATCH)), jnp.asarray(stack, BF16), jnp.asarray(pair, F32))


def _rwkv_masks():
    L, G = CHUNK, RW_GROUP
    n = G * L
    head = np.arange(n) // L
    bd = (head[:, None] == head[None, :]).astype(np.float32)
    cum = np.zeros((2, L, L), np.float32)
    n_lvl = int(np.log2(L // RW_BASE))
    wide = np.zeros((2, 4 + n_lvl, L, n), np.float32)
    for d in range(2):
        p = np.arange(L) if d == 0 else L - 1 - np.arange(L)
        pt, ps = p[:, None], p[None, :]
        cum[d] = ps <= pt
        m = [ps < pt, ps <= pt, np.eye(L, dtype=bool), (pt // RW_BASE) == (ps // RW_BASE)]
        for i in range(n_lvl):
            blk = RW_BASE << i
            parent, half = p // (2 * blk), (p // blk) % 2
            m.append((parent[:, None] == parent[None, :]) & (half[:, None] == 1) & (half[None, :] == 0))
        for i, mi in enumerate(m):
            wide[d, i] = np.tile(mi, (1, G))
    return bd, cum, wide


def _unit_lower_inverse(n_w, wide_ref, expand):
    mul = lambda a_w, b_w: [_dot(x.astype(BF16), expand(y)) for x, y in zip(a_w, b_w)]
    nd = [n * wide_ref[0, 3] for n in n_w]
    t_inv = [wide_ref[0, 2] - x for x in nd]
    pw = nd
    for _ in range(int(np.log2(RW_BASE)) - 1):
        pw = mul(pw, pw)
        t_inv = [t + x for t, x in zip(t_inv, mul(t_inv, pw))]
    for i in range(wide_ref.shape[1] - 4):
        tc = mul(t_inv, [n * wide_ref[0, 4 + i] for n in n_w])
        t_inv = [t - x for t, x in zip(t_inv, mul(tc, t_inv))]
    return t_inv


def _rwkv_kernel(r_ref, k_ref, v_ref, kk_ref, a_ref, lw_ref, bd_ref, cum_ref, wide_ref, o_ref, s_ref):
    c = pl.program_id(2)

    @pl.when(c == 0)
    def _():
        s_ref[...] = jnp.zeros_like(s_ref)

    G = RW_GROUP
    bd = bd_ref[...]
    strict = wide_ref[0, 0]
    incl = wide_ref[0, 1]
    cum = cum_ref[0]

    def expand(x):
        return (jnp.concatenate([x] * G, axis=0) * bd).astype(BF16)

    L = CHUNK
    chains = [(bi, g) for bi in range(SCAN_BATCH) for g in range(RW_HEADS // G)]
    cols = [slice(g * RW_GW, (g + 1) * RW_GW) for _, g in chains]
    each = lambda fn, *lists: [fn(*xs) for xs in zip(*lists)]
    r = [r_ref[bi, :, cs_] for (bi, _), cs_ in zip(chains, cols)]
    k = [k_ref[0, bi, :, cs_] for (bi, _), cs_ in zip(chains, cols)]
    v = [v_ref[bi, :, cs_] for (bi, _), cs_ in zip(chains, cols)]
    kk = [kk_ref[bi, :, cs_] for (bi, _), cs_ in zip(chains, cols)]
    lw = [lw_ref[0, bi, :, cs_] for (bi, _), cs_ in zip(chains, cols)]
    kb = [x * a_ref[0, bi, :, cs_] for x, (bi, _), cs_ in zip(kk, chains, cols)]
    cs = each(lambda x: _mask_dot(cum, x), lw)
    c_last = each(lambda x: jnp.sum(x, axis=0, keepdims=True), lw)
    p_inv = each(lambda x: jnp.exp(-x), cs)
    p_to_end = each(lambda cl, x: jnp.exp(cl - x), c_last, cs)
    kkd_rp = each(lambda kk_, r_, cs_, lw_: jnp.concatenate(
        [kk_ * jnp.exp(cs_ - lw_), r_ * jnp.exp(cs_)], axis=0).astype(BF16), kk, r, cs, lw)
    v_bd = each(expand, v)
    sc_k = each(lambda q_, k_, p_: _dot_nt(q_, expand(k_ * p_)), kkd_rp, k, p_inv)
    sc_b = each(lambda q_, b_, p_: _dot_nt(q_, expand(b_ * p_)), kkd_rp, kb, p_inv)
    n_w = each(lambda x: strict * x[:L], sc_b)
    m_w = each(lambda x: jnp.concatenate([strict * x[:L], incl * x[L:]], axis=0).astype(BF16), sc_k)
    mrb_w = each(lambda x: (incl * x[L:]).astype(BF16), sc_b)
    st = [s_ref[bi, g] for bi, g in chains]
    from_state = each(lambda q_, s_, m_, vb_: _dot_nt(q_, s_.astype(BF16)) + _dot(m_, vb_),
                      kkd_rp, st, m_w, v_bd)
    t_inv = _unit_lower_inverse(n_w, wide_ref, expand)
    u = each(lambda t_, f_: _dot(t_.astype(BF16), expand(f_[:L])), t_inv, from_state)
    y = each(lambda f_, m_, u_: f_[L:] - _dot(m_, expand(u_)), from_state, mrb_w, u)
    for (bi, g), cs_, y_ in zip(chains, cols, y):
        o_ref[0, bi, :, cs_] = y_
    new_st = each(lambda s_, cl, v_, u_, k_, b_, p_: s_ * jnp.exp(cl) + bd * _dot_tn(
        jnp.concatenate([v_, -u_], axis=0).astype(BF16),
        jnp.concatenate([k_ * p_, b_ * p_], axis=0).astype(BF16)), st, c_last, v, u, k, kb, p_to_end)
    for (bi, g), s_ in zip(chains, new_st):
        s_ref[bi, g] = s_


def rwkv_scan(r, k_eff, v, kk, a, lw, n_ctx_chunks):
    b, t, _ = r.shape
    n_chunks = t // CHUNK
    assert b % SCAN_BATCH == 0
    bd, cum, wide = _rwkv_masks()
    idx = functools.partial(_chunk_index, n_ctx_chunks=n_ctx_chunks, n_chunks=n_chunks)
    shared = pl.BlockSpec((SCAN_BATCH, CHUNK, RW_W), lambda bi, d, c: (bi, idx(d, c), 0))
    per_dir = pl.BlockSpec((1, SCAN_BATCH, CHUNK, RW_W), lambda bi, d, c: (d, bi, idx(d, c), 0))
    n = RW_GROUP * CHUNK
    return pl.pallas_call(
        _rwkv_kernel,
        grid=(b // SCAN_BATCH, 2, n_chunks),
        in_specs=[shared, per_dir, shared, shared, per_dir, per_dir,
                  pl.BlockSpec((n, RW_GW), lambda bi, d, c: (0, 0)),
                  pl.BlockSpec((1, CHUNK, CHUNK), lambda bi, d, c: (d, 0, 0)),
                  pl.BlockSpec((1,) + wide.shape[1:], lambda bi, d, c: (d, 0, 0, 0))],
        out_specs=per_dir,
        out_shape=jax.ShapeDtypeStruct((2, b, t, RW_W), F32),
        scratch_shapes=[pltpu.VMEM((SCAN_BATCH, RW_HEADS // RW_GROUP, RW_GW, RW_GW), F32)],
        compiler_params=pltpu.CompilerParams(
            dimension_semantics=("parallel", "parallel", "arbitrary"), vmem_limit_bytes=VMEM_LIMIT),
        name="rwkv7_scan",
    )(r, k_eff, v, kk, a, lw, jnp.asarray(bd, F32), jnp.asarray(cum, BF16), jnp.asarray(wide, F32))


def _moe_kernel(e_ref, x_ref, wgu_ref, bgu_ref, wdn_ref, bdn_ref, o_ref, wgu16_ref, wdn16_ref):
    i = pl.program_id(0)
    new_expert = jnp.logical_or(i == 0, e_ref[i] != e_ref[jnp.maximum(i - 1, 0)])

    @pl.when(new_expert)
    def _():
        wgu16_ref[...] = wgu_ref[0, 0].astype(BF16)
        wdn16_ref[...] = wdn_ref[0, 0].astype(BF16)

    d_ff = wdn16_ref.shape[0]
    gu = _dot(x_ref[...], wgu16_ref[...]) + bgu_ref[0, 0]
    glu = jnp.minimum(gu[:, :d_ff], SWIGLU_LIMIT)
    lin = jnp.clip(gu[:, d_ff:], -SWIGLU_LIMIT, SWIGLU_LIMIT)
    act = glu * jax.nn.sigmoid(SWIGLU_ALPHA * glu) * (lin + 1.0)
    o_ref[...] = _dot(act.astype(BF16), wdn16_ref[...]) + bdn_ref[0, 0]


def moe_experts(xb, blk_e, layer, w_gu, b_gu, w_dn, b_dn):
    n_rows, dm = xb.shape
    d_ff = w_dn.shape[2]
    n_blocks = n_rows // MOE_ROWS
    return pl.pallas_call(
        _moe_kernel,
        grid_spec=pltpu.PrefetchScalarGridSpec(
            num_scalar_prefetch=1,
            grid=(n_blocks,),
            in_specs=[pl.BlockSpec((MOE_ROWS, dm), lambda i, e: (i, 0)),
                      pl.BlockSpec((1, 1, dm, 2 * d_ff), lambda i, e: (layer, e[i], 0, 0)),
                      pl.BlockSpec((1, 1, 1, 2 * d_ff), lambda i, e: (layer, e[i], 0, 0)),
                      pl.BlockSpec((1, 1, d_ff, dm), lambda i, e: (layer, e[i], 0, 0)),
                      pl.BlockSpec((1, 1, 1, dm), lambda i, e: (layer, e[i], 0, 0))],
            out_specs=pl.BlockSpec((MOE_ROWS, dm), lambda i, e: (i, 0)),
            scratch_shapes=[pltpu.VMEM((dm, 2 * d_ff), BF16), pltpu.VMEM((d_ff, dm), BF16)]),
        out_shape=jax.ShapeDtypeStruct((n_rows, dm), F32),
        compiler_params=pltpu.CompilerParams(dimension_semantics=("arbitrary",),
                                             vmem_limit_bytes=VMEM_LIMIT),
        name="moe_experts",
    )(blk_e, xb, w_gu, b_gu, w_dn, b_dn)


def moe_ffn(t, router_w, router_b, layer, w_gu, b_gu, w_dn, b_dn):
    n, dm = t.shape
    logits = jnp.dot(t, router_w, precision=lax.Precision.HIGHEST) + router_b
    top_logit, top_e = lax.top_k(logits, TOP_K)
    top_w = jax.nn.softmax(top_logit, axis=-1)
    flat_e = top_e.reshape(-1).astype(jnp.int32)
    n_asg = n * TOP_K
    seg = 128
    assert n_asg % seg == 0
    onehot = (flat_e.reshape(n_asg // seg, seg, 1) == jnp.arange(N_EXPERTS, dtype=jnp.int32)).astype(F32)
    within = jnp.einsum('ts,bse->bte', jnp.tril(jnp.ones((seg, seg), F32)), onehot)
    seg_tot = within[:, -1, :]
    before = jnp.cumsum(seg_tot, axis=0) - seg_tot
    rank = (jnp.sum(onehot * (within + before[:, None, :]), axis=-1) - 1.0).astype(jnp.int32).reshape(-1)
    counts = jnp.sum(seg_tot, axis=0).astype(jnp.int32)
    padded = (counts + MOE_ROWS - 1) // MOE_ROWS * MOE_ROWS
    pad_end = jnp.cumsum(padded)
    pad_start = pad_end - padded
    grp_start = jnp.cumsum(counts) - counts
    n_blocks = -(-n_asg // MOE_ROWS) + N_EXPERTS
    n_rows = n_blocks * MOE_ROWS
    blk_e = jnp.minimum(jnp.searchsorted(pad_end, jnp.arange(n_blocks, dtype=jnp.int32) * MOE_ROWS, side='right'),
                        N_EXPERTS - 1).astype(jnp.int32)
    order = jnp.argsort(flat_e)
    e_row = jnp.repeat(blk_e, MOE_ROWS)
    j = jnp.arange(n_rows, dtype=jnp.int32) - pad_start[e_row]
    used = j < counts[e_row]
    row_tok = jnp.where(used, order[jnp.where(used, grp_start[e_row] + j, 0)] // TOP_K, 0)
    xb = t.astype(BF16)[row_tok]
    yb = moe_experts(xb, blk_e, layer, w_gu, b_gu[:, :, None, :], w_dn, b_dn[:, :, None, :])
    slot = (pad_start[flat_e] + rank).reshape(n, TOP_K)
    y = jnp.zeros((n, dm), F32)
    for j in range(TOP_K):
        y = y + yb[slot[:, j]] * top_w[:, j:j + 1]
    return y


def rms_norm(x, g):
    return x * lax.rsqrt(jnp.mean(x * x, axis=-1, keepdims=True) + EPS) * g


def head_rms_norm(o, g):
    return o * lax.rsqrt(jnp.mean(o * o, axis=-1, keepdims=True) + EPS) * g


def dwconv3(x, w, n_ctx):
    t = x.shape[1]
    xp = jnp.pad(x, ((0, 0), (1, 1), (0, 0)))
    pos = jnp.arange(t)[None, :, None]
    has_prev = (pos != 0) & (pos != n_ctx)
    has_next = (pos != n_ctx - 1) & (pos != t - 1)
    return (w[0] * jnp.where(has_prev, xp[:, :-2], 0.0) + w[1] * x
            + w[2] * jnp.where(has_next, xp[:, 2:], 0.0))


def rope_tables(n_ctx, seq):
    rows = seq // GRID_W
    pos_r = jnp.repeat(jnp.arange(rows, dtype=F32), GRID_W)
    pos_c = jnp.tile(jnp.arange(GRID_W, dtype=F32), rows)
    nf = RET_DK // 4
    inv = ROPE_BASE ** (-jnp.arange(nf, dtype=F32) / nf)
    ang = jnp.concatenate([pos_r[:, None] * inv, pos_c[:, None] * inv], axis=-1)
    cos = jnp.concatenate([jnp.ones((n_ctx, RET_DK // 2), F32), jnp.cos(ang)], axis=0)
    sin = jnp.concatenate([jnp.zeros((n_ctx, RET_DK // 2), F32), jnp.sin(ang)], axis=0)
    return cos, sin


def apply_rope(x, cos, sin):
    half = x.shape[-1] // 2
    x1, x2 = x[..., :half], x[..., half:]
    c, s = cos[None, :, None, :], sin[None, :, None, :]
    return jnp.concatenate([x1 * c - x2 * s, x1 * s + x2 * c], axis=-1)


def project(u, w):
    b, t, dm = u.shape
    m = w.shape[1]
    m_pad = -(-m // 128) * 128
    w16 = jnp.pad(w, ((0, 0), (0, m_pad - m))).astype(BF16)
    z = matmul(u.reshape(b * t, dm).astype(BF16), w16)
    return z[:, :m].reshape(b, t, m)


def mixer_ab(u, n_ctx, rope, w_in, w_out, ret_decay, ret_norm, ml_conv, ml_gate_b, ml_norm):
    b, t, _ = u.shape
    n_ctx_chunks = n_ctx // CHUNK
    rq, rk, rv, rg, mqk, mv, mo, mg = _split(project(u, w_in), AB_SPLITS)
    rq = apply_rope(rq.reshape(b, t, RET_HEADS, RET_DK), *rope).reshape(b, t, RET_QK)
    rk = apply_rope(rk.reshape(b, t, RET_HEADS, RET_DK) * RET_DK ** -0.5, *rope).reshape(b, t, RET_QK)
    log_gamma = jnp.log1p(-jnp.exp(ret_decay))
    ret = retention_scan(rq, rk, rv, log_gamma, n_ctx_chunks)
    ret = ret[0] + ret[1]

    mq, mk = _split(jax.nn.silu(dwconv3(mqk, ml_conv, n_ctx)), (ML_QK, ML_QK))
    mq = mq * ML_DK ** -0.5
    gates = (mg + ml_gate_b).reshape(b, t, 2, 2, ML_HEADS)
    gates = jnp.stack([gates[:, :, :, 0], jax.nn.log_sigmoid(gates[:, :, :, 1])], axis=3)
    g_col = jnp.moveaxis(gates, 2, 0).reshape(2, b, t, 2 * ML_HEADS)
    g_row = jnp.swapaxes(g_col.reshape(2, b, t // CHUNK, CHUNK, 2 * ML_HEADS), 3, 4)
    ml = mlstm_scan(mq, mk, mv, g_col, g_row, n_ctx_chunks)
    ml = ml[0] + ml[1]

    ret = head_rms_norm(ret.reshape(b, t, RET_HEADS, RET_DV), ret_norm).reshape(b, t, RET_W) * jax.nn.silu(rg)
    ml = head_rms_norm(ml.reshape(b, t, ML_HEADS, ML_DV), ml_norm).reshape(b, t, ML_W) * jax.nn.sigmoid(mo)
    return project(jnp.concatenate([ret, ml], axis=-1), w_out)


def mixer_cd(u, n_ctx, w_in, w_out, lb, hg_norm, rw_shift, rw_w0, rw_w2, rw_a0, rw_a2, rw_g2,
             rw_kk_scale, rw_k_a, rw_r_k, rw_norm):
    b, t, _ = u.shape
    n_ctx_chunks = n_ctx // CHUNK
    hq, hf, hi, hg, zr = _split(project(u, w_in), CD_SPLITS)
    hgo = hgrn_scan(hq, hf, hi, lb.reshape(1, HG_K), n_ctx_chunks)
    hgo = hgo[0] + hgo[1]

    r, k, v, wl, al, gl = _split(dwconv3(zr, rw_shift, n_ctx), RWKV_SPLITS)
    kk = (k * rw_kk_scale).reshape(b, t, RW_HEADS, RW_HD)
    kk = (kk * lax.rsqrt(jnp.sum(kk * kk, axis=-1, keepdims=True) + EPS)).reshape(b, t, RW_W)
    gate = jax.nn.sigmoid(gl) @ rw_g2
    k_eff, a_all, lw_all = [], [], []
    for d in range(2):
        wl_d = wl[..., d * RW_W_RANK:(d + 1) * RW_W_RANK]
        al_d = al[..., d * RW_A_RANK:(d + 1) * RW_A_RANK]
        w_log = -jax.nn.softplus(-(rw_w0[d] + jnp.tanh(wl_d) @ rw_w2[d])) - 0.5
        lw_all.append(-jnp.exp(w_log))
        a = jax.nn.sigmoid(rw_a0[d] + al_d @ rw_a2[d])
        a_all.append(a)
        k_eff.append(k * (1.0 + (a - 1.0) * rw_k_a))
    k_eff, a_all, lw_all = jnp.stack(k_eff), jnp.stack(a_all), jnp.stack(lw_all)
    rwo = rwkv_scan(r, k_eff, v, kk, a_all, lw_all, n_ctx_chunks)
    rwo = (rwo[0] + rwo[1]).reshape(b, t, RW_HEADS, RW_HD)

    r4, v4 = r.reshape(b, t, RW_HEADS, RW_HD), v.reshape(b, t, RW_HEADS, RW_HD)
    bonus = sum(jnp.sum(r4 * k_eff[d].reshape(b, t, RW_HEADS, RW_HD) * rw_r_k, axis=-1, keepdims=True) * v4
                for d in range(2))
    hgo = head_rms_norm(hgo.reshape(b, t, HG_HEADS, HG_DV), hg_norm).reshape(b, t, HG_W) * jax.nn.sigmoid(hg)
    rwo = (head_rms_norm(rwo, rw_norm) + bonus).reshape(b, t, RW_W) * gate
    return project(jnp.concatenate([hgo, rwo], axis=-1), w_out)


def hgrn_lower_bound(p, layer):
    sm = jax.nn.softmax(p, axis=0)
    return jnp.cumsum(sm, axis=0)[layer] - sm[0]


def kernel(x, c, ctx, c_ctx, mod_w, mod_b, norm_mix, norm_ffn, norm_final, ab_w_in, ab_w_out, ret_decay, ret_norm, mlstm_conv, mlstm_gate_b, mlstm_norm, cd_w_in, cd_w_out, hgrn_lb, hgrn_norm, rwkv_shift, rwkv_w0, rwkv_w2, rwkv_a0, rwkv_a2, rwkv_g2, rwkv_kk_scale, rwkv_k_a, rwkv_r_k, rwkv_norm, router_w, router_b, exp_w_gate_up, exp_b_gate_up, exp_w_down, exp_b_down):
    bsz, seq, dm = x.shape
    n_ctx = ctx.shape[1]
    depth = mod_w.shape[0]
    t = n_ctx + seq
    rope = rope_tables(n_ctx, seq)
    cond_l = jax.nn.silu(c)
    cond_c = jax.nn.silu(c_ctx)[None, :]
    h = jnp.concatenate([ctx, x], axis=1)
    hp = lax.Precision.HIGHEST

    is_ctx = (jnp.arange(t) < n_ctx)[None, :, None]

    def per_token(m_c, m_l):
        return jnp.where(is_ctx, m_c[:, None, :], m_l[:, None, :])

    for i in range(depth):
        j = i // 2
        m_l = jnp.split(jnp.dot(cond_l, mod_w[i], precision=hp) + mod_b[i], 6, axis=-1)
        m_c = jnp.split(jnp.dot(cond_c, mod_w[i], precision=hp) + mod_b[i], 6, axis=-1)
        m = [per_token(a, b_) for a, b_ in zip(m_c, m_l)]
        u = rms_norm(h, norm_mix[i]) * (1.0 + m[1]) + m[0]
        if i % 2 == 0:
            y = mixer_ab(u, n_ctx, rope, ab_w_in[j], ab_w_out[j], ret_decay[j], ret_norm[j],
                         mlstm_conv[j], mlstm_gate_b[j], mlstm_norm[j])
        else:
            y = mixer_cd(u, n_ctx, cd_w_in[j], cd_w_out[j], hgrn_lower_bound(hgrn_lb, i), hgrn_norm[j],
                         rwkv_shift[j], rwkv_w0[j], rwkv_w2[j], rwkv_a0[j], rwkv_a2[j], rwkv_g2[j],
                         rwkv_kk_scale[j], rwkv_k_a[j], rwkv_r_k[j], rwkv_norm[j])
        h = h + m[2] * y
        vv = rms_norm(h, norm_ffn[i]) * (1.0 + m[4]) + m[3]
        f = moe_ffn(vv.reshape(-1, dm), router_w[i], router_b[i], i, exp_w_gate_up, exp_b_gate_up,
                    exp_w_down, exp_b_down).reshape(vv.shape)
        h = h + m[5] * f
    return rms_norm(h[:, n_ctx:], norm_final)
```

```python
import functools

import numpy as np
import jax
import jax.numpy as jnp
from jax import lax
from jax.experimental import pallas as pl
from jax.experimental.pallas import tpu as pltpu

F32 = jnp.float32
BF16 = jnp.bfloat16

CHUNK = 64
GRID_W = 64
EPS = 1e-6
ROPE_BASE = 10000.0
RET_HEADS, RET_DK, RET_DV = 4, 64, 128
ML_HEADS, ML_DK, ML_DV = 4, 64, 128
HG_HEADS, HG_DK, HG_DV = 4, 128, 128
RW_HEADS, RW_HD = 8, 64
RW_W_RANK, RW_A_RANK, RW_G_RANK = 64, 64, 128
N_EXPERTS, TOP_K = 32, 4
SWIGLU_LIMIT, SWIGLU_ALPHA = 7.0, 1.702

RET_QK, RET_W = RET_HEADS * RET_DK, RET_HEADS * RET_DV
ML_QK, ML_W = ML_HEADS * ML_DK, ML_HEADS * ML_DV
HG_K, HG_W = HG_HEADS * HG_DK, HG_HEADS * HG_DV
RW_W = RW_HEADS * RW_HD
AB_SPLITS = (RET_QK, RET_QK, RET_W, RET_W, 2 * ML_QK, ML_W, ML_W, 4 * ML_HEADS)
RWKV_SPLITS = (RW_W, RW_W, RW_W, 2 * RW_W_RANK, 2 * RW_A_RANK, RW_G_RANK)
RWKV_IN = sum(RWKV_SPLITS)
CD_SPLITS = (HG_K, 2 * HG_K, HG_W, HG_W, RWKV_IN)

RW_GROUP = 4
RW_GW = RW_GROUP * RW_HD
RW_BASE = 8
HG_LEVELS = 6

SCAN_BATCH = 4
MM_ROWS = 256
MOE_ROWS = 512
MOE_SPLIT = 2
VMEM_LIMIT = 56 * 1024 * 1024


def _split(z, sizes):
    return jnp.split(z, [int(s) for s in np.cumsum(sizes)[:-1]], axis=-1)


def _dot(a, b):
    return jnp.dot(a, b, preferred_element_type=F32)


def _dot_nt(a, b):
    return lax.dot_general(a, b, (((1,), (1,)), ((), ())), preferred_element_type=F32)


def _dot_tn(a, b):
    return lax.dot_general(a, b, (((0,), (0,)), ((), ())), preferred_element_type=F32)


def _mask_dot(m16, x):
    hi = x.astype(BF16)
    lo = (x - hi.astype(F32)).astype(BF16)
    return _dot(m16, hi) + _dot(m16, lo)


def _chunk_index(d, c, n_ctx_chunks, n_chunks):
    rev = jnp.where(c < n_ctx_chunks, n_ctx_chunks - 1 - c, n_chunks + n_ctx_chunks - 1 - c)
    return jnp.where(d == 0, c, rev)


def _mm_kernel(x_ref, w_ref, o_ref):
    o_ref[...] = _dot(x_ref[...], w_ref[...])


def matmul(x, w):
    n, k = x.shape
    m = w.shape[1]
    assert n % MM_ROWS == 0 and m % 128 == 0
    return pl.pallas_call(
        _mm_kernel,
        grid=(n // MM_ROWS,),
        in_specs=[pl.BlockSpec((MM_ROWS, k), lambda i: (i, 0)),
                  pl.BlockSpec((k, m), lambda i: (0, 0))],
        out_specs=pl.BlockSpec((MM_ROWS, m), lambda i: (i, 0)),
        out_shape=jax.ShapeDtypeStruct((n, m), F32),
        compiler_params=pltpu.CompilerParams(dimension_semantics=("parallel",),
                                             vmem_limit_bytes=VMEM_LIMIT),
        name="dense_proj",
    )(x, w)


def _ret_kernel(lg_ref, q_ref, k_ref, v_ref, o_ref, s_ref):
    d = pl.program_id(1)
    c = pl.program_id(2)

    @pl.when(c == 0)
    def _():
        s_ref[...] = jnp.zeros_like(s_ref)

    L = CHUNK
    ti = lax.broadcasted_iota(jnp.int32, (L, L), 0)
    si = lax.broadcasted_iota(jnp.int32, (L, L), 1)
    dist = jnp.where(d == 0, ti - si, si - ti)
    causal = dist >= 0
    distf = jnp.maximum(dist, 0).astype(F32)
    row = lax.broadcasted_iota(jnp.int32, (L, 1), 0)
    pos = jnp.where(d == 0, row, L - 1 - row).astype(F32)
    lg = [lg_ref[d, h] for h in range(RET_HEADS)]
    dm = [jnp.where(causal, jnp.exp(x * distf), 0.0) for x in lg]
    q_dec = [jnp.exp(x * (pos + 1.0)) for x in lg]
    k_dec = [jnp.exp(x * (L - 1.0 - pos)) for x in lg]
    chains = [(bi, h) for bi in range(SCAN_BATCH) for h in range(RET_HEADS)]
    qh = [q_ref[bi, :, h * RET_DK:(h + 1) * RET_DK] for bi, h in chains]
    kh = [k_ref[bi, :, h * RET_DK:(h + 1) * RET_DK] for bi, h in chains]
    vh = [v_ref[bi, :, h * RET_DV:(h + 1) * RET_DV].astype(BF16) for bi, h in chains]
    att = [_dot_nt(q_.astype(BF16), k_.astype(BF16)) * dm[h] for q_, k_, (_, h) in zip(qh, kh, chains)]
    s = [s_ref[bi, h] for bi, h in chains]
    o = [_dot((q_ * q_dec[h]).astype(BF16), s_.astype(BF16)) + _dot(a_.astype(BF16), v_)
         for q_, s_, a_, v_, (_, h) in zip(qh, s, att, vh, chains)]
    for (bi, h), o_ in zip(chains, o):
        o_ref[0, bi, :, h * RET_DV:(h + 1) * RET_DV] = o_
    new_s = [jnp.exp(lg[h] * L) * s_ + _dot_tn((k_ * k_dec[h]).astype(BF16), v_)
             for s_, k_, v_, (_, h) in zip(s, kh, vh, chains)]
    for (bi, h), s_ in zip(chains, new_s):
        s_ref[bi, h] = s_


def retention_scan(q, k, v, log_gamma, n_ctx_chunks):
    b, t, _ = q.shape
    n_chunks = t // CHUNK
    idx = functools.partial(_chunk_index, n_ctx_chunks=n_ctx_chunks, n_chunks=n_chunks)
    in_map = lambda bi, d, c, lg: (bi, idx(d, c), 0)
    return pl.pallas_call(
        _ret_kernel,
        grid_spec=pltpu.PrefetchScalarGridSpec(
            num_scalar_prefetch=1,
            grid=(b // SCAN_BATCH, 2, n_chunks),
            in_specs=[pl.BlockSpec((SCAN_BATCH, CHUNK, RET_QK), in_map),
                      pl.BlockSpec((SCAN_BATCH, CHUNK, RET_QK), in_map),
                      pl.BlockSpec((SCAN_BATCH, CHUNK, RET_W), in_map)],
            out_specs=pl.BlockSpec((1, SCAN_BATCH, CHUNK, RET_W), lambda bi, d, c, lg: (d, bi, idx(d, c), 0)),
            scratch_shapes=[pltpu.VMEM((SCAN_BATCH, RET_HEADS, RET_DK, RET_DV), F32)]),
        out_shape=jax.ShapeDtypeStruct((2, b, t, RET_W), F32),
        compiler_params=pltpu.CompilerParams(
            dimension_semantics=("parallel", "parallel", "arbitrary"), vmem_limit_bytes=VMEM_LIMIT),
        name="retention_scan",
    )(log_gamma, q, k, v)


def _mlstm_kernel(q_ref, k_ref, v_ref, gc_ref, gr_ref, o_ref, c_ref, m_ref):
    d = pl.program_id(1)
    c = pl.program_id(2)

    @pl.when(c == 0)
    def _():
        c_ref[...] = jnp.zeros_like(c_ref)
        m_ref[...] = jnp.zeros_like(m_ref)

    L = CHUNK
    ti = lax.broadcasted_iota(jnp.int32, (L, L), 0)
    si = lax.broadcasted_iota(jnp.int32, (L, L), 1)
    dist = jnp.where(d == 0, ti - si, si - ti)
    le = dist >= 0
    le_t = dist <= 0
    lane = lax.broadcasted_iota(jnp.int32, (L, ML_DV), 1)
    ones_col = jnp.where(lane == 0, 1.0, 0.0).astype(BF16)
    H = ML_HEADS
    chains = [(bi, h) for bi in range(SCAN_BATCH) for h in range(H)]
    each = lambda fn, *lists: [fn(*xs) for xs in zip(*lists)]
    gc = [gc_ref[0, bi] for bi in range(SCAN_BATCH)]
    gr = [gr_ref[0, bi, 0] for bi in range(SCAN_BATCH)]
    i_col = [gc[bi][:, h:h + 1] for bi, h in chains]
    f_col = [gc[bi][:, H + h:H + h + 1] for bi, h in chains]
    i_row = [gr[bi][h:h + 1, :] for bi, h in chains]
    f_row = [gr[bi][H + h:H + h + 1, :] for bi, h in chains]
    b_col = each(lambda f_: jnp.sum(jnp.where(le, f_, 0.0), axis=1, keepdims=True), f_row)
    b_row = each(lambda f_: jnp.sum(jnp.where(le_t, f_, 0.0), axis=0, keepdims=True), f_col)
    b_last = each(lambda f_: jnp.sum(f_, axis=1, keepdims=True), f_row)
    m = [m_ref[bi * H + h:bi * H + h + 1, 0:1] for bi, h in chains]
    log_d = each(lambda bc, br, ir: jnp.where(le, bc - br + ir, -jnp.inf), b_col, b_row, i_row)
    inter = each(lambda bc, m_: bc + m_, b_col, m)
    m_row = each(lambda in_, ld: jnp.maximum(in_, jnp.max(ld, axis=1, keepdims=True)), inter, log_d)
    qh = [q_ref[bi, :, h * ML_DK:(h + 1) * ML_DK].astype(BF16) for bi, h in chains]
    kh = [k_ref[bi, :, h * ML_DK:(h + 1) * ML_DK] for bi, h in chains]
    v_aug = [jnp.concatenate([v_ref[bi, :, h * ML_DV:(h + 1) * ML_DV].astype(BF16), ones_col], axis=1)
             for bi, h in chains]
    s = each(lambda q_, k_, ld, mr: _dot_nt(q_, k_.astype(BF16)) * jnp.exp(ld - mr), qh, kh, log_d, m_row)
    c_aug = [c_ref[bi, h] for bi, h in chains]
    num = each(lambda in_, mr, q_, c_, s_, v_: jnp.exp(in_ - mr) * _dot(q_, c_.astype(BF16))
               + _dot(s_.astype(BF16), v_), inter, m_row, qh, c_aug, s, v_aug)
    hh = each(lambda n_, mr: n_[:, :ML_DV] / jnp.maximum(jnp.abs(n_[:, ML_DV:ML_DV + 1]), jnp.exp(-mr)), num, m_row)
    for (bi, h), h_ in zip(chains, hh):
        o_ref[0, bi, :, h * ML_DV:(h + 1) * ML_DV] = h_
    m_new = each(lambda bl, m_, br, ir: jnp.maximum(bl + m_, jnp.max(bl - br + ir, axis=1, keepdims=True)),
                 b_last, m, b_row, i_row)
    new_c = each(lambda bl, m_, mn, c_, k_, bc, ic, v_: jnp.exp(bl + m_ - mn) * c_ + _dot_tn(
        (k_ * jnp.exp(bl - bc + ic - mn)).astype(BF16), v_), b_last, m, m_new, c_aug, kh, b_col, i_col, v_aug)
    for (bi, h), c_, mn in zip(chains, new_c, m_new):
        c_ref[bi, h] = c_
        m_ref[bi * H + h:bi * H + h + 1, :] = jnp.broadcast_to(mn, (1, 128))


def mlstm_scan(q, k, v, g_col, g_row, n_ctx_chunks):
    b, t, _ = q.shape
    n_chunks = t // CHUNK
    idx = functools.partial(_chunk_index, n_ctx_chunks=n_ctx_chunks, n_chunks=n_chunks)
    in_map = lambda bi, d, c: (bi, idx(d, c), 0)
    return pl.pallas_call(
        _mlstm_kernel,
        grid=(b // SCAN_BATCH, 2, n_chunks),
        in_specs=[pl.BlockSpec((SCAN_BATCH, CHUNK, ML_QK), in_map),
                  pl.BlockSpec((SCAN_BATCH, CHUNK, ML_QK), in_map),
                  pl.BlockSpec((SCAN_BATCH, CHUNK, ML_W), in_map),
                  pl.BlockSpec((1, SCAN_BATCH, CHUNK, 2 * ML_HEADS), lambda bi, d, c: (d, bi, idx(d, c), 0)),
                  pl.BlockSpec((1, SCAN_BATCH, 1, 2 * ML_HEADS, CHUNK), lambda bi, d, c: (d, bi, idx(d, c), 0, 0))],
        out_specs=pl.BlockSpec((1, SCAN_BATCH, CHUNK, ML_W), lambda bi, d, c: (d, bi, idx(d, c), 0)),
        out_shape=jax.ShapeDtypeStruct((2, b, t, ML_W), F32),
        scratch_shapes=[pltpu.VMEM((SCAN_BATCH, ML_HEADS, ML_DK, 2 * ML_DV), F32),
                        pltpu.VMEM((SCAN_BATCH * ML_HEADS, 128), F32)],
        compiler_params=pltpu.CompilerParams(
            dimension_semantics=("parallel", "parallel", "arbitrary"), vmem_limit_bytes=VMEM_LIMIT),
        name="mlstm_scan",
    )(q, k, v, g_col, g_row)


def _hgrn_masks():
    L = CHUNK
    stack = np.zeros((2, (1 + 2 * HG_LEVELS) * L, L), np.float32)
    pair = np.zeros((2, HG_LEVELS + 1, L, L), np.float32)
    for d in range(2):
        p = np.arange(L) if d == 0 else L - 1 - np.arange(L)
        pt, ps = p[:, None], p[None, :]
        stack[d, :L] = ps <= pt
        for l in range(HG_LEVELS):
            parent, half = p >> (l + 1), (p >> l) & 1
            split = parent * (2 << l) + (1 << l) - 1
            same = parent[:, None] == parent[None, :]
            e = same & (half[:, None] == 1) & (ps > split[:, None]) & (ps <= pt)
            f = same & (half[:, None] == 0) & (ps > pt) & (ps <= split[:, None])
            stack[d, (1 + 2 * l) * L:(2 + 2 * l) * L] = e
            stack[d, (2 + 2 * l) * L:(3 + 2 * l) * L] = f
            pair[d, l] = same & (half[:, None] == 1) & (half[None, :] == 0)
        pair[d, HG_LEVELS] = np.eye(L)
    return stack, pair


def _hgrn_kernel(q_ref, hf_ref, v_ref, lb_ref, stack_ref, pair_ref, o_ref, s_ref):
    c = pl.program_id(2)

    @pl.when(c == 0)
    def _():
        s_ref[...] = jnp.zeros_like(s_ref)

    L = CHUNK
    lb = lb_ref[...]
    f = lb + (1.0 - lb) * jax.nn.sigmoid(jnp.concatenate([hf_ref[bi] for bi in range(SCAN_BATCH)], axis=1))
    kk = 1.0 - f
    g = jnp.log(f)
    sums = _mask_dot(stack_ref[0], g)
    chains = [(bi, h) for bi in range(SCAN_BATCH) for h in range(HG_HEADS)]
    each = lambda fn, *lists: [fn(*xs) for xs in zip(*lists)]
    hs = [slice(bi * HG_K + h * HG_DK, bi * HG_K + (h + 1) * HG_DK) for bi, h in chains]
    qh = [q_ref[bi, :, h * HG_DK:(h + 1) * HG_DK] for bi, h in chains]
    kh = [kk[:, s_] for s_ in hs]
    vh = [v_ref[bi, :, h * HG_DV:(h + 1) * HG_DV].astype(BF16) for bi, h in chains]
    b = [sums[0:L, s_] for s_ in hs]
    att = each(lambda q_, k_: pair_ref[0, HG_LEVELS] * _dot_nt(q_.astype(BF16), k_.astype(BF16)), qh, kh)
    for l in range(HG_LEVELS):
        e = [sums[(1 + 2 * l) * L:(2 + 2 * l) * L, s_] for s_ in hs]
        fl = [sums[(2 + 2 * l) * L:(3 + 2 * l) * L, s_] for s_ in hs]
        att = each(lambda a_, q_, k_, e_, f_: a_ + pair_ref[0, l] * _dot_nt(
            (q_ * jnp.exp(e_)).astype(BF16), (k_ * jnp.exp(f_)).astype(BF16)), att, qh, kh, e, fl)
    b_last = [jnp.sum(g[:, s_], axis=0, keepdims=True) for s_ in hs]
    st = [s_ref[bi, h] for bi, h in chains]
    o = each(lambda q_, b_, s_, a_, v_: _dot_nt((q_ * jnp.exp(b_)).astype(BF16), s_.astype(BF16))
             + _dot(a_.astype(BF16), v_), qh, b, st, att, vh)
    for (bi, h), o_ in zip(chains, o):
        o_ref[0, bi, :, h * HG_DV:(h + 1) * HG_DV] = o_
    new_st = each(lambda s_, bl, v_, k_, b_: s_ * jnp.exp(bl) + _dot_tn(v_, (k_ * jnp.exp(bl - b_)).astype(BF16)),
                  st, b_last, vh, kh, b)
    for (bi, h), s_ in zip(chains, new_st):
        s_ref[bi, h] = s_


def hgrn_scan(q, hf, v, lb, n_ctx_chunks):
    b, t, _ = q.shape
    n_chunks = t // CHUNK
    stack, pair = _hgrn_masks()
    idx = functools.partial(_chunk_index, n_ctx_chunks=n_ctx_chunks, n_chunks=n_chunks)
    in_map = lambda bi, d, c: (bi, idx(d, c), 0)
    return pl.pallas_call(
        _hgrn_kernel,
        grid=(b // SCAN_BATCH, 2, n_chunks),
        in_specs=[pl.BlockSpec((SCAN_BATCH, CHUNK, HG_K), in_map),
                  pl.BlockSpec((SCAN_BATCH, CHUNK, HG_K), lambda bi, d, c: (bi, idx(d, c), d)),
                  pl.BlockSpec((SCAN_BATCH, CHUNK, HG_W), in_map),
                  pl.BlockSpec((1, SCAN_BATCH * HG_K), lambda bi, d, c: (0, 0)),
                  pl.BlockSpec((1,) + stack.shape[1:], lambda bi, d, c: (d, 0, 0)),
                  pl.BlockSpec((1,) + pair.shape[1:], lambda bi, d, c: (d, 0, 0, 0))],
        out_specs=pl.BlockSpec((1, SCAN_BATCH, CHUNK, HG_W), lambda bi, d, c: (d, bi, idx(d, c), 0)),
        out_shape=jax.ShapeDtypeStruct((2, b, t, HG_W), F32),
        scratch_shapes=[pltpu.VMEM((SCAN_BATCH, HG_HEADS, HG_DV, HG_DK), F32)],
        compiler_params=pltpu.CompilerParams(
            dimension_semantics=("parallel", "parallel", "arbitrary"), vmem_limit_bytes=VMEM_LIMIT),
        name="hgrn_scan",
    )(q, hf, v, jnp.tile(lb, (1, SCAN_BATCH)), jnp.asarray(stack, BF16), jnp.asarray(pair, F32))


def _rwkv_masks():
    L, G = CHUNK, RW_GROUP
    n = G * L
    head = np.arange(n) // L
    bd = (head[:, None] == head[None, :]).astype(np.float32)
    cum = np.zeros((2, L, L), np.float32)
    n_lvl = int(np.log2(L // RW_BASE))
    wide = np.zeros((2, 4 + n_lvl, L, n), np.float32)
    for d in range(2):
        p = np.arange(L) if d == 0 else L - 1 - np.arange(L)
        pt, ps = p[:, None], p[None, :]
        cum[d] = ps <= pt
        m = [ps < pt, ps <= pt, np.eye(L, dtype=bool), (pt // RW_BASE) == (ps // RW_BASE)]
        for i in range(n_lvl):
            blk = RW_BASE << i
            parent, half = p // (2 * blk), (p // blk) % 2
            m.append((parent[:, None] == parent[None, :]) & (half[:, None] == 1) & (half[None, :] == 0))
        for i, mi in enumerate(m):
            wide[d, i] = np.tile(mi, (1, G))
    return bd, cum, wide


def _unit_lower_inverse(n_w, wide_ref, expand):
    mul = lambda a_w, b_w: [_dot(x.astype(BF16), expand(y)) for x, y in zip(a_w, b_w)]
    nd = [n * wide_ref[0, 3] for n in n_w]
    t_inv = [wide_ref[0, 2] - x for x in nd]
    pw = nd
    for _ in range(int(np.log2(RW_BASE)) - 1):
        pw = mul(pw, pw)
        t_inv = [t + x for t, x in zip(t_inv, mul(t_inv, pw))]
    for i in range(wide_ref.shape[1] - 4):
        tc = mul(t_inv, [n * wide_ref[0, 4 + i] for n in n_w])
        t_inv = [t - x for t, x in zip(t_inv, mul(tc, t_inv))]
    return t_inv


def _rwkv_kernel(r_ref, k_ref, v_ref, kk_ref, a_ref, lw_ref, bd_ref, cum_ref, wide_ref, o_ref, s_ref):
    c = pl.program_id(2)

    @pl.when(c == 0)
    def _():
        s_ref[...] = jnp.zeros_like(s_ref)

    G = RW_GROUP
    bd = bd_ref[...]
    strict = wide_ref[0, 0]
    incl = wide_ref[0, 1]
    cum = cum_ref[0]

    def expand(x):
        return (jnp.concatenate([x] * G, axis=0) * bd).astype(BF16)

    L = CHUNK
    chains = [(bi, g) for bi in range(SCAN_BATCH) for g in range(RW_HEADS // G)]
    cols = [slice(g * RW_GW, (g + 1) * RW_GW) for _, g in chains]
    each = lambda fn, *lists: [fn(*xs) for xs in zip(*lists)]
    r = [r_ref[bi, :, cs_] for (bi, _), cs_ in zip(chains, cols)]
    k = [k_ref[0, bi, :, cs_] for (bi, _), cs_ in zip(chains, cols)]
    v = [v_ref[bi, :, cs_] for (bi, _), cs_ in zip(chains, cols)]
    kk = [kk_ref[bi, :, cs_] for (bi, _), cs_ in zip(chains, cols)]
    lw = [lw_ref[0, bi, :, cs_] for (bi, _), cs_ in zip(chains, cols)]
    kb = [x * a_ref[0, bi, :, cs_] for x, (bi, _), cs_ in zip(kk, chains, cols)]
    cs = each(lambda x: _mask_dot(cum, x), lw)
    c_last = each(lambda x: jnp.sum(x, axis=0, keepdims=True), lw)
    p_inv = each(lambda x: jnp.exp(-x), cs)
    p_to_end = each(lambda cl, x: jnp.exp(cl - x), c_last, cs)
    kkd_rp = each(lambda kk_, r_, cs_, lw_: jnp.concatenate(
        [kk_ * jnp.exp(cs_ - lw_), r_ * jnp.exp(cs_)], axis=0).astype(BF16), kk, r, cs, lw)
    v_bd = each(expand, v)
    sc_k = each(lambda q_, k_, p_: _dot_nt(q_, expand(k_ * p_)), kkd_rp, k, p_inv)
    sc_b = each(lambda q_, b_, p_: _dot_nt(q_, expand(b_ * p_)), kkd_rp, kb, p_inv)
    n_w = each(lambda x: strict * x[:L], sc_b)
    m_w = each(lambda x: jnp.concatenate([strict * x[:L], incl * x[L:]], axis=0).astype(BF16), sc_k)
    mrb_w = each(lambda x: (incl * x[L:]).astype(BF16), sc_b)
    st = [s_ref[bi, g] for bi, g in chains]
    from_state = each(lambda q_, s_, m_, vb_: _dot_nt(q_, s_.astype(BF16)) + _dot(m_, vb_),
                      kkd_rp, st, m_w, v_bd)
    t_inv = _unit_lower_inverse(n_w, wide_ref, expand)
    u = each(lambda t_, f_: _dot(t_.astype(BF16), expand(f_[:L])), t_inv, from_state)
    y = each(lambda f_, m_, u_: f_[L:] - _dot(m_, expand(u_)), from_state, mrb_w, u)
    for (bi, g), cs_, y_ in zip(chains, cols, y):
        o_ref[0, bi, :, cs_] = y_
    new_st = each(lambda s_, cl, v_, u_, k_, b_, p_: s_ * jnp.exp(cl) + bd * _dot_tn(
        jnp.concatenate([v_, -u_], axis=0).astype(BF16),
        jnp.concatenate([k_ * p_, b_ * p_], axis=0).astype(BF16)), st, c_last, v, u, k, kb, p_to_end)
    for (bi, g), s_ in zip(chains, new_st):
        s_ref[bi, g] = s_


def rwkv_scan(r, k_eff, v, kk, a, lw, n_ctx_chunks):
    b, t, _ = r.shape
    n_chunks = t // CHUNK
    assert b % SCAN_BATCH == 0
    bd, cum, wide = _rwkv_masks()
    idx = functools.partial(_chunk_index, n_ctx_chunks=n_ctx_chunks, n_chunks=n_chunks)
    shared = pl.BlockSpec((SCAN_BATCH, CHUNK, RW_W), lambda bi, d, c: (bi, idx(d, c), 0))
    per_dir = pl.BlockSpec((1, SCAN_BATCH, CHUNK, RW_W), lambda bi, d, c: (d, bi, idx(d, c), 0))
    n = RW_GROUP * CHUNK
    return pl.pallas_call(
        _rwkv_kernel,
        grid=(b // SCAN_BATCH, 2, n_chunks),
        in_specs=[shared, per_dir, shared, shared, per_dir, per_dir,
                  pl.BlockSpec((n, RW_GW), lambda bi, d, c: (0, 0)),
                  pl.BlockSpec((1, CHUNK, CHUNK), lambda bi, d, c: (d, 0, 0)),
                  pl.BlockSpec((1,) + wide.shape[1:], lambda bi, d, c: (d, 0, 0, 0))],
        out_specs=per_dir,
        out_shape=jax.ShapeDtypeStruct((2, b, t, RW_W), F32),
        scratch_shapes=[pltpu.VMEM((SCAN_BATCH, RW_HEADS // RW_GROUP, RW_GW, RW_GW), F32)],
        compiler_params=pltpu.CompilerParams(
            dimension_semantics=("parallel", "parallel", "arbitrary"), vmem_limit_bytes=VMEM_LIMIT),
        name="rwkv7_scan",
    )(r, k_eff, v, kk, a, lw, jnp.asarray(bd, F32), jnp.asarray(cum, BF16), jnp.asarray(wide, F32))


def _moe_kernel(e_ref, x_ref, wgu_ref, bgu_ref, wdn_ref, bdn_ref, o_ref, wgu16_ref, wdn16_ref):
    i = pl.program_id(0)
    new_expert = jnp.logical_or(i == 0, e_ref[i] != e_ref[jnp.maximum(i - 1, 0)])

    @pl.when(new_expert)
    def _():
        wgu16_ref[...] = wgu_ref[0, 0].astype(BF16)
        wdn16_ref[...] = wdn_ref[0, 0].astype(BF16)

    d_ff = wdn16_ref.shape[0]
    gu = _dot(x_ref[...], wgu16_ref[...]) + bgu_ref[0, 0]
    glu = jnp.minimum(gu[:, :d_ff], SWIGLU_LIMIT)
    lin = jnp.clip(gu[:, d_ff:], -SWIGLU_LIMIT, SWIGLU_LIMIT)
    act = glu * jax.nn.sigmoid(SWIGLU_ALPHA * glu) * (lin + 1.0)
    o_ref[...] = _dot(act.astype(BF16), wdn16_ref[...]) + bdn_ref[0, 0]


def moe_experts(xb, blk_e, layer, w_gu, b_gu, w_dn, b_dn):
    n_rows, dm = xb.shape
    d_ff = w_dn.shape[2]
    n_blocks = n_rows // MOE_ROWS
    return pl.pallas_call(
        _moe_kernel,
        grid_spec=pltpu.PrefetchScalarGridSpec(
            num_scalar_prefetch=1,
            grid=(n_blocks,),
            in_specs=[pl.BlockSpec((MOE_ROWS, dm), lambda i, e: (i, 0)),
                      pl.BlockSpec((1, 1, dm, 2 * d_ff), lambda i, e: (layer, e[i], 0, 0)),
                      pl.BlockSpec((1, 1, 1, 2 * d_ff), lambda i, e: (layer, e[i], 0, 0)),
                      pl.BlockSpec((1, 1, d_ff, dm), lambda i, e: (layer, e[i], 0, 0)),
                      pl.BlockSpec((1, 1, 1, dm), lambda i, e: (layer, e[i], 0, 0))],
            out_specs=pl.BlockSpec((MOE_ROWS, dm), lambda i, e: (i, 0)),
            scratch_shapes=[pltpu.VMEM((dm, 2 * d_ff), BF16), pltpu.VMEM((d_ff, dm), BF16)]),
        out_shape=jax.ShapeDtypeStruct((n_rows, dm), F32),
        compiler_params=pltpu.CompilerParams(dimension_semantics=("arbitrary",),
                                             vmem_limit_bytes=VMEM_LIMIT),
        name="moe_experts",
    )(blk_e, xb, w_gu, b_gu, w_dn, b_dn)


def moe_ffn(t, router_w, router_b, layer, w_gu, b_gu, w_dn, b_dn):
    n, dm = t.shape
    logits = jnp.dot(t, router_w, precision=lax.Precision.HIGHEST) + router_b
    top_logit, top_e = lax.top_k(logits, TOP_K)
    top_w = jax.nn.softmax(top_logit, axis=-1)
    flat_e = top_e.reshape(-1).astype(jnp.int32)
    n_asg = n * TOP_K
    seg = 128
    assert n_asg % seg == 0
    onehot = (flat_e.reshape(n_asg // seg, seg, 1) == jnp.arange(N_EXPERTS, dtype=jnp.int32)).astype(F32)
    within = jnp.einsum('ts,bse->bte', jnp.tril(jnp.ones((seg, seg), F32)), onehot)
    seg_tot = within[:, -1, :]
    before = jnp.cumsum(seg_tot, axis=0) - seg_tot
    rank = (jnp.sum(onehot * (within + before[:, None, :]), axis=-1) - 1.0).astype(jnp.int32).reshape(-1)
    counts = jnp.sum(seg_tot, axis=0).astype(jnp.int32)
    padded = (counts + MOE_ROWS - 1) // MOE_ROWS * MOE_ROWS
    pad_end = jnp.cumsum(padded)
    pad_start = pad_end - padded
    grp_start = jnp.cumsum(counts) - counts
    n_blocks = -(-n_asg // MOE_ROWS) + N_EXPERTS
    n_rows = n_blocks * MOE_ROWS
    blk_e = jnp.minimum(jnp.searchsorted(pad_end, jnp.arange(n_blocks, dtype=jnp.int32) * MOE_ROWS, side='right'),
                        N_EXPERTS - 1).astype(jnp.int32)
    order = jnp.argsort(flat_e)
    e_row = jnp.repeat(blk_e, MOE_ROWS)
    j = jnp.arange(n_rows, dtype=jnp.int32) - pad_start[e_row]
    used = j < counts[e_row]
    row_tok = jnp.where(used, order[jnp.where(used, grp_start[e_row] + j, 0)] // TOP_K, 0)
    xb = t.astype(BF16)[row_tok]
    yb = moe_experts(xb, blk_e, layer, w_gu, b_gu[:, :, None, :], w_dn, b_dn[:, :, None, :])
    slot = (pad_start[flat_e] + rank).reshape(n, TOP_K)
    y = jnp.zeros((n, dm), F32)
    for j in range(TOP_K):
        y = y + yb[slot[:, j]] * top_w[:, j:j + 1]
    return y


def rms_norm(x, g):
    return x * lax.rsqrt(jnp.mean(x * x, axis=-1, keepdims=True) + EPS) * g


def head_rms_norm(o, g):
    return o * lax.rsqrt(jnp.mean(o * o, axis=-1, keepdims=True) + EPS) * g


def dwconv3(x, w, n_ctx):
    t = x.shape[1]
    xp = jnp.pad(x, ((0, 0), (1, 1), (0, 0)))
    pos = jnp.arange(t)[None, :, None]
    has_prev = (pos != 0) & (pos != n_ctx)
    has_next = (pos != n_ctx - 1) & (pos != t - 1)
    return (w[0] * jnp.where(has_prev, xp[:, :-2], 0.0) + w[1] * x
            + w[2] * jnp.where(has_next, xp[:, 2:], 0.0))


def rope_tables(n_ctx, seq):
    rows = seq // GRID_W
    pos_r = jnp.repeat(jnp.arange(rows, dtype=F32), GRID_W)
    pos_c = jnp.tile(jnp.arange(GRID_W, dtype=F32), rows)
    nf = RET_DK // 4
    inv = ROPE_BASE ** (-jnp.arange(nf, dtype=F32) / nf)
    ang = jnp.concatenate([pos_r[:, None] * inv, pos_c[:, None] * inv], axis=-1)
    cos = jnp.concatenate([jnp.ones((n_ctx, RET_DK // 2), F32), jnp.cos(ang)], axis=0)
    sin = jnp.concatenate([jnp.zeros((n_ctx, RET_DK // 2), F32), jnp.sin(ang)], axis=0)
    return cos, sin


def apply_rope(x, cos, sin):
    half = x.shape[-1] // 2
    x1, x2 = x[..., :half], x[..., half:]
    c, s = cos[None, :, None, :], sin[None, :, None, :]
    return jnp.concatenate([x1 * c - x2 * s, x1 * s + x2 * c], axis=-1)


def project(u, w):
    b, t, dm = u.shape
    m = w.shape[1]
    m_pad = -(-m // 128) * 128
    w16 = jnp.pad(w, ((0, 0), (0, m_pad - m))).astype(BF16)
    z = matmul(u.reshape(b * t, dm).astype(BF16), w16)
    return z[:, :m].reshape(b, t, m)


def mixer_ab(u, n_ctx, rope, w_in, w_out, ret_decay, ret_norm, ml_conv, ml_gate_b, ml_norm):
    b, t, _ = u.shape
    n_ctx_chunks = n_ctx // CHUNK
    rq, rk, rv, rg, mqk, mv, mo, mg = _split(project(u, w_in), AB_SPLITS)
    rq = apply_rope(rq.reshape(b, t, RET_HEADS, RET_DK), *rope).reshape(b, t, RET_QK)
    rk = apply_rope(rk.reshape(b, t, RET_HEADS, RET_DK) * RET_DK ** -0.5, *rope).reshape(b, t, RET_QK)
    log_gamma = jnp.log1p(-jnp.exp(ret_decay))
    ret = retention_scan(rq, rk, rv, log_gamma, n_ctx_chunks)
    ret = ret[0] + ret[1]

    mq, mk = _split(jax.nn.silu(dwconv3(mqk, ml_conv, n_ctx)), (ML_QK, ML_QK))
    mq = mq * ML_DK ** -0.5
    gates = (mg + ml_gate_b).reshape(b, t, 2, 2, ML_HEADS)
    gates = jnp.stack([gates[:, :, :, 0], jax.nn.log_sigmoid(gates[:, :, :, 1])], axis=3)
    g_col = jnp.moveaxis(gates, 2, 0).reshape(2, b, t, 2 * ML_HEADS)
    g_row = jnp.swapaxes(g_col.reshape(2, b, t // CHUNK, CHUNK, 2 * ML_HEADS), 3, 4)
    ml = mlstm_scan(mq, mk, mv, g_col, g_row, n_ctx_chunks)
    ml = ml[0] + ml[1]

    ret = head_rms_norm(ret.reshape(b, t, RET_HEADS, RET_DV), ret_norm).reshape(b, t, RET_W) * jax.nn.silu(rg)
    ml = head_rms_norm(ml.reshape(b, t, ML_HEADS, ML_DV), ml_norm).reshape(b, t, ML_W) * jax.nn.sigmoid(mo)
    return project(jnp.concatenate([ret, ml], axis=-1), w_out)


def mixer_cd(u, n_ctx, w_in, w_out, lb, hg_norm, rw_shift, rw_w0, rw_w2, rw_a0, rw_a2, rw_g2,
             rw_kk_scale, rw_k_a, rw_r_k, rw_norm):
    b, t, _ = u.shape
    n_ctx_chunks = n_ctx // CHUNK
    hq, hf, hi, hg, zr = _split(project(u, w_in), CD_SPLITS)
    hgo = hgrn_scan(hq, hf, hi, lb.reshape(1, HG_K), n_ctx_chunks)
    hgo = hgo[0] + hgo[1]

    r, k, v, wl, al, gl = _split(dwconv3(zr, rw_shift, n_ctx), RWKV_SPLITS)
    kk = (k * rw_kk_scale).reshape(b, t, RW_HEADS, RW_HD)
    kk = (kk * lax.rsqrt(jnp.sum(kk * kk, axis=-1, keepdims=True) + EPS)).reshape(b, t, RW_W)
    gate = jax.nn.sigmoid(gl) @ rw_g2
    k_eff, a_all, lw_all = [], [], []
    for d in range(2):
        wl_d = wl[..., d * RW_W_RANK:(d + 1) * RW_W_RANK]
        al_d = al[..., d * RW_A_RANK:(d + 1) * RW_A_RANK]
        w_log = -jax.nn.softplus(-(rw_w0[d] + jnp.tanh(wl_d) @ rw_w2[d])) - 0.5
        lw_all.append(-jnp.exp(w_log))
        a = jax.nn.sigmoid(rw_a0[d] + al_d @ rw_a2[d])
        a_all.append(a)
        k_eff.append(k * (1.0 + (a - 1.0) * rw_k_a))
    k_eff, a_all, lw_all = jnp.stack(k_eff), jnp.stack(a_all), jnp.stack(lw_all)
    rwo = rwkv_scan(r, k_eff, v, kk, a_all, lw_all, n_ctx_chunks)
    rwo = (rwo[0] + rwo[1]).reshape(b, t, RW_HEADS, RW_HD)

    r4, v4 = r.reshape(b, t, RW_HEADS, RW_HD), v.reshape(b, t, RW_HEADS, RW_HD)
    bonus = sum(jnp.sum(r4 * k_eff[d].reshape(b, t, RW_HEADS, RW_HD) * rw_r_k, axis=-1, keepdims=True) * v4
                for d in range(2))
    hgo = head_rms_norm(hgo.reshape(b, t, HG_HEADS, HG_DV), hg_norm).reshape(b, t, HG_W) * jax.nn.sigmoid(hg)
    rwo = (head_rms_norm(rwo, rw_norm) + bonus).reshape(b, t, RW_W) * gate
    return project(jnp.concatenate([hgo, rwo], axis=-1), w_out)


def hgrn_lower_bound(p, layer):
    sm = jax.nn.softmax(p, axis=0)
    return jnp.cumsum(sm, axis=0)[layer] - sm[0]


def kernel(x, c, ctx, c_ctx, mod_w, mod_b, norm_mix, norm_ffn, norm_final, ab_w_in, ab_w_out, ret_decay, ret_norm, mlstm_conv, mlstm_gate_b, mlstm_norm, cd_w_in, cd_w_out, hgrn_lb, hgrn_norm, rwkv_shift, rwkv_w0, rwkv_w2, rwkv_a0, rwkv_a2, rwkv_g2, rwkv_kk_scale, rwkv_k_a, rwkv_r_k, rwkv_norm, router_w, router_b, exp_w_gate_up, exp_b_gate_up, exp_w_down, exp_b_down):
    bsz, seq, dm = x.shape
    n_ctx = ctx.shape[1]
    depth = mod_w.shape[0]
    t = n_ctx + seq
    rope = rope_tables(n_ctx, seq)
    cond_l = jax.nn.silu(c)
    cond_c = jax.nn.silu(c_ctx)[None, :]
    h = jnp.concatenate([ctx, x], axis=1)
    hp = lax.Precision.HIGHEST

    is_ctx = (jnp.arange(t) < n_ctx)[None, :, None]

    def per_token(m_c, m_l):
        return jnp.where(is_ctx, m_c[:, None, :], m_l[:, None, :])

    for i in range(depth):
        j = i // 2
        m_l = jnp.split(jnp.dot(cond_l, mod_w[i], precision=hp) + mod_b[i], 6, axis=-1)
        m_c = jnp.split(jnp.dot(cond_c, mod_w[i], precision=hp) + mod_b[i], 6, axis=-1)
        m = [per_token(a, b_) for a, b_ in zip(m_c, m_l)]
        u = rms_norm(h, norm_mix[i]) * (1.0 + m[1]) + m[0]
        if i % 2 == 0:
            y = mixer_ab(u, n_ctx, rope, ab_w_in[j], ab_w_out[j], ret_decay[j], ret_norm[j],
                         mlstm_conv[j], mlstm_gate_b[j], mlstm_norm[j])
        else:
            y = mixer_cd(u, n_ctx, cd_w_in[j], cd_w_out[j], hgrn_lower_bound(hgrn_lb, i), hgrn_norm[j],
                         rwkv_shift[j], rwkv_w0[j], rwkv_w2[j], rwkv_a0[j], rwkv_a2[j], rwkv_g2[j],
                         rwkv_kk_scale[j], rwkv_k_a[j], rwkv_r_k[j], rwkv_norm[j])
        h = h + m[2] * y
        vv = rms_norm(h, norm_ffn[i]) * (1.0 + m[4]) + m[3]
        f = jnp.concatenate(
            [moe_ffn(part, router_w[i], router_b[i], i, exp_w_gate_up, exp_b_gate_up, exp_w_down, exp_b_down)
             for part in jnp.split(vv.reshape(-1, dm), MOE_SPLIT, axis=0)], axis=0).reshape(vv.shape)
        h = h + m[5] * f
    return rms_norm(h[:, n_ctx:], norm_final)
```

```python
import functools

import numpy as np
import jax
import jax.numpy as jnp
from jax import lax
from jax.experimental import pallas as pl
from jax.experimental.pallas import tpu as pltpu

F32 = jnp.float32
BF16 = jnp.bfloat16

CHUNK = 64
GRID_W = 64
EPS = 1e-6
ROPE_BASE = 10000.0
RET_HEADS, RET_DK, RET_DV = 4, 64, 128
ML_HEADS, ML_DK, ML_DV = 4, 64, 128
HG_HEADS, HG_DK, HG_DV = 4, 128, 128
RW_HEADS, RW_HD = 8, 64
RW_W_RANK, RW_A_RANK, RW_G_RANK = 64, 64, 128
N_EXPERTS, TOP_K = 32, 4
SWIGLU_LIMIT, SWIGLU_ALPHA = 7.0, 1.702

RET_QK, RET_W = RET_HEADS * RET_DK, RET_HEADS * RET_DV
ML_QK, ML_W = ML_HEADS * ML_DK, ML_HEADS * ML_DV
HG_K, HG_W = HG_HEADS * HG_DK, HG_HEADS * HG_DV
RW_W = RW_HEADS * RW_HD
AB_SPLITS = (RET_QK, RET_QK, RET_W, RET_W, 2 * ML_QK, ML_W, ML_W, 4 * ML_HEADS)
RWKV_SPLITS = (RW_W, RW_W, RW_W, 2 * RW_W_RANK, 2 * RW_A_RANK, RW_G_RANK)
RWKV_IN = sum(RWKV_SPLITS)
CD_SPLITS = (HG_K, 2 * HG_K, HG_W, HG_W, RWKV_IN)

RW_GROUP = 4
RW_GW = RW_GROUP * RW_HD
RW_BASE = 8
HG_LEVELS = 6

SCAN_BATCH = 4
MM_ROWS = 256
MOE_ROWS = 512
VMEM_LIMIT = 56 * 1024 * 1024


def _split(z, sizes):
    return jnp.split(z, [int(s) for s in np.cumsum(sizes)[:-1]], axis=-1)


def _dot(a, b):
    return jnp.dot(a, b, preferred_element_type=F32)


def _dot_nt(a, b):
    return lax.dot_general(a, b, (((1,), (1,)), ((), ())), preferred_element_type=F32)


def _dot_tn(a, b):
    return lax.dot_general(a, b, (((0,), (0,)), ((), ())), preferred_element_type=F32)


def _mask_dot(m16, x):
    hi = x.astype(BF16)
    lo = (x - hi.astype(F32)).astype(BF16)
    return _dot(m16, hi) + _dot(m16, lo)


def _chunk_index(d, c, n_ctx_chunks, n_chunks):
    rev = jnp.where(c < n_ctx_chunks, n_ctx_chunks - 1 - c, n_chunks + n_ctx_chunks - 1 - c)
    return jnp.where(d == 0, c, rev)


def _mm_kernel(x_ref, w_ref, o_ref):
    o_ref[...] = _dot(x_ref[...], w_ref[...])


def matmul(x, w):
    n, k = x.shape
    m = w.shape[1]
    assert n % MM_ROWS == 0 and m % 128 == 0
    return pl.pallas_call(
        _mm_kernel,
        grid=(n // MM_ROWS,),
        in_specs=[pl.BlockSpec((MM_ROWS, k), lambda i: (i, 0)),
                  pl.BlockSpec((k, m), lambda i: (0, 0))],
        out_specs=pl.BlockSpec((MM_ROWS, m), lambda i: (i, 0)),
        out_shape=jax.ShapeDtypeStruct((n, m), F32),
        compiler_params=pltpu.CompilerParams(dimension_semantics=("parallel",),
                                             vmem_limit_bytes=VMEM_LIMIT),
        name="dense_proj",
    )(x, w)


def _ret_kernel(lg_ref, q_ref, k_ref, v_ref, o_ref, s_ref):
    d = pl.program_id(1)
    c = pl.program_id(2)

    @pl.when(c == 0)
    def _():
        s_ref[...] = jnp.zeros_like(s_ref)

    L = CHUNK
    ti = lax.broadcasted_iota(jnp.int32, (L, L), 0)
    si = lax.broadcasted_iota(jnp.int32, (L, L), 1)
    dist = jnp.where(d == 0, ti - si, si - ti)
    causal = dist >= 0
    distf = jnp.maximum(dist, 0).astype(F32)
    row = lax.broadcasted_iota(jnp.int32, (L, 1), 0)
    pos = jnp.where(d == 0, row, L - 1 - row).astype(F32)
    lg = [lg_ref[d, h] for h in range(RET_HEADS)]
    dm = [jnp.where(causal, jnp.exp(x * distf), 0.0) for x in lg]
    q_dec = [jnp.exp(x * (pos + 1.0)) for x in lg]
    k_dec = [jnp.exp(x * (L - 1.0 - pos)) for x in lg]
    chains = [(bi, h) for bi in range(SCAN_BATCH) for h in range(RET_HEADS)]
    qh = [q_ref[bi, :, h * RET_DK:(h + 1) * RET_DK] for bi, h in chains]
    kh = [k_ref[bi, :, h * RET_DK:(h + 1) * RET_DK] for bi, h in chains]
    vh = [v_ref[bi, :, h * RET_DV:(h + 1) * RET_DV].astype(BF16) for bi, h in chains]
    att = [_dot_nt(q_.astype(BF16), k_.astype(BF16)) * dm[h] for q_, k_, (_, h) in zip(qh, kh, chains)]
    s = [s_ref[bi, h] for bi, h in chains]
    o = [_dot((q_ * q_dec[h]).astype(BF16), s_.astype(BF16)) + _dot(a_.astype(BF16), v_)
         for q_, s_, a_, v_, (_, h) in zip(qh, s, att, vh, chains)]
    for (bi, h), o_ in zip(chains, o):
        o_ref[0, bi, :, h * RET_DV:(h + 1) * RET_DV] = o_
    new_s = [jnp.exp(lg[h] * L) * s_ + _dot_tn((k_ * k_dec[h]).astype(BF16), v_)
             for s_, k_, v_, (_, h) in zip(s, kh, vh, chains)]
    for (bi, h), s_ in zip(chains, new_s):
        s_ref[bi, h] = s_


def retention_scan(q, k, v, log_gamma, n_ctx_chunks):
    b, t, _ = q.shape
    n_chunks = t // CHUNK
    idx = functools.partial(_chunk_index, n_ctx_chunks=n_ctx_chunks, n_chunks=n_chunks)
    in_map = lambda bi, d, c, lg: (bi, idx(d, c), 0)
    return pl.pallas_call(
        _ret_kernel,
        grid_spec=pltpu.PrefetchScalarGridSpec(
            num_scalar_prefetch=1,
            grid=(b // SCAN_BATCH, 2, n_chunks),
            in_specs=[pl.BlockSpec((SCAN_BATCH, CHUNK, RET_QK), in_map),
                      pl.BlockSpec((SCAN_BATCH, CHUNK, RET_QK), in_map),
                      pl.BlockSpec((SCAN_BATCH, CHUNK, RET_W), in_map)],
            out_specs=pl.BlockSpec((1, SCAN_BATCH, CHUNK, RET_W), lambda bi, d, c, lg: (d, bi, idx(d, c), 0)),
            scratch_shapes=[pltpu.VMEM((SCAN_BATCH, RET_HEADS, RET_DK, RET_DV), F32)]),
        out_shape=jax.ShapeDtypeStruct((2, b, t, RET_W), F32),
        compiler_params=pltpu.CompilerParams(
            dimension_semantics=("parallel", "parallel", "arbitrary"), vmem_limit_bytes=VMEM_LIMIT),
        name="retention_scan",
    )(log_gamma, q, k, v)


def _mlstm_kernel(q_ref, k_ref, v_ref, gc_ref, gr_ref, o_ref, c_ref, m_ref):
    d = pl.program_id(1)
    c = pl.program_id(2)

    @pl.when(c == 0)
    def _():
        c_ref[...] = jnp.zeros_like(c_ref)
        m_ref[...] = jnp.zeros_like(m_ref)

    L = CHUNK
    ti = lax.broadcasted_iota(jnp.int32, (L, L), 0)
    si = lax.broadcasted_iota(jnp.int32, (L, L), 1)
    dist = jnp.where(d == 0, ti - si, si - ti)
    le = dist >= 0
    le_t = dist <= 0
    lane = lax.broadcasted_iota(jnp.int32, (L, ML_DV), 1)
    ones_col = jnp.where(lane == 0, 1.0, 0.0).astype(BF16)
    H = ML_HEADS
    chains = [(bi, h) for bi in range(SCAN_BATCH) for h in range(H)]
    each = lambda fn, *lists: [fn(*xs) for xs in zip(*lists)]
    gc = [gc_ref[0, bi] for bi in range(SCAN_BATCH)]
    gr = [gr_ref[0, bi, 0] for bi in range(SCAN_BATCH)]
    i_col = [gc[bi][:, h:h + 1] for bi, h in chains]
    f_col = [gc[bi][:, H + h:H + h + 1] for bi, h in chains]
    i_row = [gr[bi][h:h + 1, :] for bi, h in chains]
    f_row = [gr[bi][H + h:H + h + 1, :] for bi, h in chains]
    b_col = each(lambda f_: jnp.sum(jnp.where(le, f_, 0.0), axis=1, keepdims=True), f_row)
    b_row = each(lambda f_: jnp.sum(jnp.where(le_t, f_, 0.0), axis=0, keepdims=True), f_col)
    b_last = each(lambda f_: jnp.sum(f_, axis=1, keepdims=True), f_row)
    m = [m_ref[bi * H + h:bi * H + h + 1, 0:1] for bi, h in chains]
    log_d = each(lambda bc, br, ir: jnp.where(le, bc - br + ir, -jnp.inf), b_col, b_row, i_row)
    inter = each(lambda bc, m_: bc + m_, b_col, m)
    m_row = each(lambda in_, ld: jnp.maximum(in_, jnp.max(ld, axis=1, keepdims=True)), inter, log_d)
    qh = [q_ref[bi, :, h * ML_DK:(h + 1) * ML_DK].astype(BF16) for bi, h in chains]
    kh = [k_ref[bi, :, h * ML_DK:(h + 1) * ML_DK] for bi, h in chains]
    v_aug = [jnp.concatenate([v_ref[bi, :, h * ML_DV:(h + 1) * ML_DV].astype(BF16), ones_col], axis=1)
             for bi, h in chains]
    s = each(lambda q_, k_, ld, mr: _dot_nt(q_, k_.astype(BF16)) * jnp.exp(ld - mr), qh, kh, log_d, m_row)
    c_aug = [c_ref[bi, h] for bi, h in chains]
    num = each(lambda in_, mr, q_, c_, s_, v_: jnp.exp(in_ - mr) * _dot(q_, c_.astype(BF16))
               + _dot(s_.astype(BF16), v_), inter, m_row, qh, c_aug, s, v_aug)
    hh = each(lambda n_, mr: n_[:, :ML_DV] / jnp.maximum(jnp.abs(n_[:, ML_DV:ML_DV + 1]), jnp.exp(-mr)), num, m_row)
    for (bi, h), h_ in zip(chains, hh):
        o_ref[0, bi, :, h * ML_DV:(h + 1) * ML_DV] = h_
    m_new = each(lambda bl, m_, br, ir: jnp.maximum(bl + m_, jnp.max(bl - br + ir, axis=1, keepdims=True)),
                 b_last, m, b_row, i_row)
    new_c = each(lambda bl, m_, mn, c_, k_, bc, ic, v_: jnp.exp(bl + m_ - mn) * c_ + _dot_tn(
        (k_ * jnp.exp(bl - bc + ic - mn)).astype(BF16), v_), b_last, m, m_new, c_aug, kh, b_col, i_col, v_aug)
    for (bi, h), c_, mn in zip(chains, new_c, m_new):
        c_ref[bi, h] = c_
        m_ref[bi * H + h:bi * H + h + 1, :] = jnp.broadcast_to(mn, (1, 128))


def mlstm_scan(q, k, v, g_col, g_row, n_ctx_chunks):
    b, t, _ = q.shape
    n_chunks = t // CHUNK
    idx = functools.partial(_chunk_index, n_ctx_chunks=n_ctx_chunks, n_chunks=n_chunks)
    in_map = lambda bi, d, c: (bi, idx(d, c), 0)
    return pl.pallas_call(
        _mlstm_kernel,
        grid=(b // SCAN_BATCH, 2, n_chunks),
        in_specs=[pl.BlockSpec((SCAN_BATCH, CHUNK, ML_QK), in_map),
                  pl.BlockSpec((SCAN_BATCH, CHUNK, ML_QK), in_map),
                  pl.BlockSpec((SCAN_BATCH, CHUNK, ML_W), in_map),
                  pl.BlockSpec((1, SCAN_BATCH, CHUNK, 2 * ML_HEADS), lambda bi, d, c: (d, bi, idx(d, c), 0)),
                  pl.BlockSpec((1, SCAN_BATCH, 1, 2 * ML_HEADS, CHUNK), lambda bi, d, c: (d, bi, idx(d, c), 0, 0))],
        out_specs=pl.BlockSpec((1, SCAN_BATCH, CHUNK, ML_W), lambda bi, d, c: (d, bi, idx(d, c), 0)),
        out_shape=jax.ShapeDtypeStruct((2, b, t, ML_W), F32),
        scratch_shapes=[pltpu.VMEM((SCAN_BATCH, ML_HEADS, ML_DK, 2 * ML_DV), F32),
                        pltpu.VMEM((SCAN_BATCH * ML_HEADS, 128), F32)],
        compiler_params=pltpu.CompilerParams(
            dimension_semantics=("parallel", "parallel", "arbitrary"), vmem_limit_bytes=VMEM_LIMIT),
        name="mlstm_scan",
    )(q, k, v, g_col, g_row)


def _hgrn_masks():
    L = CHUNK
    stack = np.zeros((2, (1 + 2 * HG_LEVELS) * L, L), np.float32)
    pair = np.zeros((2, HG_LEVELS + 1, L, L), np.float32)
    for d in range(2):
        p = np.arange(L) if d == 0 else L - 1 - np.arange(L)
        pt, ps = p[:, None], p[None, :]
        stack[d, :L] = ps <= pt
        for l in range(HG_LEVELS):
            parent, half = p >> (l + 1), (p >> l) & 1
            split = parent * (2 << l) + (1 << l) - 1
            same = parent[:, None] == parent[None, :]
            e = same & (half[:, None] == 1) & (ps > split[:, None]) & (ps <= pt)
            f = same & (half[:, None] == 0) & (ps > pt) & (ps <= split[:, None])
            stack[d, (1 + 2 * l) * L:(2 + 2 * l) * L] = e
            stack[d, (2 + 2 * l) * L:(3 + 2 * l) * L] = f
            pair[d, l] = same & (half[:, None] == 1) & (half[None, :] == 0)
        pair[d, HG_LEVELS] = np.eye(L)
    return stack, pair


def _hgrn_kernel(q_ref, hf_ref, v_ref, lb_ref, stack_ref, pair_ref, o_ref, s_ref):
    c = pl.program_id(2)

    @pl.when(c == 0)
    def _():
        s_ref[...] = jnp.zeros_like(s_ref)

    L = CHUNK
    lb = lb_ref[...]
    f = lb + (1.0 - lb) * jax.nn.sigmoid(jnp.concatenate([hf_ref[bi] for bi in range(SCAN_BATCH)], axis=1))
    kk = 1.0 - f
    g = jnp.log(f)
    sums = _mask_dot(stack_ref[0], g)
    chains = [(bi, h) for bi in range(SCAN_BATCH) for h in range(HG_HEADS)]
    each = lambda fn, *lists: [fn(*xs) for xs in zip(*lists)]
    hs = [slice(bi * HG_K + h * HG_DK, bi * HG_K + (h + 1) * HG_DK) for bi, h in chains]
    qh = [q_ref[bi, :, h * HG_DK:(h + 1) * HG_DK] for bi, h in chains]
    kh = [kk[:, s_] for s_ in hs]
    vh = [v_ref[bi, :, h * HG_DV:(h + 1) * HG_DV].astype(BF16) for bi, h in chains]
    b = [sums[0:L, s_] for s_ in hs]
    att = each(lambda q_, k_: pair_ref[0, HG_LEVELS] * _dot_nt(q_.astype(BF16), k_.astype(BF16)), qh, kh)
    for l in range(HG_LEVELS):
        e = [sums[(1 + 2 * l) * L:(2 + 2 * l) * L, s_] for s_ in hs]
        fl = [sums[(2 + 2 * l) * L:(3 + 2 * l) * L, s_] for s_ in hs]
        att = each(lambda a_, q_, k_, e_, f_: a_ + pair_ref[0, l] * _dot_nt(
            (q_ * jnp.exp(e_)).astype(BF16), (k_ * jnp.exp(f_)).astype(BF16)), att, qh, kh, e, fl)
    b_last = [jnp.sum(g[:, s_], axis=0, keepdims=True) for s_ in hs]
    st = [s_ref[bi, h] for bi, h in chains]
    o = each(lambda q_, b_, s_, a_, v_: _dot_nt((q_ * jnp.exp(b_)).astype(BF16), s_.astype(BF16))
             + _dot(a_.astype(BF16), v_), qh, b, st, att, vh)
    for (bi, h), o_ in zip(chains, o):
        o_ref[0, bi, :, h * HG_DV:(h + 1) * HG_DV] = o_
    new_st = each(lambda s_, bl, v_, k_, b_: s_ * jnp.exp(bl) + _dot_tn(v_, (k_ * jnp.exp(bl - b_)).astype(BF16)),
                  st, b_last, vh, kh, b)
    for (bi, h), s_ in zip(chains, new_st):
        s_ref[bi, h] = s_


def hgrn_scan(q, hf, v, lb, n_ctx_chunks):
    b, t, _ = q.shape
    n_chunks = t // CHUNK
    stack, pair = _hgrn_masks()
    idx = functools.partial(_chunk_index, n_ctx_chunks=n_ctx_chunks, n_chunks=n_chunks)
    in_map = lambda bi, d, c: (bi, idx(d, c), 0)
    return pl.pallas_call(
        _hgrn_kernel,
        grid=(b // SCAN_BATCH, 2, n_chunks),
        in_specs=[pl.BlockSpec((SCAN_BATCH, CHUNK, HG_K), in_map),
                  pl.BlockSpec((SCAN_BATCH, CHUNK, HG_K), lambda bi, d, c: (bi, idx(d, c), d)),
                  pl.BlockSpec((SCAN_BATCH, CHUNK, HG_W), in_map),
                  pl.BlockSpec((1, SCAN_BATCH * HG_K), lambda bi, d, c: (0, 0)),
                  pl.BlockSpec((1,) + stack.shape[1:], lambda bi, d, c: (d, 0, 0)),
                  pl.BlockSpec((1,) + pair.shape[1:], lambda bi, d, c: (d, 0, 0, 0))],
        out_specs=pl.BlockSpec((1, SCAN_BATCH, CHUNK, HG_W), lambda bi, d, c: (d, bi, idx(d, c), 0)),
        out_shape=jax.ShapeDtypeStruct((2, b, t, HG_W), F32),
        scratch_shapes=[pltpu.VMEM((SCAN_BATCH, HG_HEADS, HG_DV, HG_DK), F32)],
        compiler_params=pltpu.CompilerParams(
            dimension_semantics=("parallel", "parallel", "arbitrary"), vmem_limit_bytes=VMEM_LIMIT),
        name="hgrn_scan",
    )(q, hf, v, jnp.tile(lb, (1, SCAN_BATCH)), jnp.asarray(stack, BF16), jnp.asarray(pair, F32))


def _rwkv_masks():
    L, G = CHUNK, RW_GROUP
    n = G * L
    head = np.arange(n) // L
    bd = (head[:, None] == head[None, :]).astype(np.float32)
    cum = np.zeros((2, L, L), np.float32)
    n_lvl = int(np.log2(L // RW_BASE))
    wide = np.zeros((2, 4 + n_lvl, L, n), np.float32)
    for d in range(2):
        p = np.arange(L) if d == 0 else L - 1 - np.arange(L)
        pt, ps = p[:, None], p[None, :]
        cum[d] = ps <= pt
        m = [ps < pt, ps <= pt, np.eye(L, dtype=bool), (pt // RW_BASE) == (ps // RW_BASE)]
        for i in range(n_lvl):
            blk = RW_BASE << i
            parent, half = p // (2 * blk), (p // blk) % 2
            m.append((parent[:, None] == parent[None, :]) & (half[:, None] == 1) & (half[None, :] == 0))
        for i, mi in enumerate(m):
            wide[d, i] = np.tile(mi, (1, G))
    return bd, cum, wide


def _unit_lower_inverse(n_w, wide_ref, expand):
    mul = lambda a_w, b_w: [_dot(x.astype(BF16), expand(y)) for x, y in zip(a_w, b_w)]
    nd = [n * wide_ref[0, 3] for n in n_w]
    t_inv = [wide_ref[0, 2] - x for x in nd]
    pw = nd
    for _ in range(int(np.log2(RW_BASE)) - 1):
        pw = mul(pw, pw)
        t_inv = [t + x for t, x in zip(t_inv, mul(t_inv, pw))]
    for i in range(wide_ref.shape[1] - 4):
        tc = mul(t_inv, [n * wide_ref[0, 4 + i] for n in n_w])
        t_inv = [t - x for t, x in zip(t_inv, mul(tc, t_inv))]
    return t_inv


def _rwkv_kernel(r_ref, k_ref, v_ref, kk_ref, a_ref, lw_ref, bd_ref, cum_ref, wide_ref, o_ref, s_ref):
    c = pl.program_id(2)

    @pl.when(c == 0)
    def _():
        s_ref[...] = jnp.zeros_like(s_ref)

    G = RW_GROUP
    bd = bd_ref[...]
    strict = wide_ref[0, 0]
    incl = wide_ref[0, 1]
    cum = cum_ref[0]

    def expand(x):
        return (jnp.concatenate([x] * G, axis=0) * bd).astype(BF16)

    L = CHUNK
    chains = [(bi, g) for bi in range(SCAN_BATCH) for g in range(RW_HEADS // G)]
    cols = [slice(g * RW_GW, (g + 1) * RW_GW) for _, g in chains]
    each = lambda fn, *lists: [fn(*xs) for xs in zip(*lists)]
    r = [r_ref[bi, :, cs_] for (bi, _), cs_ in zip(chains, cols)]
    k = [k_ref[0, bi, :, cs_] for (bi, _), cs_ in zip(chains, cols)]
    v = [v_ref[bi, :, cs_] for (bi, _), cs_ in zip(chains, cols)]
    kk = [kk_ref[bi, :, cs_] for (bi, _), cs_ in zip(chains, cols)]
    lw = [lw_ref[0, bi, :, cs_] for (bi, _), cs_ in zip(chains, cols)]
    kb = [x * a_ref[0, bi, :, cs_] for x, (bi, _), cs_ in zip(kk, chains, cols)]
    cs = each(lambda x: _mask_dot(cum, x), lw)
    c_last = each(lambda x: jnp.sum(x, axis=0, keepdims=True), lw)
    p_inv = each(lambda x: jnp.exp(-x), cs)
    p_to_end = each(lambda cl, x: jnp.exp(cl - x), c_last, cs)
    kkd_rp = each(lambda kk_, r_, cs_, lw_: jnp.concatenate(
        [kk_ * jnp.exp(cs_ - lw_), r_ * jnp.exp(cs_)], axis=0).astype(BF16), kk, r, cs, lw)
    v_bd = each(expand, v)
    sc_k = each(lambda q_, k_, p_: _dot_nt(q_, expand(k_ * p_)), kkd_rp, k, p_inv)
    sc_b = each(lambda q_, b_, p_: _dot_nt(q_, expand(b_ * p_)), kkd_rp, kb, p_inv)
    n_w = each(lambda x: strict * x[:L], sc_b)
    m_w = each(lambda x: jnp.concatenate([strict * x[:L], incl * x[L:]], axis=0).astype(BF16), sc_k)
    mrb_w = each(lambda x: (incl * x[L:]).astype(BF16), sc_b)
    st = [s_ref[bi, g] for bi, g in chains]
    from_state = each(lambda q_, s_, m_, vb_: _dot_nt(q_, s_.astype(BF16)) + _dot(m_, vb_),
                      kkd_rp, st, m_w, v_bd)
    t_inv = _unit_lower_inverse(n_w, wide_ref, expand)
    u = each(lambda t_, f_: _dot(t_.astype(BF16), expand(f_[:L])), t_inv, from_state)
    y = each(lambda f_, m_, u_: f_[L:] - _dot(m_, expand(u_)), from_state, mrb_w, u)
    for (bi, g), cs_, y_ in zip(chains, cols, y):
        o_ref[0, bi, :, cs_] = y_
    new_st = each(lambda s_, cl, v_, u_, k_, b_, p_: s_ * jnp.exp(cl) + bd * _dot_tn(
        jnp.concatenate([v_, -u_], axis=0).astype(BF16),
        jnp.concatenate([k_ * p_, b_ * p_], axis=0).astype(BF16)), st, c_last, v, u, k, kb, p_to_end)
    for (bi, g), s_ in zip(chains, new_st):
        s_ref[bi, g] = s_


def rwkv_scan(r, k_eff, v, kk, a, lw, n_ctx_chunks):
    b, t, _ = r.shape
    n_chunks = t // CHUNK
    assert b % SCAN_BATCH == 0
    bd, cum, wide = _rwkv_masks()
    idx = functools.partial(_chunk_index, n_ctx_chunks=n_ctx_chunks, n_chunks=n_chunks)
    shared = pl.BlockSpec((SCAN_BATCH, CHUNK, RW_W), lambda bi, d, c: (bi, idx(d, c), 0))
    per_dir = pl.BlockSpec((1, SCAN_BATCH, CHUNK, RW_W), lambda bi, d, c: (d, bi, idx(d, c), 0))
    n = RW_GROUP * CHUNK
    return pl.pallas_call(
        _rwkv_kernel,
        grid=(b // SCAN_BATCH, 2, n_chunks),
        in_specs=[shared, per_dir, shared, shared, per_dir, per_dir,
                  pl.BlockSpec((n, RW_GW), lambda bi, d, c: (0, 0)),
                  pl.BlockSpec((1, CHUNK, CHUNK), lambda bi, d, c: (d, 0, 0)),
                  pl.BlockSpec((1,) + wide.shape[1:], lambda bi, d, c: (d, 0, 0, 0))],
        out_specs=per_dir,
        out_shape=jax.ShapeDtypeStruct((2, b, t, RW_W), F32),
        scratch_shapes=[pltpu.VMEM((SCAN_BATCH, RW_HEADS // RW_GROUP, RW_GW, RW_GW), F32)],
        compiler_params=pltpu.CompilerParams(
            dimension_semantics=("parallel", "parallel", "arbitrary"), vmem_limit_bytes=VMEM_LIMIT),
        name="rwkv7_scan",
    )(r, k_eff, v, kk, a, lw, jnp.asarray(bd, F32), jnp.asarray(cum, BF16), jnp.asarray(wide, F32))


def _moe_kernel(e_ref, x_ref, wgu_ref, bgu_ref, wdn_ref, bdn_ref, o_ref, wgu16_ref, wdn16_ref):
    i = pl.program_id(0)
    new_expert = jnp.logical_or(i == 0, e_ref[i] != e_ref[jnp.maximum(i - 1, 0)])

    @pl.when(new_expert)
    def _():
        wgu16_ref[...] = wgu_ref[0, 0].astype(BF16)
        wdn16_ref[...] = wdn_ref[0, 0].astype(BF16)

    d_ff = wdn16_ref.shape[0]
    gu = _dot(x_ref[...].astype(BF16), wgu16_ref[...]) + bgu_ref[0, 0]
    glu = jnp.minimum(gu[:, :d_ff], SWIGLU_LIMIT)
    lin = jnp.clip(gu[:, d_ff:], -SWIGLU_LIMIT, SWIGLU_LIMIT)
    act = glu * jax.nn.sigmoid(SWIGLU_ALPHA * glu) * (lin + 1.0)
    o_ref[...] = _dot(act.astype(BF16), wdn16_ref[...]) + bdn_ref[0, 0]


def moe_experts(xb, blk_e, layer, w_gu, b_gu, w_dn, b_dn):
    n_rows, dm = xb.shape
    d_ff = w_dn.shape[2]
    n_blocks = n_rows // MOE_ROWS
    return pl.pallas_call(
        _moe_kernel,
        grid_spec=pltpu.PrefetchScalarGridSpec(
            num_scalar_prefetch=1,
            grid=(n_blocks,),
            in_specs=[pl.BlockSpec((MOE_ROWS, dm), lambda i, e: (i, 0)),
                      pl.BlockSpec((1, 1, dm, 2 * d_ff), lambda i, e: (layer, e[i], 0, 0)),
                      pl.BlockSpec((1, 1, 1, 2 * d_ff), lambda i, e: (layer, e[i], 0, 0)),
                      pl.BlockSpec((1, 1, d_ff, dm), lambda i, e: (layer, e[i], 0, 0)),
                      pl.BlockSpec((1, 1, 1, dm), lambda i, e: (layer, e[i], 0, 0))],
            out_specs=pl.BlockSpec((MOE_ROWS, dm), lambda i, e: (i, 0)),
            scratch_shapes=[pltpu.VMEM((dm, 2 * d_ff), BF16), pltpu.VMEM((d_ff, dm), BF16)]),
        out_shape=jax.ShapeDtypeStruct((n_rows, dm), F32),
        compiler_params=pltpu.CompilerParams(dimension_semantics=("arbitrary",),
                                             vmem_limit_bytes=VMEM_LIMIT),
        name="moe_experts",
    )(blk_e, xb, w_gu, b_gu, w_dn, b_dn)


def moe_ffn(t, router_w, router_b, layer, w_gu, b_gu, w_dn, b_dn):
    n, dm = t.shape
    logits = jnp.dot(t, router_w, precision=lax.Precision.HIGHEST) + router_b
    top_logit, top_e = lax.top_k(logits, TOP_K)
    top_w = jax.nn.softmax(top_logit, axis=-1)
    flat_e = top_e.reshape(-1).astype(jnp.int32)
    n_asg = n * TOP_K
    seg = 128
    assert n_asg % seg == 0
    onehot = (flat_e.reshape(n_asg // seg, seg, 1) == jnp.arange(N_EXPERTS, dtype=jnp.int32)).astype(F32)
    within = jnp.einsum('ts,bse->bte', jnp.tril(jnp.ones((seg, seg), F32)), onehot)
    seg_tot = within[:, -1, :]
    before = jnp.cumsum(seg_tot, axis=0) - seg_tot
    rank = (jnp.sum(onehot * (within + before[:, None, :]), axis=-1) - 1.0).astype(jnp.int32).reshape(-1)
    counts = jnp.sum(seg_tot, axis=0).astype(jnp.int32)
    padded = (counts + MOE_ROWS - 1) // MOE_ROWS * MOE_ROWS
    pad_end = jnp.cumsum(padded)
    pad_start = pad_end - padded
    grp_start = jnp.cumsum(counts) - counts
    n_blocks = -(-n_asg // MOE_ROWS) + N_EXPERTS
    n_rows = n_blocks * MOE_ROWS
    blk_e = jnp.minimum(jnp.searchsorted(pad_end, jnp.arange(n_blocks, dtype=jnp.int32) * MOE_ROWS, side='right'),
                        N_EXPERTS - 1).astype(jnp.int32)
    order = jnp.argsort(flat_e)
    e_row = jnp.repeat(blk_e, MOE_ROWS)
    j = jnp.arange(n_rows, dtype=jnp.int32) - pad_start[e_row]
    used = j < counts[e_row]
    row_tok = jnp.where(used, order[jnp.where(used, grp_start[e_row] + j, 0)] // TOP_K, 0)
    xb = t[row_tok]
    yb = moe_experts(xb, blk_e, layer, w_gu, b_gu[:, :, None, :], w_dn, b_dn[:, :, None, :])
    slot = (pad_start[flat_e] + rank).reshape(n, TOP_K)
    y = jnp.zeros((n, dm), F32)
    for j in range(TOP_K):
        y = y + yb[slot[:, j]] * top_w[:, j:j + 1]
    return y


def rms_norm(x, g):
    return x * lax.rsqrt(jnp.mean(x * x, axis=-1, keepdims=True) + EPS) * g


def head_rms_norm(o, g):
    return o * lax.rsqrt(jnp.mean(o * o, axis=-1, keepdims=True) + EPS) * g


def dwconv3(x, w, n_ctx):
    t = x.shape[1]
    xp = jnp.pad(x, ((0, 0), (1, 1), (0, 0)))
    pos = jnp.arange(t)[None, :, None]
    has_prev = (pos != 0) & (pos != n_ctx)
    has_next = (pos != n_ctx - 1) & (pos != t - 1)
    return (w[0] * jnp.where(has_prev, xp[:, :-2], 0.0) + w[1] * x
            + w[2] * jnp.where(has_next, xp[:, 2:], 0.0))


def rope_tables(n_ctx, seq):
    rows = seq // GRID_W
    pos_r = jnp.repeat(jnp.arange(rows, dtype=F32), GRID_W)
    pos_c = jnp.tile(jnp.arange(GRID_W, dtype=F32), rows)
    nf = RET_DK // 4
    inv = ROPE_BASE ** (-jnp.arange(nf, dtype=F32) / nf)
    ang = jnp.concatenate([pos_r[:, None] * inv, pos_c[:, None] * inv], axis=-1)
    cos = jnp.concatenate([jnp.ones((n_ctx, RET_DK // 2), F32), jnp.cos(ang)], axis=0)
    sin = jnp.concatenate([jnp.zeros((n_ctx, RET_DK // 2), F32), jnp.sin(ang)], axis=0)
    return cos, sin


def apply_rope(x, cos, sin):
    half = x.shape[-1] // 2
    x1, x2 = x[..., :half], x[..., half:]
    c, s = cos[None, :, None, :], sin[None, :, None, :]
    return jnp.concatenate([x1 * c - x2 * s, x1 * s + x2 * c], axis=-1)


def project(u, w):
    b, t, dm = u.shape
    m = w.shape[1]
    m_pad = -(-m // 128) * 128
    w16 = jnp.pad(w, ((0, 0), (0, m_pad - m))).astype(BF16)
    z = matmul(u.reshape(b * t, dm).astype(BF16), w16)
    return z[:, :m].reshape(b, t, m)


def mixer_ab(u, n_ctx, rope, w_in, w_out, ret_decay, ret_norm, ml_conv, ml_gate_b, ml_norm):
    b, t, _ = u.shape
    n_ctx_chunks = n_ctx // CHUNK
    rq, rk, rv, rg, mqk, mv, mo, mg = _split(project(u, w_in), AB_SPLITS)
    rq = apply_rope(rq.reshape(b, t, RET_HEADS, RET_DK), *rope).reshape(b, t, RET_QK)
    rk = apply_rope(rk.reshape(b, t, RET_HEADS, RET_DK) * RET_DK ** -0.5, *rope).reshape(b, t, RET_QK)
    log_gamma = jnp.log1p(-jnp.exp(ret_decay))
    ret = retention_scan(rq, rk, rv, log_gamma, n_ctx_chunks)
    ret = ret[0] + ret[1]

    mq, mk = _split(jax.nn.silu(dwconv3(mqk, ml_conv, n_ctx)), (ML_QK, ML_QK))
    mq = mq * ML_DK ** -0.5
    gates = (mg + ml_gate_b).reshape(b, t, 2, 2, ML_HEADS)
    gates = jnp.stack([gates[:, :, :, 0], jax.nn.log_sigmoid(gates[:, :, :, 1])], axis=3)
    g_col = jnp.moveaxis(gates, 2, 0).reshape(2, b, t, 2 * ML_HEADS)
    g_row = jnp.swapaxes(g_col.reshape(2, b, t // CHUNK, CHUNK, 2 * ML_HEADS), 3, 4)
    ml = mlstm_scan(mq, mk, mv, g_col, g_row, n_ctx_chunks)
    ml = ml[0] + ml[1]

    ret = head_rms_norm(ret.reshape(b, t, RET_HEADS, RET_DV), ret_norm).reshape(b, t, RET_W) * jax.nn.silu(rg)
    ml = head_rms_norm(ml.reshape(b, t, ML_HEADS, ML_DV), ml_norm).reshape(b, t, ML_W) * jax.nn.sigmoid(mo)
    return project(jnp.concatenate([ret, ml], axis=-1), w_out)


def mixer_cd(u, n_ctx, w_in, w_out, lb, hg_norm, rw_shift, rw_w0, rw_w2, rw_a0, rw_a2, rw_g2,
             rw_kk_scale, rw_k_a, rw_r_k, rw_norm):
    b, t, _ = u.shape
    n_ctx_chunks = n_ctx // CHUNK
    hq, hf, hi, hg, zr = _split(project(u, w_in), CD_SPLITS)
    hgo = hgrn_scan(hq, hf, hi, lb.reshape(1, HG_K), n_ctx_chunks)
    hgo = hgo[0] + hgo[1]

    r, k, v, wl, al, gl = _split(dwconv3(zr, rw_shift, n_ctx), RWKV_SPLITS)
    kk = (k * rw_kk_scale).reshape(b, t, RW_HEADS, RW_HD)
    kk = (kk * lax.rsqrt(jnp.sum(kk * kk, axis=-1, keepdims=True) + EPS)).reshape(b, t, RW_W)
    gate = jax.nn.sigmoid(gl) @ rw_g2
    k_eff, a_all, lw_all = [], [], []
    for d in range(2):
        wl_d = wl[..., d * RW_W_RANK:(d + 1) * RW_W_RANK]
        al_d = al[..., d * RW_A_RANK:(d + 1) * RW_A_RANK]
        w_log = -jax.nn.softplus(-(rw_w0[d] + jnp.tanh(wl_d) @ rw_w2[d])) - 0.5
        lw_all.append(-jnp.exp(w_log))
        a = jax.nn.sigmoid(rw_a0[d] + al_d @ rw_a2[d])
        a_all.append(a)
        k_eff.append(k * (1.0 + (a - 1.0) * rw_k_a))
    k_eff, a_all, lw_all = jnp.stack(k_eff), jnp.stack(a_all), jnp.stack(lw_all)
    rwo = rwkv_scan(r, k_eff, v, kk, a_all, lw_all, n_ctx_chunks)
    rwo = (rwo[0] + rwo[1]).reshape(b, t, RW_HEADS, RW_HD)

    r4, v4 = r.reshape(b, t, RW_HEADS, RW_HD), v.reshape(b, t, RW_HEADS, RW_HD)
    bonus = sum(jnp.sum(r4 * k_eff[d].reshape(b, t, RW_HEADS, RW_HD) * rw_r_k, axis=-1, keepdims=True) * v4
                for d in range(2))
    hgo = head_rms_norm(hgo.reshape(b, t, HG_HEADS, HG_DV), hg_norm).reshape(b, t, HG_W) * jax.nn.sigmoid(hg)
    rwo = (head_rms_norm(rwo, rw_norm) + bonus).reshape(b, t, RW_W) * gate
    return project(jnp.concatenate([hgo, rwo], axis=-1), w_out)


def hgrn_lower_bound(p, layer):
    sm = jax.nn.softmax(p, axis=0)
    return jnp.cumsum(sm, axis=0)[layer] - sm[0]


def kernel(x, c, ctx, c_ctx, mod_w, mod_b, norm_mix, norm_ffn, norm_final, ab_w_in, ab_w_out, ret_decay, ret_norm, mlstm_conv, mlstm_gate_b, mlstm_norm, cd_w_in, cd_w_out, hgrn_lb, hgrn_norm, rwkv_shift, rwkv_w0, rwkv_w2, rwkv_a0, rwkv_a2, rwkv_g2, rwkv_kk_scale, rwkv_k_a, rwkv_r_k, rwkv_norm, router_w, router_b, exp_w_gate_up, exp_b_gate_up, exp_w_down, exp_b_down):
    bsz, seq, dm = x.shape
    n_ctx = ctx.shape[1]
    depth = mod_w.shape[0]
    t = n_ctx + seq
    rope = rope_tables(n_ctx, seq)
    cond_l = jax.nn.silu(c)
    cond_c = jax.nn.silu(c_ctx)[None, :]
    h = jnp.concatenate([ctx, x], axis=1)
    hp = lax.Precision.HIGHEST

    is_ctx = (jnp.arange(t) < n_ctx)[None, :, None]

    def per_token(m_c, m_l):
        return jnp.where(is_ctx, m_c[:, None, :], m_l[:, None, :])

    for i in range(depth):
        j = i // 2
        m_l = jnp.split(jnp.dot(cond_l, mod_w[i], precision=hp) + mod_b[i], 6, axis=-1)
        m_c = jnp.split(jnp.dot(cond_c, mod_w[i], precision=hp) + mod_b[i], 6, axis=-1)
        m = [per_token(a, b_) for a, b_ in zip(m_c, m_l)]
        u = rms_norm(h, norm_mix[i]) * (1.0 + m[1]) + m[0]
        if i % 2 == 0:
            y = mixer_ab(u, n_ctx, rope, ab_w_in[j], ab_w_out[j], ret_decay[j], ret_norm[j],
                         mlstm_conv[j], mlstm_gate_b[j], mlstm_norm[j])
        else:
            y = mixer_cd(u, n_ctx, cd_w_in[j], cd_w_out[j], hgrn_lower_bound(hgrn_lb, i), hgrn_norm[j],
                         rwkv_shift[j], rwkv_w0[j], rwkv_w2[j], rwkv_a0[j], rwkv_a2[j], rwkv_g2[j],
                         rwkv_kk_scale[j], rwkv_k_a[j], rwkv_r_k[j], rwkv_norm[j])
        h = h + m[2] * y
        vv = rms_norm(h, norm_ffn[i]) * (1.0 + m[4]) + m[3]
        f = moe_ffn(vv.reshape(-1, dm), router_w[i], router_b[i], i, exp_w_gate_up, exp_b_gate_up,
                    exp_w_down, exp_b_down).reshape(vv.shape)
        h = h + m[5] * f
    return rms_norm(h[:, n_ctx:], norm_final)
```

```python
import functools

import numpy as np
import jax
import jax.numpy as jnp
from jax import lax
from jax.experimental import pallas as pl
from jax.experimental.pallas import tpu as pltpu

F32 = jnp.float32
BF16 = jnp.bfloat16

CHUNK = 64
GRID_W = 64
EPS = 1e-6
ROPE_BASE = 10000.0
RET_HEADS, RET_DK, RET_DV = 4, 64, 128
ML_HEADS, ML_DK, ML_DV = 4, 64, 128
HG_HEADS, HG_DK, HG_DV = 4, 128, 128
RW_HEADS, RW_HD = 8, 64
RW_W_RANK, RW_A_RANK, RW_G_RANK = 64, 64, 128
N_EXPERTS, TOP_K = 32, 4
SWIGLU_LIMIT, SWIGLU_ALPHA = 7.0, 1.702

RET_QK, RET_W = RET_HEADS * RET_DK, RET_HEADS * RET_DV
ML_QK, ML_W = ML_HEADS * ML_DK, ML_HEADS * ML_DV
HG_K, HG_W = HG_HEADS * HG_DK, HG_HEADS * HG_DV
RW_W = RW_HEADS * RW_HD
AB_SPLITS = (RET_QK, RET_QK, RET_W, RET_W, 2 * ML_QK, ML_W, ML_W, 4 * ML_HEADS)
RWKV_SPLITS = (RW_W, RW_W, RW_W, 2 * RW_W_RANK, 2 * RW_A_RANK, RW_G_RANK)
RWKV_IN = sum(RWKV_SPLITS)
CD_SPLITS = (HG_K, 2 * HG_K, HG_W, HG_W, RWKV_IN)

RW_GROUP = 4
RW_GW = RW_GROUP * RW_HD
RW_BASE = 8
HG_LEVELS = 6

SCAN_BATCH = 4
MM_ROWS = 256
MOE_ROWS = 512
VMEM_LIMIT = 56 * 1024 * 1024


def _split(z, sizes):
    return jnp.split(z, [int(s) for s in np.cumsum(sizes)[:-1]], axis=-1)


def _dot(a, b):
    return jnp.dot(a, b, preferred_element_type=F32)


def _dot_nt(a, b):
    return lax.dot_general(a, b, (((1,), (1,)), ((), ())), preferred_element_type=F32)


def _dot_tn(a, b):
    return lax.dot_general(a, b, (((0,), (0,)), ((), ())), preferred_element_type=F32)


def _mask_dot(m16, x):
    hi = x.astype(BF16)
    lo = (x - hi.astype(F32)).astype(BF16)
    return _dot(m16, hi) + _dot(m16, lo)


def _chunk_index(d, c, n_ctx_chunks, n_chunks):
    rev = jnp.where(c < n_ctx_chunks, n_ctx_chunks - 1 - c, n_chunks + n_ctx_chunks - 1 - c)
    return jnp.where(d == 0, c, rev)


def _mm_kernel(x_ref, w_ref, o_ref):
    o_ref[...] = _dot(x_ref[...], w_ref[...])


def matmul(x, w):
    n, k = x.shape
    m = w.shape[1]
    assert n % MM_ROWS == 0 and m % 128 == 0
    return pl.pallas_call(
        _mm_kernel,
        grid=(n // MM_ROWS,),
        in_specs=[pl.BlockSpec((MM_ROWS, k), lambda i: (i, 0)),
                  pl.BlockSpec((k, m), lambda i: (0, 0))],
        out_specs=pl.BlockSpec((MM_ROWS, m), lambda i: (i, 0)),
        out_shape=jax.ShapeDtypeStruct((n, m), F32),
        compiler_params=pltpu.CompilerParams(dimension_semantics=("parallel",),
                                             vmem_limit_bytes=VMEM_LIMIT),
        name="dense_proj",
    )(x, w)


def _ret_kernel(lg_ref, q_ref, k_ref, v_ref, o_ref, s_ref):
    d = pl.program_id(1)
    c = pl.program_id(2)

    @pl.when(c == 0)
    def _():
        s_ref[...] = jnp.zeros_like(s_ref)

    L = CHUNK
    ti = lax.broadcasted_iota(jnp.int32, (L, L), 0)
    si = lax.broadcasted_iota(jnp.int32, (L, L), 1)
    dist = jnp.where(d == 0, ti - si, si - ti)
    causal = dist >= 0
    distf = jnp.maximum(dist, 0).astype(F32)
    row = lax.broadcasted_iota(jnp.int32, (L, 1), 0)
    pos = jnp.where(d == 0, row, L - 1 - row).astype(F32)
    lg = [lg_ref[d, h] for h in range(RET_HEADS)]
    dm = [jnp.where(causal, jnp.exp(x * distf), 0.0) for x in lg]
    q_dec = [jnp.exp(x * (pos + 1.0)) for x in lg]
    k_dec = [jnp.exp(x * (L - 1.0 - pos)) for x in lg]
    chains = [(bi, h) for bi in range(SCAN_BATCH) for h in range(RET_HEADS)]
    qh = [q_ref[bi, :, h * RET_DK:(h + 1) * RET_DK] for bi, h in chains]
    kh = [k_ref[bi, :, h * RET_DK:(h + 1) * RET_DK] for bi, h in chains]
    vh = [v_ref[bi, :, h * RET_DV:(h + 1) * RET_DV].astype(BF16) for bi, h in chains]
    att = [_dot_nt(q_.astype(BF16), k_.astype(BF16)) * dm[h] for q_, k_, (_, h) in zip(qh, kh, chains)]
    s = [s_ref[bi, h] for bi, h in chains]
    o = [_dot((q_ * q_dec[h]).astype(BF16), s_.astype(BF16)) + _dot(a_.astype(BF16), v_)
         for q_, s_, a_, v_, (_, h) in zip(qh, s, att, vh, chains)]
    for (bi, h), o_ in zip(chains, o):
        o_ref[0, bi, :, h * RET_DV:(h + 1) * RET_DV] = o_
    new_s = [jnp.exp(lg[h] * L) * s_ + _dot_tn((k_ * k_dec[h]).astype(BF16), v_)
             for s_, k_, v_, (_, h) in zip(s, kh, vh, chains)]
    for (bi, h), s_ in zip(chains, new_s):
        s_ref[bi, h] = s_


def retention_scan(q, k, v, log_gamma, n_ctx_chunks):
    b, t, _ = q.shape
    n_chunks = t // CHUNK
    idx = functools.partial(_chunk_index, n_ctx_chunks=n_ctx_chunks, n_chunks=n_chunks)
    in_map = lambda bi, d, c, lg: (bi, idx(d, c), 0)
    return pl.pallas_call(
        _ret_kernel,
        grid_spec=pltpu.PrefetchScalarGridSpec(
            num_scalar_prefetch=1,
            grid=(b // SCAN_BATCH, 2, n_chunks),
            in_specs=[pl.BlockSpec((SCAN_BATCH, CHUNK, RET_QK), in_map),
                      pl.BlockSpec((SCAN_BATCH, CHUNK, RET_QK), in_map),
                      pl.BlockSpec((SCAN_BATCH, CHUNK, RET_W), in_map)],
            out_specs=pl.BlockSpec((1, SCAN_BATCH, CHUNK, RET_W), lambda bi, d, c, lg: (d, bi, idx(d, c), 0)),
            scratch_shapes=[pltpu.VMEM((SCAN_BATCH, RET_HEADS, RET_DK, RET_DV), F32)]),
        out_shape=jax.ShapeDtypeStruct((2, b, t, RET_W), F32),
        compiler_params=pltpu.CompilerParams(
            dimension_semantics=("parallel", "parallel", "arbitrary"), vmem_limit_bytes=VMEM_LIMIT),
        name="retention_scan",
    )(log_gamma, q, k, v)


def _mlstm_kernel(q_ref, k_ref, v_ref, gc_ref, gr_ref, o_ref, c_ref, m_ref):
    d = pl.program_id(1)
    c = pl.program_id(2)

    @pl.when(c == 0)
    def _():
        c_ref[...] = jnp.zeros_like(c_ref)
        m_ref[...] = jnp.zeros_like(m_ref)

    L = CHUNK
    ti = lax.broadcasted_iota(jnp.int32, (L, L), 0)
    si = lax.broadcasted_iota(jnp.int32, (L, L), 1)
    dist = jnp.where(d == 0, ti - si, si - ti)
    le = dist >= 0
    le_t = dist <= 0
    lane = lax.broadcasted_iota(jnp.int32, (L, ML_DV), 1)
    ones_col = jnp.where(lane == 0, 1.0, 0.0).astype(BF16)
    H = ML_HEADS
    chains = [(bi, h) for bi in range(SCAN_BATCH) for h in range(H)]
    each = lambda fn, *lists: [fn(*xs) for xs in zip(*lists)]
    gc = [gc_ref[0, bi] for bi in range(SCAN_BATCH)]
    gr = [gr_ref[0, bi, 0] for bi in range(SCAN_BATCH)]
    i_col = [gc[bi][:, h:h + 1] for bi, h in chains]
    f_col = [gc[bi][:, H + h:H + h + 1] for bi, h in chains]
    i_row = [gr[bi][h:h + 1, :] for bi, h in chains]
    f_row = [gr[bi][H + h:H + h + 1, :] for bi, h in chains]
    b_col = each(lambda f_: jnp.sum(jnp.where(le, f_, 0.0), axis=1, keepdims=True), f_row)
    b_row = each(lambda f_: jnp.sum(jnp.where(le_t, f_, 0.0), axis=0, keepdims=True), f_col)
    b_last = each(lambda f_: jnp.sum(f_, axis=1, keepdims=True), f_row)
    m = [m_ref[bi * H + h:bi * H + h + 1, 0:1] for bi, h in chains]
    log_d = each(lambda bc, br, ir: jnp.where(le, bc - br + ir, -jnp.inf), b_col, b_row, i_row)
    inter = each(lambda bc, m_: bc + m_, b_col, m)
    m_row = each(lambda in_, ld: jnp.maximum(in_, jnp.max(ld, axis=1, keepdims=True)), inter, log_d)
    qh = [q_ref[bi, :, h * ML_DK:(h + 1) * ML_DK].astype(BF16) for bi, h in chains]
    kh = [k_ref[bi, :, h * ML_DK:(h + 1) * ML_DK] for bi, h in chains]
    v_aug = [jnp.concatenate([v_ref[bi, :, h * ML_DV:(h + 1) * ML_DV].astype(BF16), ones_col], axis=1)
             for bi, h in chains]
    s = each(lambda q_, k_, ld, mr: _dot_nt(q_, k_.astype(BF16)) * jnp.exp(ld - mr), qh, kh, log_d, m_row)
    c_aug = [c_ref[bi, h] for bi, h in chains]
    num = each(lambda in_, mr, q_, c_, s_, v_: jnp.exp(in_ - mr) * _dot(q_, c_.astype(BF16))
               + _dot(s_.astype(BF16), v_), inter, m_row, qh, c_aug, s, v_aug)
    hh = each(lambda n_, mr: n_[:, :ML_DV] / jnp.maximum(jnp.abs(n_[:, ML_DV:ML_DV + 1]), jnp.exp(-mr)), num, m_row)
    for (bi, h), h_ in zip(chains, hh):
        o_ref[0, bi, :, h * ML_DV:(h + 1) * ML_DV] = h_
    m_new = each(lambda bl, m_, br, ir: jnp.maximum(bl + m_, jnp.max(bl - br + ir, axis=1, keepdims=True)),
                 b_last, m, b_row, i_row)
    new_c = each(lambda bl, m_, mn, c_, k_, bc, ic, v_: jnp.exp(bl + m_ - mn) * c_ + _dot_tn(
        (k_ * jnp.exp(bl - bc + ic - mn)).astype(BF16), v_), b_last, m, m_new, c_aug, kh, b_col, i_col, v_aug)
    for (bi, h), c_, mn in zip(chains, new_c, m_new):
        c_ref[bi, h] = c_
        m_ref[bi * H + h:bi * H + h + 1, :] = jnp.broadcast_to(mn, (1, 128))


def mlstm_scan(q, k, v, g_col, g_row, n_ctx_chunks):
    b, t, _ = q.shape
    n_chunks = t // CHUNK
    idx = functools.partial(_chunk_index, n_ctx_chunks=n_ctx_chunks, n_chunks=n_chunks)
    in_map = lambda bi, d, c: (bi, idx(d, c), 0)
    return pl.pallas_call(
        _mlstm_kernel,
        grid=(b // SCAN_BATCH, 2, n_chunks),
        in_specs=[pl.BlockSpec((SCAN_BATCH, CHUNK, ML_QK), in_map),
                  pl.BlockSpec((SCAN_BATCH, CHUNK, ML_QK), in_map),
                  pl.BlockSpec((SCAN_BATCH, CHUNK, ML_W), in_map),
                  pl.BlockSpec((1, SCAN_BATCH, CHUNK, 2 * ML_HEADS), lambda bi, d, c: (d, bi, idx(d, c), 0)),
                  pl.BlockSpec((1, SCAN_BATCH, 1, 2 * ML_HEADS, CHUNK), lambda bi, d, c: (d, bi, idx(d, c), 0, 0))],
        out_specs=pl.BlockSpec((1, SCAN_BATCH, CHUNK, ML_W), lambda bi, d, c: (d, bi, idx(d, c), 0)),
        out_shape=jax.ShapeDtypeStruct((2, b, t, ML_W), F32),
        scratch_shapes=[pltpu.VMEM((SCAN_BATCH, ML_HEADS, ML_DK, 2 * ML_DV), F32),
                        pltpu.VMEM((SCAN_BATCH * ML_HEADS, 128), F32)],
        compiler_params=pltpu.CompilerParams(
            dimension_semantics=("parallel", "parallel", "arbitrary"), vmem_limit_bytes=VMEM_LIMIT),
        name="mlstm_scan",
    )(q, k, v, g_col, g_row)


def _hgrn_masks():
    L = CHUNK
    stack = np.zeros((2, (1 + 2 * HG_LEVELS) * L, L), np.float32)
    pair = np.zeros((2, HG_LEVELS + 1, L, L), np.float32)
    for d in range(2):
        p = np.arange(L) if d == 0 else L - 1 - np.arange(L)
        pt, ps = p[:, None], p[None, :]
        stack[d, :L] = ps <= pt
        for l in range(HG_LEVELS):
            parent, half = p >> (l + 1), (p >> l) & 1
            split = parent * (2 << l) + (1 << l) - 1
            same = parent[:, None] == parent[None, :]
            e = same & (half[:, None] == 1) & (ps > split[:, None]) & (ps <= pt)
            f = same & (half[:, None] == 0) & (ps > pt) & (ps <= split[:, None])
            stack[d, (1 + 2 * l) * L:(2 + 2 * l) * L] = e
            stack[d, (2 + 2 * l) * L:(3 + 2 * l) * L] = f
            pair[d, l] = same & (half[:, None] == 1) & (half[None, :] == 0)
        pair[d, HG_LEVELS] = np.eye(L)
    return stack, pair


def _hgrn_kernel(q_ref, hf_ref, v_ref, lb_ref, stack_ref, pair_ref, o_ref, s_ref):
    c = pl.program_id(2)

    @pl.when(c == 0)
    def _():
        s_ref[...] = jnp.zeros_like(s_ref)

    L = CHUNK
    lb = lb_ref[...]
    f = lb + (1.0 - lb) * jax.nn.sigmoid(jnp.concatenate([hf_ref[bi] for bi in range(SCAN_BATCH)], axis=1))
    kk = 1.0 - f
    g = jnp.log(f)
    sums = _mask_dot(stack_ref[0], g)
    chains = [(bi, h) for bi in range(SCAN_BATCH) for h in range(HG_HEADS)]
    each = lambda fn, *lists: [fn(*xs) for xs in zip(*lists)]
    hs = [slice(bi * HG_K + h * HG_DK, bi * HG_K + (h + 1) * HG_DK) for bi, h in chains]
    qh = [q_ref[bi, :, h * HG_DK:(h + 1) * HG_DK] for bi, h in chains]
    kh = [kk[:, s_] for s_ in hs]
    vh = [v_ref[bi, :, h * HG_DV:(h + 1) * HG_DV].astype(BF16) for bi, h in chains]
    b = [sums[0:L, s_] for s_ in hs]
    att = each(lambda q_, k_: pair_ref[0, HG_LEVELS] * _dot_nt(q_.astype(BF16), k_.astype(BF16)), qh, kh)
    for l in range(HG_LEVELS):
        e = [sums[(1 + 2 * l) * L:(2 + 2 * l) * L, s_] for s_ in hs]
        fl = [sums[(2 + 2 * l) * L:(3 + 2 * l) * L, s_] for s_ in hs]
        att = each(lambda a_, q_, k_, e_, f_: a_ + pair_ref[0, l] * _dot_nt(
            (q_ * jnp.exp(e_)).astype(BF16), (k_ * jnp.exp(f_)).astype(BF16)), att, qh, kh, e, fl)
    b_last = [jnp.sum(g[:, s_], axis=0, keepdims=True) for s_ in hs]
    st = [s_ref[bi, h] for bi, h in chains]
    o = each(lambda q_, b_, s_, a_, v_: _dot_nt((q_ * jnp.exp(b_)).astype(BF16), s_.astype(BF16))
             + _dot(a_.astype(BF16), v_), qh, b, st, att, vh)
    for (bi, h), o_ in zip(chains, o):
        o_ref[0, bi, :, h * HG_DV:(h + 1) * HG_DV] = o_
    new_st = each(lambda s_, bl, v_, k_, b_: s_ * jnp.exp(bl) + _dot_tn(v_, (k_ * jnp.exp(bl - b_)).astype(BF16)),
                  st, b_last, vh, kh, b)
    for (bi, h), s_ in zip(chains, new_st):
        s_ref[bi, h] = s_


def hgrn_scan(q, hf, v, lb, n_ctx_chunks):
    b, t, _ = q.shape
    n_chunks = t // CHUNK
    stack, pair = _hgrn_masks()
    idx = functools.partial(_chunk_index, n_ctx_chunks=n_ctx_chunks, n_chunks=n_chunks)
    in_map = lambda bi, d, c: (bi, idx(d, c), 0)
    return pl.pallas_call(
        _hgrn_kernel,
        grid=(b // SCAN_BATCH, 2, n_chunks),
        in_specs=[pl.BlockSpec((SCAN_BATCH, CHUNK, HG_K), in_map),
                  pl.BlockSpec((SCAN_BATCH, CHUNK, HG_K), lambda bi, d, c: (bi, idx(d, c), d)),
                  pl.BlockSpec((SCAN_BATCH, CHUNK, HG_W), in_map),
                  pl.BlockSpec((1, SCAN_BATCH * HG_K), lambda bi, d, c: (0, 0)),
                  pl.BlockSpec((1,) + stack.shape[1:], lambda bi, d, c: (d, 0, 0)),
                  pl.BlockSpec((1,) + pair.shape[1:], lambda bi, d, c: (d, 0, 0, 0))],
        out_specs=pl.BlockSpec((1, SCAN_BATCH, CHUNK, HG_W), lambda bi, d, c: (d, bi, idx(d, c), 0)),
        out_shape=jax.ShapeDtypeStruct((2, b, t, HG_W), F32),
        scratch_shapes=[pltpu.VMEM((SCAN_BATCH, HG_HEADS, HG_DV, HG_DK), F32)],
        compiler_params=pltpu.CompilerParams(
            dimension_semantics=("parallel", "parallel", "arbitrary"), vmem_limit_bytes=VMEM_LIMIT),
        name="hgrn_scan",
    )(q, hf, v, jnp.tile(lb, (1, SCAN_BATCH)), jnp.asarray(stack, BF16), jnp.asarray(pair, F32))


def _rwkv_masks():
    L, G = CHUNK, RW_GROUP
    n = G * L
    head = np.arange(n) // L
    bd = (head[:, None] == head[None, :]).astype(np.float32)
    cum = np.zeros((2, L, L), np.float32)
    n_lvl = int(np.log2(L // RW_BASE))
    wide = np.zeros((2, 4 + n_lvl, L, n), np.float32)
    for d in range(2):
        p = np.arange(L) if d == 0 else L - 1 - np.arange(L)
        pt, ps = p[:, None], p[None, :]
        cum[d] = ps <= pt
        m = [ps < pt, ps <= pt, np.eye(L, dtype=bool), (pt // RW_BASE) == (ps // RW_BASE)]
        for i in range(n_lvl):
            blk = RW_BASE << i
            parent, half = p // (2 * blk), (p // blk) % 2
            m.append((parent[:, None] == parent[None, :]) & (half[:, None] == 1) & (half[None, :] == 0))
        for i, mi in enumerate(m):
            wide[d, i] = np.tile(mi, (1, G))
    return bd, cum, wide


def _unit_lower_inverse(n_w, wide_ref, expand):
    mul = lambda a_w, b_w: [_dot(x.astype(BF16), expand(y)) for x, y in zip(a_w, b_w)]
    nd = [n * wide_ref[0, 3] for n in n_w]
    t_inv = [wide_ref[0, 2] - x for x in nd]
    pw = nd
    for _ in range(int(np.log2(RW_BASE)) - 1):
        pw = mul(pw, pw)
        t_inv = [t + x for t, x in zip(t_inv, mul(t_inv, pw))]
    for i in range(wide_ref.shape[1] - 4):
        tc = mul(t_inv, [n * wide_ref[0, 4 + i] for n in n_w])
        t_inv = [t - x for t, x in zip(t_inv, mul(tc, t_inv))]
    return t_inv


def _rwkv_kernel(r_ref, k_ref, v_ref, kk_ref, a_ref, lw_ref, bd_ref, cum_ref, wide_ref, o_ref, s_ref):
    c = pl.program_id(2)

    @pl.when(c == 0)
    def _():
        s_ref[...] = jnp.zeros_like(s_ref)

    G = RW_GROUP
    bd = bd_ref[...]
    strict = wide_ref[0, 0]
    incl = wide_ref[0, 1]
    cum = cum_ref[0]

    def expand(x):
        return (jnp.concatenate([x] * G, axis=0) * bd).astype(BF16)

    L = CHUNK
    chains = [(bi, g) for bi in range(SCAN_BATCH) for g in range(RW_HEADS // G)]
    cols = [slice(g * RW_GW, (g + 1) * RW_GW) for _, g in chains]
    each = lambda fn, *lists: [fn(*xs) for xs in zip(*lists)]
    r = [r_ref[bi, :, cs_] for (bi, _), cs_ in zip(chains, cols)]
    k = [k_ref[0, bi, :, cs_] for (bi, _), cs_ in zip(chains, cols)]
    v = [v_ref[bi, :, cs_] for (bi, _), cs_ in zip(chains, cols)]
    kk = [kk_ref[bi, :, cs_] for (bi, _), cs_ in zip(chains, cols)]
    lw = [lw_ref[0, bi, :, cs_] for (bi, _), cs_ in zip(chains, cols)]
    kb = [x * a_ref[0, bi, :, cs_] for x, (bi, _), cs_ in zip(kk, chains, cols)]
    cs = each(lambda x: _mask_dot(cum, x), lw)
    c_last = each(lambda x: jnp.sum(x, axis=0, keepdims=True), lw)
    p_inv = each(lambda x: jnp.exp(-x), cs)
    p_to_end = each(lambda cl, x: jnp.exp(cl - x), c_last, cs)
    kkd_rp = each(lambda kk_, r_, cs_, lw_: jnp.concatenate(
        [kk_ * jnp.exp(cs_ - lw_), r_ * jnp.exp(cs_)], axis=0).astype(BF16), kk, r, cs, lw)
    v_bd = each(expand, v)
    sc_k = each(lambda q_, k_, p_: _dot_nt(q_, expand(k_ * p_)), kkd_rp, k, p_inv)
    sc_b = each(lambda q_, b_, p_: _dot_nt(q_, expand(b_ * p_)), kkd_rp, kb, p_inv)
    n_w = each(lambda x: strict * x[:L], sc_b)
    m_w = each(lambda x: jnp.concatenate([strict * x[:L], incl * x[L:]], axis=0).astype(BF16), sc_k)
    mrb_w = each(lambda x: (incl * x[L:]).astype(BF16), sc_b)
    st = [s_ref[bi, g] for bi, g in chains]
    from_state = each(lambda q_, s_, m_, vb_: _dot_nt(q_, s_.astype(BF16)) + _dot(m_, vb_),
                      kkd_rp, st, m_w, v_bd)
    t_inv = _unit_lower_inverse(n_w, wide_ref, expand)
    u = each(lambda t_, f_: _dot(t_.astype(BF16), expand(f_[:L])), t_inv, from_state)
    y = each(lambda f_, m_, u_: f_[L:] - _dot(m_, expand(u_)), from_state, mrb_w, u)
    for (bi, g), cs_, y_ in zip(chains, cols, y):
        o_ref[0, bi, :, cs_] = y_
    new_st = each(lambda s_, cl, v_, u_, k_, b_, p_: s_ * jnp.exp(cl) + bd * _dot_tn(
        jnp.concatenate([v_, -u_], axis=0).astype(BF16),
        jnp.concatenate([k_ * p_, b_ * p_], axis=0).astype(BF16)), st, c_last, v, u, k, kb, p_to_end)
    for (bi, g), s_ in zip(chains, new_st):
        s_ref[bi, g] = s_


def rwkv_scan(r, k_eff, v, kk, a, lw, n_ctx_chunks):
    b, t, _ = r.shape
    n_chunks = t // CHUNK
    assert b % SCAN_BATCH == 0
    bd, cum, wide = _rwkv_masks()
    idx = functools.partial(_chunk_index, n_ctx_chunks=n_ctx_chunks, n_chunks=n_chunks)
    shared = pl.BlockSpec((SCAN_BATCH, CHUNK, RW_W), lambda bi, d, c: (bi, idx(d, c), 0))
    per_dir = pl.BlockSpec((1, SCAN_BATCH, CHUNK, RW_W), lambda bi, d, c: (d, bi, idx(d, c), 0))
    n = RW_GROUP * CHUNK
    return pl.pallas_call(
        _rwkv_kernel,
        grid=(b // SCAN_BATCH, 2, n_chunks),
        in_specs=[shared, per_dir, shared, shared, per_dir, per_dir,
                  pl.BlockSpec((n, RW_GW), lambda bi, d, c: (0, 0)),
                  pl.BlockSpec((1, CHUNK, CHUNK), lambda bi, d, c: (d, 0, 0)),
                  pl.BlockSpec((1,) + wide.shape[1:], lambda bi, d, c: (d, 0, 0, 0))],
        out_specs=per_dir,
        out_shape=jax.ShapeDtypeStruct((2, b, t, RW_W), F32),
        scratch_shapes=[pltpu.VMEM((SCAN_BATCH, RW_HEADS // RW_GROUP, RW_GW, RW_GW), F32)],
        compiler_params=pltpu.CompilerParams(
            dimension_semantics=("parallel", "parallel", "arbitrary"), vmem_limit_bytes=VMEM_LIMIT),
        name="rwkv7_scan",
    )(r, k_eff, v, kk, a, lw, jnp.asarray(bd, F32), jnp.asarray(cum, BF16), jnp.asarray(wide, F32))


def _moe_kernel(e_ref, x_ref, wgu_ref, bgu_ref, wdn_ref, bdn_ref, o_ref, wgu16_ref, wdn16_ref):
    i = pl.program_id(0)
    new_expert = jnp.logical_or(i == 0, e_ref[i] != e_ref[jnp.maximum(i - 1, 0)])

    @pl.when(new_expert)
    def _():
        wgu16_ref[...] = wgu_ref[0, 0].astype(BF16)
        wdn16_ref[...] = wdn_ref[0, 0].astype(BF16)

    d_ff = wdn16_ref.shape[0]
    gu = _dot(x_ref[...].astype(BF16), wgu16_ref[...]) + bgu_ref[0, 0]
    glu = jnp.minimum(gu[:, :d_ff], SWIGLU_LIMIT)
    lin = jnp.clip(gu[:, d_ff:], -SWIGLU_LIMIT, SWIGLU_LIMIT)
    act = glu * jax.nn.sigmoid(SWIGLU_ALPHA * glu) * (lin + 1.0)
    o_ref[...] = _dot(act.astype(BF16), wdn16_ref[...]) + bdn_ref[0, 0]


def moe_experts(xb, blk_e, layer, w_gu, b_gu, w_dn, b_dn):
    n_rows, dm = xb.shape
    d_ff = w_dn.shape[2]
    n_blocks = n_rows // MOE_ROWS
    return pl.pallas_call(
        _moe_kernel,
        grid_spec=pltpu.PrefetchScalarGridSpec(
            num_scalar_prefetch=1,
            grid=(n_blocks,),
            in_specs=[pl.BlockSpec((MOE_ROWS, dm), lambda i, e: (i, 0)),
                      pl.BlockSpec((1, 1, dm, 2 * d_ff), lambda i, e: (layer, e[i], 0, 0)),
                      pl.BlockSpec((1, 1, 1, 2 * d_ff), lambda i, e: (layer, e[i], 0, 0)),
                      pl.BlockSpec((1, 1, d_ff, dm), lambda i, e: (layer, e[i], 0, 0)),
                      pl.BlockSpec((1, 1, 1, dm), lambda i, e: (layer, e[i], 0, 0))],
            out_specs=pl.BlockSpec((MOE_ROWS, dm), lambda i, e: (i, 0)),
            scratch_shapes=[pltpu.VMEM((dm, 2 * d_ff), BF16), pltpu.VMEM((d_ff, dm), BF16)]),
        out_shape=jax.ShapeDtypeStruct((n_rows, dm), F32),
        compiler_params=pltpu.CompilerParams(dimension_semantics=("arbitrary",),
                                             vmem_limit_bytes=VMEM_LIMIT),
        name="moe_experts",
    )(blk_e, xb, w_gu, b_gu, w_dn, b_dn)


def moe_ffn(t, router_w, router_b, layer, w_gu, b_gu, w_dn, b_dn):
    n, dm = t.shape
    logits = jnp.dot(t, router_w, precision=lax.Precision.HIGHEST) + router_b
    expert_id = lax.broadcasted_iota(jnp.int32, logits.shape, 1)
    rest, top_logit, top_e = logits, [], []
    for _ in range(TOP_K):
        best = jnp.max(rest, axis=-1, keepdims=True)
        pick = jnp.min(jnp.where(rest == best, expert_id, N_EXPERTS), axis=-1, keepdims=True)
        top_logit.append(best)
        top_e.append(pick)
        rest = jnp.where(expert_id == pick, -jnp.inf, rest)
    top_logit, top_e = jnp.concatenate(top_logit, axis=-1), jnp.concatenate(top_e, axis=-1)
    top_w = jax.nn.softmax(top_logit, axis=-1)
    flat_e = top_e.reshape(-1).astype(jnp.int32)
    n_asg = n * TOP_K
    seg = 128
    assert n_asg % seg == 0
    onehot = (flat_e.reshape(n_asg // seg, seg, 1) == jnp.arange(N_EXPERTS, dtype=jnp.int32)).astype(F32)
    within = jnp.einsum('ts,bse->bte', jnp.tril(jnp.ones((seg, seg), F32)), onehot)
    seg_tot = within[:, -1, :]
    before = jnp.cumsum(seg_tot, axis=0) - seg_tot
    rank = (jnp.sum(onehot * (within + before[:, None, :]), axis=-1) - 1.0).astype(jnp.int32).reshape(-1)
    counts = jnp.sum(seg_tot, axis=0).astype(jnp.int32)
    padded = (counts + MOE_ROWS - 1) // MOE_ROWS * MOE_ROWS
    pad_end = jnp.cumsum(padded)
    pad_start = pad_end - padded
    grp_start = jnp.cumsum(counts) - counts
    n_blocks = -(-n_asg // MOE_ROWS) + N_EXPERTS
    n_rows = n_blocks * MOE_ROWS
    blk_start = jnp.arange(n_blocks, dtype=jnp.int32) * MOE_ROWS
    blk_e = jnp.minimum(jnp.sum((pad_end[None, :] <= blk_start[:, None]).astype(jnp.int32), axis=1),
                        N_EXPERTS - 1)
    order = jnp.argsort(flat_e)
    e_row = jnp.repeat(blk_e, MOE_ROWS)
    j = jnp.arange(n_rows, dtype=jnp.int32) - pad_start[e_row]
    used = j < counts[e_row]
    row_tok = jnp.where(used, order[jnp.where(used, grp_start[e_row] + j, 0)] // TOP_K, 0)
    xb = t[row_tok]
    yb = moe_experts(xb, blk_e, layer, w_gu, b_gu[:, :, None, :], w_dn, b_dn[:, :, None, :])
    slot = (pad_start[flat_e] + rank).reshape(n, TOP_K)
    y = jnp.zeros((n, dm), F32)
    for j in range(TOP_K):
        y = y + yb[slot[:, j]] * top_w[:, j:j + 1]
    return y


def rms_norm(x, g):
    return x * lax.rsqrt(jnp.mean(x * x, axis=-1, keepdims=True) + EPS) * g


def head_rms_norm(o, g):
    return o * lax.rsqrt(jnp.mean(o * o, axis=-1, keepdims=True) + EPS) * g


def dwconv3(x, w, n_ctx):
    t = x.shape[1]
    xp = jnp.pad(x, ((0, 0), (1, 1), (0, 0)))
    pos = jnp.arange(t)[None, :, None]
    has_prev = (pos != 0) & (pos != n_ctx)
    has_next = (pos != n_ctx - 1) & (pos != t - 1)
    return (w[0] * jnp.where(has_prev, xp[:, :-2], 0.0) + w[1] * x
            + w[2] * jnp.where(has_next, xp[:, 2:], 0.0))


def rope_tables(n_ctx, seq):
    rows = seq // GRID_W
    pos_r = jnp.repeat(jnp.arange(rows, dtype=F32), GRID_W)
    pos_c = jnp.tile(jnp.arange(GRID_W, dtype=F32), rows)
    nf = RET_DK // 4
    inv = ROPE_BASE ** (-jnp.arange(nf, dtype=F32) / nf)
    ang = jnp.concatenate([pos_r[:, None] * inv, pos_c[:, None] * inv], axis=-1)
    cos = jnp.concatenate([jnp.ones((n_ctx, RET_DK // 2), F32), jnp.cos(ang)], axis=0)
    sin = jnp.concatenate([jnp.zeros((n_ctx, RET_DK // 2), F32), jnp.sin(ang)], axis=0)
    return cos, sin


def apply_rope(x, cos, sin):
    half = x.shape[-1] // 2
    x1, x2 = x[..., :half], x[..., half:]
    c, s = cos[None, :, None, :], sin[None, :, None, :]
    return jnp.concatenate([x1 * c - x2 * s, x1 * s + x2 * c], axis=-1)


def project(u, w):
    b, t, dm = u.shape
    m = w.shape[1]
    m_pad = -(-m // 128) * 128
    w16 = jnp.pad(w, ((0, 0), (0, m_pad - m))).astype(BF16)
    z = matmul(u.reshape(b * t, dm).astype(BF16), w16)
    return z[:, :m].reshape(b, t, m)


def mixer_ab(u, n_ctx, rope, w_in, w_out, ret_decay, ret_norm, ml_conv, ml_gate_b, ml_norm):
    b, t, _ = u.shape
    n_ctx_chunks = n_ctx // CHUNK
    rq, rk, rv, rg, mqk, mv, mo, mg = _split(project(u, w_in), AB_SPLITS)
    rq = apply_rope(rq.reshape(b, t, RET_HEADS, RET_DK), *rope).reshape(b, t, RET_QK)
    rk = apply_rope(rk.reshape(b, t, RET_HEADS, RET_DK) * RET_DK ** -0.5, *rope).reshape(b, t, RET_QK)
    log_gamma = jnp.log1p(-jnp.exp(ret_decay))
    ret = retention_scan(rq, rk, rv, log_gamma, n_ctx_chunks)
    ret = ret[0] + ret[1]

    mq, mk = _split(jax.nn.silu(dwconv3(mqk, ml_conv, n_ctx)), (ML_QK, ML_QK))
    mq = mq * ML_DK ** -0.5
    gates = (mg + ml_gate_b).reshape(b, t, 2, 2, ML_HEADS)
    gates = jnp.stack([gates[:, :, :, 0], jax.nn.log_sigmoid(gates[:, :, :, 1])], axis=3)
    g_col = jnp.moveaxis(gates, 2, 0).reshape(2, b, t, 2 * ML_HEADS)
    g_row = jnp.swapaxes(g_col.reshape(2, b, t // CHUNK, CHUNK, 2 * ML_HEADS), 3, 4)
    ml = mlstm_scan(mq, mk, mv, g_col, g_row, n_ctx_chunks)
    ml = ml[0] + ml[1]

    ret = head_rms_norm(ret.reshape(b, t, RET_HEADS, RET_DV), ret_norm).reshape(b, t, RET_W) * jax.nn.silu(rg)
    ml = head_rms_norm(ml.reshape(b, t, ML_HEADS, ML_DV), ml_norm).reshape(b, t, ML_W) * jax.nn.sigmoid(mo)
    return project(jnp.concatenate([ret, ml], axis=-1), w_out)


def mixer_cd(u, n_ctx, w_in, w_out, lb, hg_norm, rw_shift, rw_w0, rw_w2, rw_a0, rw_a2, rw_g2,
             rw_kk_scale, rw_k_a, rw_r_k, rw_norm):
    b, t, _ = u.shape
    n_ctx_chunks = n_ctx // CHUNK
    hq, hf, hi, hg, zr = _split(project(u, w_in), CD_SPLITS)
    hgo = hgrn_scan(hq, hf, hi, lb.reshape(1, HG_K), n_ctx_chunks)
    hgo = hgo[0] + hgo[1]

    r, k, v, wl, al, gl = _split(dwconv3(zr, rw_shift, n_ctx), RWKV_SPLITS)
    kk = (k * rw_kk_scale).reshape(b, t, RW_HEADS, RW_HD)
    kk = (kk * lax.rsqrt(jnp.sum(kk * kk, axis=-1, keepdims=True) + EPS)).reshape(b, t, RW_W)
    gate = jax.nn.sigmoid(gl) @ rw_g2
    k_eff, a_all, lw_all = [], [], []
    for d in range(2):
        wl_d = wl[..., d * RW_W_RANK:(d + 1) * RW_W_RANK]
        al_d = al[..., d * RW_A_RANK:(d + 1) * RW_A_RANK]
        w_log = -jax.nn.softplus(-(rw_w0[d] + jnp.tanh(wl_d) @ rw_w2[d])) - 0.5
        lw_all.append(-jnp.exp(w_log))
        a = jax.nn.sigmoid(rw_a0[d] + al_d @ rw_a2[d])
        a_all.append(a)
        k_eff.append(k * (1.0 + (a - 1.0) * rw_k_a))
    k_eff, a_all, lw_all = jnp.stack(k_eff), jnp.stack(a_all), jnp.stack(lw_all)
    rwo = rwkv_scan(r, k_eff, v, kk, a_all, lw_all, n_ctx_chunks)
    rwo = (rwo[0] + rwo[1]).reshape(b, t, RW_HEADS, RW_HD)

    r4, v4 = r.reshape(b, t, RW_HEADS, RW_HD), v.reshape(b, t, RW_HEADS, RW_HD)
    bonus = sum(jnp.sum(r4 * k_eff[d].reshape(b, t, RW_HEADS, RW_HD) * rw_r_k, axis=-1, keepdims=True) * v4
                for d in range(2))
    hgo = head_rms_norm(hgo.reshape(b, t, HG_HEADS, HG_DV), hg_norm).reshape(b, t, HG_W) * jax.nn.sigmoid(hg)
    rwo = (head_rms_norm(rwo, rw_norm) + bonus).reshape(b, t, RW_W) * gate
    return project(jnp.concatenate([hgo, rwo], axis=-1), w_out)


def hgrn_lower_bound(p, layer):
    sm = jax.nn.softmax(p, axis=0)
    return jnp.cumsum(sm, axis=0)[layer] - sm[0]


def kernel(x, c, ctx, c_ctx, mod_w, mod_b, norm_mix, norm_ffn, norm_final, ab_w_in, ab_w_out, ret_decay, ret_norm, mlstm_conv, mlstm_gate_b, mlstm_norm, cd_w_in, cd_w_out, hgrn_lb, hgrn_norm, rwkv_shift, rwkv_w0, rwkv_w2, rwkv_a0, rwkv_a2, rwkv_g2, rwkv_kk_scale, rwkv_k_a, rwkv_r_k, rwkv_norm, router_w, router_b, exp_w_gate_up, exp_b_gate_up, exp_w_down, exp_b_down):
    bsz, seq, dm = x.shape
    n_ctx = ctx.shape[1]
    depth = mod_w.shape[0]
    t = n_ctx + seq
    rope = rope_tables(n_ctx, seq)
    cond_l = jax.nn.silu(c)
    cond_c = jax.nn.silu(c_ctx)[None, :]
    h = jnp.concatenate([ctx, x], axis=1)
    hp = lax.Precision.HIGHEST

    is_ctx = (jnp.arange(t) < n_ctx)[None, :, None]

    def per_token(m_c, m_l):
        return jnp.where(is_ctx, m_c[:, None, :], m_l[:, None, :])

    for i in range(depth):
        j = i // 2
        m_l = jnp.split(jnp.dot(cond_l, mod_w[i], precision=hp) + mod_b[i], 6, axis=-1)
        m_c = jnp.split(jnp.dot(cond_c, mod_w[i], precision=hp) + mod_b[i], 6, axis=-1)
        m = [per_token(a, b_) for a, b_ in zip(m_c, m_l)]
        u = rms_norm(h, norm_mix[i]) * (1.0 + m[1]) + m[0]
        if i % 2 == 0:
            y = mixer_ab(u, n_ctx, rope, ab_w_in[j], ab_w_out[j], ret_decay[j], ret_norm[j],
                         mlstm_conv[j], mlstm_gate_b[j], mlstm_norm[j])
        else:
            y = mixer_cd(u, n_ctx, cd_w_in[j], cd_w_out[j], hgrn_lower_bound(hgrn_lb, i), hgrn_norm[j],
                         rwkv_shift[j], rwkv_w0[j], rwkv_w2[j], rwkv_a0[j], rwkv_a2[j], rwkv_g2[j],
                         rwkv_kk_scale[j], rwkv_k_a[j], rwkv_r_k[j], rwkv_norm[j])
        h = h + m[2] * y
        vv = rms_norm(h, norm_ffn[i]) * (1.0 + m[4]) + m[3]
        f = moe_ffn(vv.reshape(-1, dm), router_w[i], router_b[i], i, exp_w_gate_up, exp_b_gate_up,
                    exp_w_down, exp_b_down).reshape(vv.shape)
        h = h + m[5] * f
    return rms_norm(h[:, n_ctx:], norm_final)
```

```python
import functools

import numpy as np
import jax
import jax.numpy as jnp
from jax import lax
from jax.experimental import pallas as pl
from jax.experimental.pallas import tpu as pltpu

F32 = jnp.float32
BF16 = jnp.bfloat16

CHUNK = 64
GRID_W = 64
EPS = 1e-6
ROPE_BASE = 10000.0
RET_HEADS, RET_DK, RET_DV = 4, 64, 128
ML_HEADS, ML_DK, ML_DV = 4, 64, 128
HG_HEADS, HG_DK, HG_DV = 4, 128, 128
RW_HEADS, RW_HD = 8, 64
RW_W_RANK, RW_A_RANK, RW_G_RANK = 64, 64, 128
N_EXPERTS, TOP_K = 32, 4
SWIGLU_LIMIT, SWIGLU_ALPHA = 7.0, 1.702

RET_QK, RET_W = RET_HEADS * RET_DK, RET_HEADS * RET_DV
ML_QK, ML_W = ML_HEADS * ML_DK, ML_HEADS * ML_DV
HG_K, HG_W = HG_HEADS * HG_DK, HG_HEADS * HG_DV
RW_W = RW_HEADS * RW_HD
AB_SPLITS = (RET_QK, RET_QK, RET_W, RET_W, 2 * ML_QK, ML_W, ML_W, 4 * ML_HEADS)
RWKV_SPLITS = (RW_W, RW_W, RW_W, 2 * RW_W_RANK, 2 * RW_A_RANK, RW_G_RANK)
RWKV_IN = sum(RWKV_SPLITS)
CD_SPLITS = (HG_K, 2 * HG_K, HG_W, HG_W, RWKV_IN)

RW_GROUP = 4
RW_GW = RW_GROUP * RW_HD
RW_BASE = 8
HG_LEVELS = 6

SCAN_BATCH = 4
MM_ROWS = 256
MOE_ROWS = 512
VMEM_LIMIT = 56 * 1024 * 1024


def _split(z, sizes):
    return jnp.split(z, [int(s) for s in np.cumsum(sizes)[:-1]], axis=-1)


def _dot(a, b):
    return jnp.dot(a, b, preferred_element_type=F32)


def _dot_nt(a, b):
    return lax.dot_general(a, b, (((1,), (1,)), ((), ())), preferred_element_type=F32)


def _dot_tn(a, b):
    return lax.dot_general(a, b, (((0,), (0,)), ((), ())), preferred_element_type=F32)


def _mask_dot(m16, x):
    hi = x.astype(BF16)
    lo = (x - hi.astype(F32)).astype(BF16)
    return _dot(m16, hi) + _dot(m16, lo)


def _chunk_index(d, c, n_ctx_chunks, n_chunks):
    rev = jnp.where(c < n_ctx_chunks, n_ctx_chunks - 1 - c, n_chunks + n_ctx_chunks - 1 - c)
    return jnp.where(d == 0, c, rev)


def _mm_kernel(x_ref, w_ref, o_ref):
    o_ref[...] = _dot(x_ref[...], w_ref[...])


def matmul(x, w):
    n, k = x.shape
    m = w.shape[1]
    assert n % MM_ROWS == 0 and m % 128 == 0
    return pl.pallas_call(
        _mm_kernel,
        grid=(n // MM_ROWS,),
        in_specs=[pl.BlockSpec((MM_ROWS, k), lambda i: (i, 0)),
                  pl.BlockSpec((k, m), lambda i: (0, 0))],
        out_specs=pl.BlockSpec((MM_ROWS, m), lambda i: (i, 0)),
        out_shape=jax.ShapeDtypeStruct((n, m), F32),
        compiler_params=pltpu.CompilerParams(dimension_semantics=("parallel",),
                                             vmem_limit_bytes=VMEM_LIMIT),
        name="dense_proj",
    )(x, w)


def _ret_kernel(lg_ref, q_ref, k_ref, v_ref, o_ref, s_ref):
    d = pl.program_id(1)
    c = pl.program_id(2)

    @pl.when(c == 0)
    def _():
        s_ref[...] = jnp.zeros_like(s_ref)

    L = CHUNK
    ti = lax.broadcasted_iota(jnp.int32, (L, L), 0)
    si = lax.broadcasted_iota(jnp.int32, (L, L), 1)
    dist = jnp.where(d == 0, ti - si, si - ti)
    causal = dist >= 0
    distf = jnp.maximum(dist, 0).astype(F32)
    row = lax.broadcasted_iota(jnp.int32, (L, 1), 0)
    pos = jnp.where(d == 0, row, L - 1 - row).astype(F32)
    lg = [lg_ref[d, h] for h in range(RET_HEADS)]
    dm = [jnp.where(causal, jnp.exp(x * distf), 0.0) for x in lg]
    q_dec = [jnp.exp(x * (pos + 1.0)) for x in lg]
    k_dec = [jnp.exp(x * (L - 1.0 - pos)) for x in lg]
    chains = [(bi, h) for bi in range(SCAN_BATCH) for h in range(RET_HEADS)]
    qh = [q_ref[bi, :, h * RET_DK:(h + 1) * RET_DK] for bi, h in chains]
    kh = [k_ref[bi, :, h * RET_DK:(h + 1) * RET_DK] for bi, h in chains]
    vh = [v_ref[bi, :, h * RET_DV:(h + 1) * RET_DV].astype(BF16) for bi, h in chains]
    att = [_dot_nt(q_.astype(BF16), k_.astype(BF16)) * dm[h] for q_, k_, (_, h) in zip(qh, kh, chains)]
    s = [s_ref[bi, h] for bi, h in chains]
    o = [_dot((q_ * q_dec[h]).astype(BF16), s_.astype(BF16)) + _dot(a_.astype(BF16), v_)
         for q_, s_, a_, v_, (_, h) in zip(qh, s, att, vh, chains)]
    for (bi, h), o_ in zip(chains, o):
        o_ref[0, bi, :, h * RET_DV:(h + 1) * RET_DV] = o_
    new_s = [jnp.exp(lg[h] * L) * s_ + _dot_tn((k_ * k_dec[h]).astype(BF16), v_)
             for s_, k_, v_, (_, h) in zip(s, kh, vh, chains)]
    for (bi, h), s_ in zip(chains, new_s):
        s_ref[bi, h] = s_


def retention_scan(q, k, v, log_gamma, n_ctx_chunks):
    b, t, _ = q.shape
    n_chunks = t // CHUNK
    idx = functools.partial(_chunk_index, n_ctx_chunks=n_ctx_chunks, n_chunks=n_chunks)
    in_map = lambda bi, d, c, lg: (bi, idx(d, c), 0)
    return pl.pallas_call(
        _ret_kernel,
        grid_spec=pltpu.PrefetchScalarGridSpec(
            num_scalar_prefetch=1,
            grid=(b // SCAN_BATCH, 2, n_chunks),
            in_specs=[pl.BlockSpec((SCAN_BATCH, CHUNK, RET_QK), in_map),
                      pl.BlockSpec((SCAN_BATCH, CHUNK, RET_QK), in_map),
                      pl.BlockSpec((SCAN_BATCH, CHUNK, RET_W), in_map)],
            out_specs=pl.BlockSpec((1, SCAN_BATCH, CHUNK, RET_W), lambda bi, d, c, lg: (d, bi, idx(d, c), 0)),
            scratch_shapes=[pltpu.VMEM((SCAN_BATCH, RET_HEADS, RET_DK, RET_DV), F32)]),
        out_shape=jax.ShapeDtypeStruct((2, b, t, RET_W), F32),
        compiler_params=pltpu.CompilerParams(
            dimension_semantics=("parallel", "parallel", "arbitrary"), vmem_limit_bytes=VMEM_LIMIT),
        name="retention_scan",
    )(log_gamma, q, k, v)


def _mlstm_kernel(q_ref, k_ref, v_ref, gc_ref, gr_ref, o_ref, c_ref, m_ref):
    d = pl.program_id(1)
    c = pl.program_id(2)

    @pl.when(c == 0)
    def _():
        c_ref[...] = jnp.zeros_like(c_ref)
        m_ref[...] = jnp.zeros_like(m_ref)

    L = CHUNK
    ti = lax.broadcasted_iota(jnp.int32, (L, L), 0)
    si = lax.broadcasted_iota(jnp.int32, (L, L), 1)
    dist = jnp.where(d == 0, ti - si, si - ti)
    le = dist >= 0
    le_t = dist <= 0
    lane = lax.broadcasted_iota(jnp.int32, (L, ML_DV), 1)
    ones_col = jnp.where(lane == 0, 1.0, 0.0).astype(BF16)
    H = ML_HEADS
    chains = [(bi, h) for bi in range(SCAN_BATCH) for h in range(H)]
    each = lambda fn, *lists: [fn(*xs) for xs in zip(*lists)]
    gc = [gc_ref[0, bi] for bi in range(SCAN_BATCH)]
    gr = [gr_ref[0, bi, 0] for bi in range(SCAN_BATCH)]
    i_col = [gc[bi][:, h:h + 1] for bi, h in chains]
    f_col = [gc[bi][:, H + h:H + h + 1] for bi, h in chains]
    i_row = [gr[bi][h:h + 1, :] for bi, h in chains]
    f_row = [gr[bi][H + h:H + h + 1, :] for bi, h in chains]
    b_col = each(lambda f_: jnp.sum(jnp.where(le, f_, 0.0), axis=1, keepdims=True), f_row)
    b_row = each(lambda f_: jnp.sum(jnp.where(le_t, f_, 0.0), axis=0, keepdims=True), f_col)
    b_last = each(lambda f_: jnp.sum(f_, axis=1, keepdims=True), f_row)
    m = [m_ref[bi * H + h:bi * H + h + 1, 0:1] for bi, h in chains]
    log_d = each(lambda bc, br, ir: jnp.where(le, bc - br + ir, -jnp.inf), b_col, b_row, i_row)
    inter = each(lambda bc, m_: bc + m_, b_col, m)
    m_row = each(lambda in_, ld: jnp.maximum(in_, jnp.max(ld, axis=1, keepdims=True)), inter, log_d)
    qh = [q_ref[bi, :, h * ML_DK:(h + 1) * ML_DK].astype(BF16) for bi, h in chains]
    kh = [k_ref[bi, :, h * ML_DK:(h + 1) * ML_DK] for bi, h in chains]
    v_aug = [jnp.concatenate([v_ref[bi, :, h * ML_DV:(h + 1) * ML_DV].astype(BF16), ones_col], axis=1)
             for bi, h in chains]
    s = each(lambda q_, k_, ld, mr: _dot_nt(q_, k_.astype(BF16)) * jnp.exp(ld - mr), qh, kh, log_d, m_row)
    c_aug = [c_ref[bi, h] for bi, h in chains]
    num = each(lambda in_, mr, q_, c_, s_, v_: jnp.exp(in_ - mr) * _dot(q_, c_.astype(BF16))
               + _dot(s_.astype(BF16), v_), inter, m_row, qh, c_aug, s, v_aug)
    hh = each(lambda n_, mr: n_[:, :ML_DV] / jnp.maximum(jnp.abs(n_[:, ML_DV:ML_DV + 1]), jnp.exp(-mr)), num, m_row)
    for (bi, h), h_ in zip(chains, hh):
        o_ref[0, bi, :, h * ML_DV:(h + 1) * ML_DV] = h_
    m_new = each(lambda bl, m_, br, ir: jnp.maximum(bl + m_, jnp.max(bl - br + ir, axis=1, keepdims=True)),
                 b_last, m, b_row, i_row)
    new_c = each(lambda bl, m_, mn, c_, k_, bc, ic, v_: jnp.exp(bl + m_ - mn) * c_ + _dot_tn(
        (k_ * jnp.exp(bl - bc + ic - mn)).astype(BF16), v_), b_last, m, m_new, c_aug, kh, b_col, i_col, v_aug)
    for (bi, h), c_, mn in zip(chains, new_c, m_new):
        c_ref[bi, h] = c_
        m_ref[bi * H + h:bi * H + h + 1, :] = jnp.broadcast_to(mn, (1, 128))


def mlstm_scan(q, k, v, g_col, g_row, n_ctx_chunks):
    b, t, _ = q.shape
    n_chunks = t // CHUNK
    idx = functools.partial(_chunk_index, n_ctx_chunks=n_ctx_chunks, n_chunks=n_chunks)
    in_map = lambda bi, d, c: (bi, idx(d, c), 0)
    return pl.pallas_call(
        _mlstm_kernel,
        grid=(b // SCAN_BATCH, 2, n_chunks),
        in_specs=[pl.BlockSpec((SCAN_BATCH, CHUNK, ML_QK), in_map),
                  pl.BlockSpec((SCAN_BATCH, CHUNK, ML_QK), in_map),
                  pl.BlockSpec((SCAN_BATCH, CHUNK, ML_W), in_map),
                  pl.BlockSpec((1, SCAN_BATCH, CHUNK, 2 * ML_HEADS), lambda bi, d, c: (d, bi, idx(d, c), 0)),
                  pl.BlockSpec((1, SCAN_BATCH, 1, 2 * ML_HEADS, CHUNK), lambda bi, d, c: (d, bi, idx(d, c), 0, 0))],
        out_specs=pl.BlockSpec((1, SCAN_BATCH, CHUNK, ML_W), lambda bi, d, c: (d, bi, idx(d, c), 0)),
        out_shape=jax.ShapeDtypeStruct((2, b, t, ML_W), F32),
        scratch_shapes=[pltpu.VMEM((SCAN_BATCH, ML_HEADS, ML_DK, 2 * ML_DV), F32),
                        pltpu.VMEM((SCAN_BATCH * ML_HEADS, 128), F32)],
        compiler_params=pltpu.CompilerParams(
            dimension_semantics=("parallel", "parallel", "arbitrary"), vmem_limit_bytes=VMEM_LIMIT),
        name="mlstm_scan",
    )(q, k, v, g_col, g_row)


def _hgrn_masks():
    L = CHUNK
    stack = np.zeros((2, (1 + 2 * HG_LEVELS) * L, L), np.float32)
    pair = np.zeros((2, HG_LEVELS + 1, L, L), np.float32)
    for d in range(2):
        p = np.arange(L) if d == 0 else L - 1 - np.arange(L)
        pt, ps = p[:, None], p[None, :]
        stack[d, :L] = ps <= pt
        for l in range(HG_LEVELS):
            parent, half = p >> (l + 1), (p >> l) & 1
            split = parent * (2 << l) + (1 << l) - 1
            same = parent[:, None] == parent[None, :]
            e = same & (half[:, None] == 1) & (ps > split[:, None]) & (ps <= pt)
            f = same & (half[:, None] == 0) & (ps > pt) & (ps <= split[:, None])
            stack[d, (1 + 2 * l) * L:(2 + 2 * l) * L] = e
            stack[d, (2 + 2 * l) * L:(3 + 2 * l) * L] = f
            pair[d, l] = same & (half[:, None] == 1) & (half[None, :] == 0)
        pair[d, HG_LEVELS] = np.eye(L)
    return stack, pair


def _hgrn_kernel(q_ref, hf_ref, v_ref, lb_ref, stack_ref, pair_ref, o_ref, s_ref):
    c = pl.program_id(2)

    @pl.when(c == 0)
    def _():
        s_ref[...] = jnp.zeros_like(s_ref)

    L = CHUNK
    lb = lb_ref[...]
    f = lb + (1.0 - lb) * jax.nn.sigmoid(jnp.concatenate([hf_ref[bi] for bi in range(SCAN_BATCH)], axis=1))
    kk = 1.0 - f
    g = jnp.log(f)
    sums = _mask_dot(stack_ref[0], g)
    chains = [(bi, h) for bi in range(SCAN_BATCH) for h in range(HG_HEADS)]
    each = lambda fn, *lists: [fn(*xs) for xs in zip(*lists)]
    hs = [slice(bi * HG_K + h * HG_DK, bi * HG_K + (h + 1) * HG_DK) for bi, h in chains]
    qh = [q_ref[bi, :, h * HG_DK:(h + 1) * HG_DK] for bi, h in chains]
    kh = [kk[:, s_] for s_ in hs]
    vh = [v_ref[bi, :, h * HG_DV:(h + 1) * HG_DV].astype(BF16) for bi, h in chains]
    b = [sums[0:L, s_] for s_ in hs]
    att = each(lambda q_, k_: pair_ref[0, HG_LEVELS] * _dot_nt(q_.astype(BF16), k_.astype(BF16)), qh, kh)
    for l in range(HG_LEVELS):
        e = [sums[(1 + 2 * l) * L:(2 + 2 * l) * L, s_] for s_ in hs]
        fl = [sums[(2 + 2 * l) * L:(3 + 2 * l) * L, s_] for s_ in hs]
        att = each(lambda a_, q_, k_, e_, f_: a_ + pair_ref[0, l] * _dot_nt(
            (q_ * jnp.exp(e_)).astype(BF16), (k_ * jnp.exp(f_)).astype(BF16)), att, qh, kh, e, fl)
    b_last = [jnp.sum(g[:, s_], axis=0, keepdims=True) for s_ in hs]
    st = [s_ref[bi, h] for bi, h in chains]
    o = each(lambda q_, b_, s_, a_, v_: _dot_nt((q_ * jnp.exp(b_)).astype(BF16), s_.astype(BF16))
             + _dot(a_.astype(BF16), v_), qh, b, st, att, vh)
    for (bi, h), o_ in zip(chains, o):
        o_ref[0, bi, :, h * HG_DV:(h + 1) * HG_DV] = o_
    new_st = each(lambda s_, bl, v_, k_, b_: s_ * jnp.exp(bl) + _dot_tn(v_, (k_ * jnp.exp(bl - b_)).astype(BF16)),
                  st, b_last, vh, kh, b)
    for (bi, h), s_ in zip(chains, new_st):
        s_ref[bi, h] = s_


def hgrn_scan(q, hf, v, lb, n_ctx_chunks):
    b, t, _ = q.shape
    n_chunks = t // CHUNK
    stack, pair = _hgrn_masks()
    idx = functools.partial(_chunk_index, n_ctx_chunks=n_ctx_chunks, n_chunks=n_chunks)
    in_map = lambda bi, d, c: (bi, idx(d, c), 0)
    return pl.pallas_call(
        _hgrn_kernel,
        grid=(b // SCAN_BATCH, 2, n_chunks),
        in_specs=[pl.BlockSpec((SCAN_BATCH, CHUNK, HG_K), in_map),
                  pl.BlockSpec((SCAN_BATCH, CHUNK, HG_K), lambda bi, d, c: (bi, idx(d, c), d)),
                  pl.BlockSpec((SCAN_BATCH, CHUNK, HG_W), in_map),
                  pl.BlockSpec((1, SCAN_BATCH * HG_K), lambda bi, d, c: (0, 0)),
                  pl.BlockSpec((1,) + stack.shape[1:], lambda bi, d, c: (d, 0, 0)),
                  pl.BlockSpec((1,) + pair.shape[1:], lambda bi, d, c: (d, 0, 0, 0))],
        out_specs=pl.BlockSpec((1, SCAN_BATCH, CHUNK, HG_W), lambda bi, d, c: (d, bi, idx(d, c), 0)),
        out_shape=jax.ShapeDtypeStruct((2, b, t, HG_W), F32),
        scratch_shapes=[pltpu.VMEM((SCAN_BATCH, HG_HEADS, HG_DV, HG_DK), F32)],
        compiler_params=pltpu.CompilerParams(
            dimension_semantics=("parallel", "parallel", "arbitrary"), vmem_limit_bytes=VMEM_LIMIT),
        name="hgrn_scan",
    )(q, hf, v, jnp.tile(lb, (1, SCAN_BATCH)), jnp.asarray(stack, BF16), jnp.asarray(pair, F32))


def _rwkv_masks():
    L, G = CHUNK, RW_GROUP
    n = G * L
    head = np.arange(n) // L
    bd = (head[:, None] == head[None, :]).astype(np.float32)
    cum = np.zeros((2, L, L), np.float32)
    n_lvl = int(np.log2(L // RW_BASE))
    wide = np.zeros((2, 4 + n_lvl, L, n), np.float32)
    for d in range(2):
        p = np.arange(L) if d == 0 else L - 1 - np.arange(L)
        pt, ps = p[:, None], p[None, :]
        cum[d] = ps <= pt
        m = [ps < pt, ps <= pt, np.eye(L, dtype=bool), (pt // RW_BASE) == (ps // RW_BASE)]
        for i in range(n_lvl):
            blk = RW_BASE << i
            parent, half = p // (2 * blk), (p // blk) % 2
            m.append((parent[:, None] == parent[None, :]) & (half[:, None] == 1) & (half[None, :] == 0))
        for i, mi in enumerate(m):
            wide[d, i] = np.tile(mi, (1, G))
    return bd, cum, wide


def _unit_lower_inverse(n_w, wide_ref, expand):
    mul = lambda a_w, b_w: [_dot(x.astype(BF16), expand(y)) for x, y in zip(a_w, b_w)]
    nd = [n * wide_ref[0, 3] for n in n_w]
    t_inv = [wide_ref[0, 2] - x for x in nd]
    pw = nd
    for _ in range(int(np.log2(RW_BASE)) - 1):
        pw = mul(pw, pw)
        t_inv = [t + x for t, x in zip(t_inv, mul(t_inv, pw))]
    for i in range(wide_ref.shape[1] - 4):
        tc = mul(t_inv, [n * wide_ref[0, 4 + i] for n in n_w])
        t_inv = [t - x for t, x in zip(t_inv, mul(tc, t_inv))]
    return t_inv


def _rwkv_kernel(r_ref, k_ref, v_ref, kk_ref, a_ref, lw_ref, bd_ref, cum_ref, wide_ref, o_ref, s_ref):
    c = pl.program_id(2)

    @pl.when(c == 0)
    def _():
        s_ref[...] = jnp.zeros_like(s_ref)

    G = RW_GROUP
    bd = bd_ref[...]
    strict = wide_ref[0, 0]
    incl = wide_ref[0, 1]
    cum = cum_ref[0]

    def expand(x):
        return (jnp.concatenate([x] * G, axis=0) * bd).astype(BF16)

    L = CHUNK
    chains = [(bi, g) for bi in range(SCAN_BATCH) for g in range(RW_HEADS // G)]
    cols = [slice(g * RW_GW, (g + 1) * RW_GW) for _, g in chains]
    each = lambda fn, *lists: [fn(*xs) for xs in zip(*lists)]
    r = [r_ref[bi, :, cs_] for (bi, _), cs_ in zip(chains, cols)]
    k = [k_ref[0, bi, :, cs_] for (bi, _), cs_ in zip(chains, cols)]
    v = [v_ref[bi, :, cs_] for (bi, _), cs_ in zip(chains, cols)]
    kk = [kk_ref[bi, :, cs_] for (bi, _), cs_ in zip(chains, cols)]
    lw = [lw_ref[0, bi, :, cs_] for (bi, _), cs_ in zip(chains, cols)]
    kb = [x * a_ref[0, bi, :, cs_] for x, (bi, _), cs_ in zip(kk, chains, cols)]
    cs = each(lambda x: _mask_dot(cum, x), lw)
    c_last = each(lambda x: jnp.sum(x, axis=0, keepdims=True), lw)
    p_inv = each(lambda x: jnp.exp(-x), cs)
    p_to_end = each(lambda cl, x: jnp.exp(cl - x), c_last, cs)
    kkd_rp = each(lambda kk_, r_, cs_, lw_: jnp.concatenate(
        [kk_ * jnp.exp(cs_ - lw_), r_ * jnp.exp(cs_)], axis=0).astype(BF16), kk, r, cs, lw)
    v_bd = each(expand, v)
    sc_k = each(lambda q_, k_, p_: _dot_nt(q_, expand(k_ * p_)), kkd_rp, k, p_inv)
    sc_b = each(lambda q_, b_, p_: _dot_nt(q_, expand(b_ * p_)), kkd_rp, kb, p_inv)
    n_w = each(lambda x: strict * x[:L], sc_b)
    m_w = each(lambda x: jnp.concatenate([strict * x[:L], incl * x[L:]], axis=0).astype(BF16), sc_k)
    mrb_w = each(lambda x: (incl * x[L:]).astype(BF16), sc_b)
    st = [s_ref[bi, g] for bi, g in chains]
    from_state = each(lambda q_, s_, m_, vb_: _dot_nt(q_, s_.astype(BF16)) + _dot(m_, vb_),
                      kkd_rp, st, m_w, v_bd)
    t_inv = _unit_lower_inverse(n_w, wide_ref, expand)
    u = each(lambda t_, f_: _dot(t_.astype(BF16), expand(f_[:L])), t_inv, from_state)
    y = each(lambda f_, m_, u_: f_[L:] - _dot(m_, expand(u_)), from_state, mrb_w, u)
    for (bi, g), cs_, y_ in zip(chains, cols, y):
        o_ref[0, bi, :, cs_] = y_
    new_st = each(lambda s_, cl, v_, u_, k_, b_, p_: s_ * jnp.exp(cl) + bd * _dot_tn(
        jnp.concatenate([v_, -u_], axis=0).astype(BF16),
        jnp.concatenate([k_ * p_, b_ * p_], axis=0).astype(BF16)), st, c_last, v, u, k, kb, p_to_end)
    for (bi, g), s_ in zip(chains, new_st):
        s_ref[bi, g] = s_


def rwkv_scan(r, k_eff, v, kk, a, lw, n_ctx_chunks):
    b, t, _ = r.shape
    n_chunks = t // CHUNK
    assert b % SCAN_BATCH == 0
    bd, cum, wide = _rwkv_masks()
    idx = functools.partial(_chunk_index, n_ctx_chunks=n_ctx_chunks, n_chunks=n_chunks)
    shared = pl.BlockSpec((SCAN_BATCH, CHUNK, RW_W), lambda bi, d, c: (bi, idx(d, c), 0))
    per_dir = pl.BlockSpec((1, SCAN_BATCH, CHUNK, RW_W), lambda bi, d, c: (d, bi, idx(d, c), 0))
    n = RW_GROUP * CHUNK
    return pl.pallas_call(
        _rwkv_kernel,
        grid=(b // SCAN_BATCH, 2, n_chunks),
        in_specs=[shared, per_dir, shared, shared, per_dir, per_dir,
                  pl.BlockSpec((n, RW_GW), lambda bi, d, c: (0, 0)),
                  pl.BlockSpec((1, CHUNK, CHUNK), lambda bi, d, c: (d, 0, 0)),
                  pl.BlockSpec((1,) + wide.shape[1:], lambda bi, d, c: (d, 0, 0, 0))],
        out_specs=per_dir,
        out_shape=jax.ShapeDtypeStruct((2, b, t, RW_W), F32),
        scratch_shapes=[pltpu.VMEM((SCAN_BATCH, RW_HEADS // RW_GROUP, RW_GW, RW_GW), F32)],
        compiler_params=pltpu.CompilerParams(
            dimension_semantics=("parallel", "parallel", "arbitrary"), vmem_limit_bytes=VMEM_LIMIT),
        name="rwkv7_scan",
    )(r, k_eff, v, kk, a, lw, jnp.asarray(bd, F32), jnp.asarray(cum, BF16), jnp.asarray(wide, F32))


def _moe_kernel(e_ref, x_ref, wgu_ref, bgu_ref, wdn_ref, bdn_ref, o_ref, wgu16_ref, wdn16_ref):
    i = pl.program_id(0)
    new_expert = jnp.logical_or(i == 0, e_ref[i] != e_ref[jnp.maximum(i - 1, 0)])

    @pl.when(new_expert)
    def _():
        wgu16_ref[...] = wgu_ref[0, 0].astype(BF16)
        wdn16_ref[...] = wdn_ref[0, 0].astype(BF16)

    d_ff = wdn16_ref.shape[0]
    gu = _dot(x_ref[...].astype(BF16), wgu16_ref[...]) + bgu_ref[0, 0]
    glu = jnp.minimum(gu[:, :d_ff], SWIGLU_LIMIT)
    lin = jnp.clip(gu[:, d_ff:], -SWIGLU_LIMIT, SWIGLU_LIMIT)
    act = glu * jax.nn.sigmoid(SWIGLU_ALPHA * glu) * (lin + 1.0)
    o_ref[...] = _dot(act.astype(BF16), wdn16_ref[...]) + bdn_ref[0, 0]


def moe_experts(xb, blk_e, layer, w_gu, b_gu, w_dn, b_dn):
    n_rows, dm = xb.shape
    d_ff = w_dn.shape[2]
    n_blocks = n_rows // MOE_ROWS
    return pl.pallas_call(
        _moe_kernel,
        grid_spec=pltpu.PrefetchScalarGridSpec(
            num_scalar_prefetch=1,
            grid=(n_blocks,),
            in_specs=[pl.BlockSpec((MOE_ROWS, dm), lambda i, e: (i, 0)),
                      pl.BlockSpec((1, 1, dm, 2 * d_ff), lambda i, e: (layer, e[i], 0, 0)),
                      pl.BlockSpec((1, 1, 1, 2 * d_ff), lambda i, e: (layer, e[i], 0, 0)),
                      pl.BlockSpec((1, 1, d_ff, dm), lambda i, e: (layer, e[i], 0, 0)),
                      pl.BlockSpec((1, 1, 1, dm), lambda i, e: (layer, e[i], 0, 0))],
            out_specs=pl.BlockSpec((MOE_ROWS, dm), lambda i, e: (i, 0)),
            scratch_shapes=[pltpu.VMEM((dm, 2 * d_ff), BF16), pltpu.VMEM((d_ff, dm), BF16)]),
        out_shape=jax.ShapeDtypeStruct((n_rows, dm), F32),
        compiler_params=pltpu.CompilerParams(dimension_semantics=("arbitrary",),
                                             vmem_limit_bytes=VMEM_LIMIT),
        name="moe_experts",
    )(blk_e, xb, w_gu, b_gu, w_dn, b_dn)


def moe_ffn(t, router_w, router_b, layer, w_gu, b_gu, w_dn, b_dn):
    n, dm = t.shape
    logits = jnp.dot(t, router_w, precision=lax.Precision.HIGHEST) + router_b
    expert_id = lax.broadcasted_iota(jnp.int32, logits.shape, 1)
    rest, top_logit, top_e = logits, [], []
    for _ in range(TOP_K):
        best = jnp.max(rest, axis=-1, keepdims=True)
        pick = jnp.min(jnp.where(rest == best, expert_id, N_EXPERTS), axis=-1, keepdims=True)
        top_logit.append(best)
        top_e.append(pick)
        rest = jnp.where(expert_id == pick, -jnp.inf, rest)
    top_logit, top_e = jnp.concatenate(top_logit, axis=-1), jnp.concatenate(top_e, axis=-1)
    top_w = jax.nn.softmax(top_logit, axis=-1)
    flat_e = top_e.reshape(-1).astype(jnp.int32)
    n_asg = n * TOP_K
    seg = 128
    assert n_asg % seg == 0
    onehot = (flat_e.reshape(n_asg // seg, seg, 1) == jnp.arange(N_EXPERTS, dtype=jnp.int32)).astype(F32)
    within = jnp.einsum('ts,bse->bte', jnp.tril(jnp.ones((seg, seg), F32)), onehot)
    seg_tot = within[:, -1, :]
    before = jnp.cumsum(seg_tot, axis=0) - seg_tot
    rank = (jnp.sum(onehot * (within + before[:, None, :]), axis=-1) - 1.0).astype(jnp.int32).reshape(-1)
    counts = jnp.sum(seg_tot, axis=0).astype(jnp.int32)
    padded = (counts + MOE_ROWS - 1) // MOE_ROWS * MOE_ROWS
    pad_end = jnp.cumsum(padded)
    pad_start = pad_end - padded
    grp_start = jnp.cumsum(counts) - counts
    n_blocks = -(-n_asg // MOE_ROWS) + N_EXPERTS
    n_rows = n_blocks * MOE_ROWS
    blk_start = jnp.arange(n_blocks, dtype=jnp.int32) * MOE_ROWS
    blk_e = jnp.minimum(jnp.sum((pad_end[None, :] <= blk_start[:, None]).astype(jnp.int32), axis=1),
                        N_EXPERTS - 1)
    assert N_EXPERTS * n_asg < 2 ** 31
    order = jnp.sort(flat_e * n_asg + jnp.arange(n_asg, dtype=jnp.int32)) % n_asg
    e_row = jnp.repeat(blk_e, MOE_ROWS)
    j = jnp.arange(n_rows, dtype=jnp.int32) - pad_start[e_row]
    used = j < counts[e_row]
    row_tok = jnp.where(used, order[jnp.where(used, grp_start[e_row] + j, 0)] // TOP_K, 0)
    xb = t[row_tok]
    yb = moe_experts(xb, blk_e, layer, w_gu, b_gu[:, :, None, :], w_dn, b_dn[:, :, None, :])
    slot = (pad_start[flat_e] + rank).reshape(n, TOP_K)
    y = jnp.zeros((n, dm), F32)
    for j in range(TOP_K):
        y = y + yb[slot[:, j]] * top_w[:, j:j + 1]
    return y


def rms_norm(x, g):
    return x * lax.rsqrt(jnp.mean(x * x, axis=-1, keepdims=True) + EPS) * g


def head_rms_norm(o, g):
    return o * lax.rsqrt(jnp.mean(o * o, axis=-1, keepdims=True) + EPS) * g


def dwconv3(x, w, n_ctx):
    t = x.shape[1]
    xp = jnp.pad(x, ((0, 0), (1, 1), (0, 0)))
    pos = jnp.arange(t)[None, :, None]
    has_prev = (pos != 0) & (pos != n_ctx)
    has_next = (pos != n_ctx - 1) & (pos != t - 1)
    return (w[0] * jnp.where(has_prev, xp[:, :-2], 0.0) + w[1] * x
            + w[2] * jnp.where(has_next, xp[:, 2:], 0.0))


def rope_tables(n_ctx, seq):
    rows = seq // GRID_W
    pos_r = jnp.repeat(jnp.arange(rows, dtype=F32), GRID_W)
    pos_c = jnp.tile(jnp.arange(GRID_W, dtype=F32), rows)
    nf = RET_DK // 4
    inv = ROPE_BASE ** (-jnp.arange(nf, dtype=F32) / nf)
    ang = jnp.concatenate([pos_r[:, None] * inv, pos_c[:, None] * inv], axis=-1)
    cos = jnp.concatenate([jnp.ones((n_ctx, RET_DK // 2), F32), jnp.cos(ang)], axis=0)
    sin = jnp.concatenate([jnp.zeros((n_ctx, RET_DK // 2), F32), jnp.sin(ang)], axis=0)
    return cos, sin


def apply_rope(x, cos, sin):
    half = x.shape[-1] // 2
    x1, x2 = x[..., :half], x[..., half:]
    c, s = cos[None, :, None, :], sin[None, :, None, :]
    return jnp.concatenate([x1 * c - x2 * s, x1 * s + x2 * c], axis=-1)


def project(u, w):
    b, t, dm = u.shape
    m = w.shape[1]
    m_pad = -(-m // 128) * 128
    w16 = jnp.pad(w, ((0, 0), (0, m_pad - m))).astype(BF16)
    z = matmul(u.reshape(b * t, dm).astype(BF16), w16)
    return z[:, :m].reshape(b, t, m)


def mixer_ab(u, n_ctx, rope, w_in, w_out, ret_decay, ret_norm, ml_conv, ml_gate_b, ml_norm):
    b, t, _ = u.shape
    n_ctx_chunks = n_ctx // CHUNK
    rq, rk, rv, rg, mqk, mv, mo, mg = _split(project(u, w_in), AB_SPLITS)
    rq = apply_rope(rq.reshape(b, t, RET_HEADS, RET_DK), *rope).reshape(b, t, RET_QK)
    rk = apply_rope(rk.reshape(b, t, RET_HEADS, RET_DK) * RET_DK ** -0.5, *rope).reshape(b, t, RET_QK)
    log_gamma = jnp.log1p(-jnp.exp(ret_decay))
    ret = retention_scan(rq, rk, rv, log_gamma, n_ctx_chunks)
    ret = ret[0] + ret[1]

    mq, mk = _split(jax.nn.silu(dwconv3(mqk, ml_conv, n_ctx)), (ML_QK, ML_QK))
    mq = mq * ML_DK ** -0.5
    gates = (mg + ml_gate_b).reshape(b, t, 2, 2, ML_HEADS)
    gates = jnp.stack([gates[:, :, :, 0], jax.nn.log_sigmoid(gates[:, :, :, 1])], axis=3)
    g_col = jnp.moveaxis(gates, 2, 0).reshape(2, b, t, 2 * ML_HEADS)
    g_row = jnp.swapaxes(g_col.reshape(2, b, t // CHUNK, CHUNK, 2 * ML_HEADS), 3, 4)
    ml = mlstm_scan(mq, mk, mv, g_col, g_row, n_ctx_chunks)
    ml = ml[0] + ml[1]

    ret = head_rms_norm(ret.reshape(b, t, RET_HEADS, RET_DV), ret_norm).reshape(b, t, RET_W) * jax.nn.silu(rg)
    ml = head_rms_norm(ml.reshape(b, t, ML_HEADS, ML_DV), ml_norm).reshape(b, t, ML_W) * jax.nn.sigmoid(mo)
    return project(jnp.concatenate([ret, ml], axis=-1), w_out)


def mixer_cd(u, n_ctx, w_in, w_out, lb, hg_norm, rw_shift, rw_w0, rw_w2, rw_a0, rw_a2, rw_g2,
             rw_kk_scale, rw_k_a, rw_r_k, rw_norm):
    b, t, _ = u.shape
    n_ctx_chunks = n_ctx // CHUNK
    hq, hf, hi, hg, zr = _split(project(u, w_in), CD_SPLITS)
    hgo = hgrn_scan(hq, hf, hi, lb.reshape(1, HG_K), n_ctx_chunks)
    hgo = hgo[0] + hgo[1]

    r, k, v, wl, al, gl = _split(dwconv3(zr, rw_shift, n_ctx), RWKV_SPLITS)
    kk = (k * rw_kk_scale).reshape(b, t, RW_HEADS, RW_HD)
    kk = (kk * lax.rsqrt(jnp.sum(kk * kk, axis=-1, keepdims=True) + EPS)).reshape(b, t, RW_W)
    gate = jax.nn.sigmoid(gl) @ rw_g2
    k_eff, a_all, lw_all = [], [], []
    for d in range(2):
        wl_d = wl[..., d * RW_W_RANK:(d + 1) * RW_W_RANK]
        al_d = al[..., d * RW_A_RANK:(d + 1) * RW_A_RANK]
        w_log = -jax.nn.softplus(-(rw_w0[d] + jnp.tanh(wl_d) @ rw_w2[d])) - 0.5
        lw_all.append(-jnp.exp(w_log))
        a = jax.nn.sigmoid(rw_a0[d] + al_d @ rw_a2[d])
        a_all.append(a)
        k_eff.append(k * (1.0 + (a - 1.0) * rw_k_a))
    k_eff, a_all, lw_all = jnp.stack(k_eff), jnp.stack(a_all), jnp.stack(lw_all)
    rwo = rwkv_scan(r, k_eff, v, kk, a_all, lw_all, n_ctx_chunks)
    rwo = (rwo[0] + rwo[1]).reshape(b, t, RW_HEADS, RW_HD)

    r4, v4 = r.reshape(b, t, RW_HEADS, RW_HD), v.reshape(b, t, RW_HEADS, RW_HD)
    bonus = sum(jnp.sum(r4 * k_eff[d].reshape(b, t, RW_HEADS, RW_HD) * rw_r_k, axis=-1, keepdims=True) * v4
                for d in range(2))
    hgo = head_rms_norm(hgo.reshape(b, t, HG_HEADS, HG_DV), hg_norm).reshape(b, t, HG_W) * jax.nn.sigmoid(hg)
    rwo = (head_rms_norm(rwo, rw_norm) + bonus).reshape(b, t, RW_W) * gate
    return project(jnp.concatenate([hgo, rwo], axis=-1), w_out)


def hgrn_lower_bound(p, layer):
    sm = jax.nn.softmax(p, axis=0)
    return jnp.cumsum(sm, axis=0)[layer] - sm[0]


def kernel(x, c, ctx, c_ctx, mod_w, mod_b, norm_mix, norm_ffn, norm_final, ab_w_in, ab_w_out, ret_decay, ret_norm, mlstm_conv, mlstm_gate_b, mlstm_norm, cd_w_in, cd_w_out, hgrn_lb, hgrn_norm, rwkv_shift, rwkv_w0, rwkv_w2, rwkv_a0, rwkv_a2, rwkv_g2, rwkv_kk_scale, rwkv_k_a, rwkv_r_k, rwkv_norm, router_w, router_b, exp_w_gate_up, exp_b_gate_up, exp_w_down, exp_b_down):
    bsz, seq, dm = x.shape
    n_ctx = ctx.shape[1]
    depth = mod_w.shape[0]
    t = n_ctx + seq
    rope = rope_tables(n_ctx, seq)
    cond_l = jax.nn.silu(c)
    cond_c = jax.nn.silu(c_ctx)[None, :]
    h = jnp.concatenate([ctx, x], axis=1)
    hp = lax.Precision.HIGHEST

    is_ctx = (jnp.arange(t) < n_ctx)[None, :, None]

    def per_token(m_c, m_l):
        return jnp.where(is_ctx, m_c[:, None, :], m_l[:, None, :])

    for i in range(depth):
        j = i // 2
        m_l = jnp.split(jnp.dot(cond_l, mod_w[i], precision=hp) + mod_b[i], 6, axis=-1)
        m_c = jnp.split(jnp.dot(cond_c, mod_w[i], precision=hp) + mod_b[i], 6, axis=-1)
        m = [per_token(a, b_) for a, b_ in zip(m_c, m_l)]
        u = rms_norm(h, norm_mix[i]) * (1.0 + m[1]) + m[0]
        if i % 2 == 0:
            y = mixer_ab(u, n_ctx, rope, ab_w_in[j], ab_w_out[j], ret_decay[j], ret_norm[j],
                         mlstm_conv[j], mlstm_gate_b[j], mlstm_norm[j])
        else:
            y = mixer_cd(u, n_ctx, cd_w_in[j], cd_w_out[j], hgrn_lower_bound(hgrn_lb, i), hgrn_norm[j],
                         rwkv_shift[j], rwkv_w0[j], rwkv_w2[j], rwkv_a0[j], rwkv_a2[j], rwkv_g2[j],
                         rwkv_kk_scale[j], rwkv_k_a[j], rwkv_r_k[j], rwkv_norm[j])
        h = h + m[2] * y
        vv = rms_norm(h, norm_ffn[i]) * (1.0 + m[4]) + m[3]
        f = moe_ffn(vv.reshape(-1, dm), router_w[i], router_b[i], i, exp_w_gate_up, exp_b_gate_up,
                    exp_w_down, exp_b_down).reshape(vv.shape)
        h = h + m[5] * f
    return rms_norm(h[:, n_ctx:], norm_final)
```

```python
import functools

import numpy as np
import jax
import jax.numpy as jnp
from jax import lax
from jax.experimental import pallas as pl
from jax.experimental.pallas import tpu as pltpu

F32 = jnp.float32
BF16 = jnp.bfloat16

CHUNK = 64
GRID_W = 64
EPS = 1e-6
ROPE_BASE = 10000.0
RET_HEADS, RET_DK, RET_DV = 4, 64, 128
ML_HEADS, ML_DK, ML_DV = 4, 64, 128
HG_HEADS, HG_DK, HG_DV = 4, 128, 128
RW_HEADS, RW_HD = 8, 64
RW_W_RANK, RW_A_RANK, RW_G_RANK = 64, 64, 128
N_EXPERTS, TOP_K = 32, 4
SWIGLU_LIMIT, SWIGLU_ALPHA = 7.0, 1.702

RET_QK, RET_W = RET_HEADS * RET_DK, RET_HEADS * RET_DV
ML_QK, ML_W = ML_HEADS * ML_DK, ML_HEADS * ML_DV
HG_K, HG_W = HG_HEADS * HG_DK, HG_HEADS * HG_DV
RW_W = RW_HEADS * RW_HD
AB_SPLITS = (RET_QK, RET_QK, RET_W, RET_W, 2 * ML_QK, ML_W, ML_W, 4 * ML_HEADS)
RWKV_SPLITS = (RW_W, RW_W, RW_W, 2 * RW_W_RANK, 2 * RW_A_RANK, RW_G_RANK)
RWKV_IN = sum(RWKV_SPLITS)
CD_SPLITS = (HG_K, 2 * HG_K, HG_W, HG_W, RWKV_IN)

RW_GROUP = 4
RW_GW = RW_GROUP * RW_HD
RW_BASE = 8
HG_LEVELS = 6

SCAN_BATCH = 4
LANES = 128
MM_ROWS = 512
MOE_ROWS = 512
VMEM_LIMIT = 56 * 1024 * 1024


def _split(z, sizes):
    return jnp.split(z, [int(s) for s in np.cumsum(sizes)[:-1]], axis=-1)


def _dot(a, b):
    return jnp.dot(a, b, preferred_element_type=F32)


def _dot_nt(a, b):
    return lax.dot_general(a, b, (((1,), (1,)), ((), ())), preferred_element_type=F32)


def _dot_tn(a, b):
    return lax.dot_general(a, b, (((0,), (0,)), ((), ())), preferred_element_type=F32)


def _mask_dot(m16, x):
    hi = x.astype(BF16)
    lo = (x - hi.astype(F32)).astype(BF16)
    return _dot(m16, hi) + _dot(m16, lo)


def _chunk_index(d, c, n_ctx_chunks, n_chunks):
    rev = jnp.where(c < n_ctx_chunks, n_ctx_chunks - 1 - c, n_chunks + n_ctx_chunks - 1 - c)
    return jnp.where(d == 0, c, rev)


def _mm_kernel(x_ref, w_ref, o_ref):
    o_ref[...] = _dot(x_ref[...], w_ref[...])


def matmul(x, w):
    n, k = x.shape
    m = w.shape[1]
    assert n % MM_ROWS == 0 and m % LANES == 0
    return pl.pallas_call(
        _mm_kernel,
        grid=(n // MM_ROWS,),
        in_specs=[pl.BlockSpec((MM_ROWS, k), lambda i: (i, 0)),
                  pl.BlockSpec((k, m), lambda i: (0, 0))],
        out_specs=pl.BlockSpec((MM_ROWS, m), lambda i: (i, 0)),
        out_shape=jax.ShapeDtypeStruct((n, m), F32),
        compiler_params=pltpu.CompilerParams(dimension_semantics=("parallel",),
                                             vmem_limit_bytes=VMEM_LIMIT),
        name="dense_proj",
    )(x, w)


def _ret_kernel(lg_ref, q_ref, k_ref, v_ref, o_ref, s_ref):
    d = pl.program_id(1)
    c = pl.program_id(2)

    @pl.when(c == 0)
    def _():
        s_ref[...] = jnp.zeros_like(s_ref)

    L = CHUNK
    ti = lax.broadcasted_iota(jnp.int32, (L, L), 0)
    si = lax.broadcasted_iota(jnp.int32, (L, L), 1)
    dist = jnp.where(d == 0, ti - si, si - ti)
    causal = dist >= 0
    distf = jnp.maximum(dist, 0).astype(F32)
    row = lax.broadcasted_iota(jnp.int32, (L, 1), 0)
    pos = jnp.where(d == 0, row, L - 1 - row).astype(F32)
    lg = [lg_ref[d, h] for h in range(RET_HEADS)]
    dm = [jnp.where(causal, jnp.exp(x * distf), 0.0) for x in lg]
    q_dec = [jnp.exp(x * (pos + 1.0)) for x in lg]
    k_dec = [jnp.exp(x * (L - 1.0 - pos)) for x in lg]
    chains = [(bi, h) for bi in range(SCAN_BATCH) for h in range(RET_HEADS)]
    qh = [q_ref[bi, :, h * RET_DK:(h + 1) * RET_DK] for bi, h in chains]
    kh = [k_ref[bi, :, h * RET_DK:(h + 1) * RET_DK] for bi, h in chains]
    vh = [v_ref[bi, :, h * RET_DV:(h + 1) * RET_DV].astype(BF16) for bi, h in chains]
    att = [_dot_nt(q_.astype(BF16), k_.astype(BF16)) * dm[h] for q_, k_, (_, h) in zip(qh, kh, chains)]
    s = [s_ref[bi, h] for bi, h in chains]
    o = [_dot((q_ * q_dec[h]).astype(BF16), s_.astype(BF16)) + _dot(a_.astype(BF16), v_)
         for q_, s_, a_, v_, (_, h) in zip(qh, s, att, vh, chains)]
    for (bi, h), o_ in zip(chains, o):
        o_ref[0, bi, :, h * RET_DV:(h + 1) * RET_DV] = o_
    new_s = [jnp.exp(lg[h] * L) * s_ + _dot_tn((k_ * k_dec[h]).astype(BF16), v_)
             for s_, k_, v_, (_, h) in zip(s, kh, vh, chains)]
    for (bi, h), s_ in zip(chains, new_s):
        s_ref[bi, h] = s_


def retention_scan(q, k, v, log_gamma, n_ctx_chunks):
    b, t, _ = q.shape
    n_chunks = t // CHUNK
    idx = functools.partial(_chunk_index, n_ctx_chunks=n_ctx_chunks, n_chunks=n_chunks)
    in_map = lambda bi, d, c, lg: (bi, idx(d, c), 0)
    return pl.pallas_call(
        _ret_kernel,
        grid_spec=pltpu.PrefetchScalarGridSpec(
            num_scalar_prefetch=1,
            grid=(b // SCAN_BATCH, 2, n_chunks),
            in_specs=[pl.BlockSpec((SCAN_BATCH, CHUNK, RET_QK), in_map),
                      pl.BlockSpec((SCAN_BATCH, CHUNK, RET_QK), in_map),
                      pl.BlockSpec((SCAN_BATCH, CHUNK, RET_W), in_map)],
            out_specs=pl.BlockSpec((1, SCAN_BATCH, CHUNK, RET_W), lambda bi, d, c, lg: (d, bi, idx(d, c), 0)),
            scratch_shapes=[pltpu.VMEM((SCAN_BATCH, RET_HEADS, RET_DK, RET_DV), F32)]),
        out_shape=jax.ShapeDtypeStruct((2, b, t, RET_W), F32),
        compiler_params=pltpu.CompilerParams(
            dimension_semantics=("parallel", "parallel", "arbitrary"), vmem_limit_bytes=VMEM_LIMIT),
        name="retention_scan",
    )(log_gamma, q, k, v)


def _mlstm_kernel(q_ref, k_ref, v_ref, gc_ref, gr_ref, o_ref, c_ref, m_ref):
    d = pl.program_id(1)
    c = pl.program_id(2)

    @pl.when(c == 0)
    def _():
        c_ref[...] = jnp.zeros_like(c_ref)
        m_ref[...] = jnp.zeros_like(m_ref)

    L = CHUNK
    ti = lax.broadcasted_iota(jnp.int32, (L, L), 0)
    si = lax.broadcasted_iota(jnp.int32, (L, L), 1)
    dist = jnp.where(d == 0, ti - si, si - ti)
    le = dist >= 0
    le_t = dist <= 0
    lane = lax.broadcasted_iota(jnp.int32, (L, ML_DV), 1)
    ones_col = jnp.where(lane == 0, 1.0, 0.0).astype(BF16)
    H = ML_HEADS
    chains = [(bi, h) for bi in range(SCAN_BATCH) for h in range(H)]
    each = lambda fn, *lists: [fn(*xs) for xs in zip(*lists)]
    gc = [gc_ref[0, bi] for bi in range(SCAN_BATCH)]
    gr = [gr_ref[0, bi, 0] for bi in range(SCAN_BATCH)]
    i_col = [gc[bi][:, h:h + 1] for bi, h in chains]
    f_col = [gc[bi][:, H + h:H + h + 1] for bi, h in chains]
    i_row = [gr[bi][h:h + 1, :] for bi, h in chains]
    f_row = [gr[bi][H + h:H + h + 1, :] for bi, h in chains]
    b_col = each(lambda f_: jnp.sum(jnp.where(le, f_, 0.0), axis=1, keepdims=True), f_row)
    b_row = each(lambda f_: jnp.sum(jnp.where(le_t, f_, 0.0), axis=0, keepdims=True), f_col)
    b_last = each(lambda f_: jnp.sum(f_, axis=1, keepdims=True), f_row)
    m = [m_ref[bi * H + h:bi * H + h + 1, 0:1] for bi, h in chains]
    log_d = each(lambda bc, br, ir: jnp.where(le, bc - br + ir, -jnp.inf), b_col, b_row, i_row)
    inter = each(lambda bc, m_: bc + m_, b_col, m)
    m_row = each(lambda in_, ld: jnp.maximum(in_, jnp.max(ld, axis=1, keepdims=True)), inter, log_d)
    qh = [q_ref[bi, :, h * ML_DK:(h + 1) * ML_DK].astype(BF16) for bi, h in chains]
    kh = [k_ref[bi, :, h * ML_DK:(h + 1) * ML_DK] for bi, h in chains]
    v_aug = [jnp.concatenate([v_ref[bi, :, h * ML_DV:(h + 1) * ML_DV].astype(BF16), ones_col], axis=1)
             for bi, h in chains]
    s = each(lambda q_, k_, ld, mr: _dot_nt(q_, k_.astype(BF16)) * jnp.exp(ld - mr), qh, kh, log_d, m_row)
    c_aug = [c_ref[bi, h] for bi, h in chains]
    num = each(lambda in_, mr, q_, c_, s_, v_: jnp.exp(in_ - mr) * _dot(q_, c_.astype(BF16))
               + _dot(s_.astype(BF16), v_), inter, m_row, qh, c_aug, s, v_aug)
    hh = each(lambda n_, mr: n_[:, :ML_DV] / jnp.maximum(jnp.abs(n_[:, ML_DV:ML_DV + 1]), jnp.exp(-mr)), num, m_row)
    for (bi, h), h_ in zip(chains, hh):
        o_ref[0, bi, :, h * ML_DV:(h + 1) * ML_DV] = h_
    m_new = each(lambda bl, m_, br, ir: jnp.maximum(bl + m_, jnp.max(bl - br + ir, axis=1, keepdims=True)),
                 b_last, m, b_row, i_row)
    new_c = each(lambda bl, m_, mn, c_, k_, bc, ic, v_: jnp.exp(bl + m_ - mn) * c_ + _dot_tn(
        (k_ * jnp.exp(bl - bc + ic - mn)).astype(BF16), v_), b_last, m, m_new, c_aug, kh, b_col, i_col, v_aug)
    for (bi, h), c_, mn in zip(chains, new_c, m_new):
        c_ref[bi, h] = c_
        m_ref[bi * H + h:bi * H + h + 1, :] = jnp.broadcast_to(mn, (1, LANES))


def mlstm_scan(q, k, v, g_col, g_row, n_ctx_chunks):
    b, t, _ = q.shape
    n_chunks = t // CHUNK
    idx = functools.partial(_chunk_index, n_ctx_chunks=n_ctx_chunks, n_chunks=n_chunks)
    in_map = lambda bi, d, c: (bi, idx(d, c), 0)
    return pl.pallas_call(
        _mlstm_kernel,
        grid=(b // SCAN_BATCH, 2, n_chunks),
        in_specs=[pl.BlockSpec((SCAN_BATCH, CHUNK, ML_QK), in_map),
                  pl.BlockSpec((SCAN_BATCH, CHUNK, ML_QK), in_map),
                  pl.BlockSpec((SCAN_BATCH, CHUNK, ML_W), in_map),
                  pl.BlockSpec((1, SCAN_BATCH, CHUNK, 2 * ML_HEADS), lambda bi, d, c: (d, bi, idx(d, c), 0)),
                  pl.BlockSpec((1, SCAN_BATCH, 1, 2 * ML_HEADS, CHUNK), lambda bi, d, c: (d, bi, idx(d, c), 0, 0))],
        out_specs=pl.BlockSpec((1, SCAN_BATCH, CHUNK, ML_W), lambda bi, d, c: (d, bi, idx(d, c), 0)),
        out_shape=jax.ShapeDtypeStruct((2, b, t, ML_W), F32),
        scratch_shapes=[pltpu.VMEM((SCAN_BATCH, ML_HEADS, ML_DK, 2 * ML_DV), F32),
                        pltpu.VMEM((SCAN_BATCH * ML_HEADS, LANES), F32)],
        compiler_params=pltpu.CompilerParams(
            dimension_semantics=("parallel", "parallel", "arbitrary"), vmem_limit_bytes=VMEM_LIMIT),
        name="mlstm_scan",
    )(q, k, v, g_col, g_row)


def _hgrn_masks():
    L = CHUNK
    stack = np.zeros((2, (1 + 2 * HG_LEVELS) * L, L), np.float32)
    pair = np.zeros((2, HG_LEVELS + 1, L, L), np.float32)
    for d in range(2):
        p = np.arange(L) if d == 0 else L - 1 - np.arange(L)
        pt, ps = p[:, None], p[None, :]
        stack[d, :L] = ps <= pt
        for l in range(HG_LEVELS):
            parent, half = p >> (l + 1), (p >> l) & 1
            split = parent * (2 << l) + (1 << l) - 1
            same = parent[:, None] == parent[None, :]
            e = same & (half[:, None] == 1) & (ps > split[:, None]) & (ps <= pt)
            f = same & (half[:, None] == 0) & (ps > pt) & (ps <= split[:, None])
            stack[d, (1 + 2 * l) * L:(2 + 2 * l) * L] = e
            stack[d, (2 + 2 * l) * L:(3 + 2 * l) * L] = f
            pair[d, l] = same & (half[:, None] == 1) & (half[None, :] == 0)
        pair[d, HG_LEVELS] = np.eye(L)
    return stack, pair


def _hgrn_kernel(q_ref, hf_ref, v_ref, lb_ref, stack_ref, pair_ref, o_ref, s_ref):
    c = pl.program_id(2)

    @pl.when(c == 0)
    def _():
        s_ref[...] = jnp.zeros_like(s_ref)

    L = CHUNK
    lb = lb_ref[...]
    f = lb + (1.0 - lb) * jax.nn.sigmoid(jnp.concatenate([hf_ref[bi] for bi in range(SCAN_BATCH)], axis=1))
    kk = 1.0 - f
    g = jnp.log(f)
    sums = _mask_dot(stack_ref[0], g)
    chains = [(bi, h) for bi in range(SCAN_BATCH) for h in range(HG_HEADS)]
    each = lambda fn, *lists: [fn(*xs) for xs in zip(*lists)]
    hs = [slice(bi * HG_K + h * HG_DK, bi * HG_K + (h + 1) * HG_DK) for bi, h in chains]
    qh = [q_ref[bi, :, h * HG_DK:(h + 1) * HG_DK] for bi, h in chains]
    kh = [kk[:, s_] for s_ in hs]
    vh = [v_ref[bi, :, h * HG_DV:(h + 1) * HG_DV].astype(BF16) for bi, h in chains]
    b = [sums[0:L, s_] for s_ in hs]
    att = each(lambda q_, k_: pair_ref[0, HG_LEVELS] * _dot_nt(q_.astype(BF16), k_.astype(BF16)), qh, kh)
    for l in range(HG_LEVELS):
        e = [sums[(1 + 2 * l) * L:(2 + 2 * l) * L, s_] for s_ in hs]
        fl = [sums[(2 + 2 * l) * L:(3 + 2 * l) * L, s_] for s_ in hs]
        att = each(lambda a_, q_, k_, e_, f_: a_ + pair_ref[0, l] * _dot_nt(
            (q_ * jnp.exp(e_)).astype(BF16), (k_ * jnp.exp(f_)).astype(BF16)), att, qh, kh, e, fl)
    b_last = [jnp.sum(g[:, s_], axis=0, keepdims=True) for s_ in hs]
    st = [s_ref[bi, h] for bi, h in chains]
    o = each(lambda q_, b_, s_, a_, v_: _dot_nt((q_ * jnp.exp(b_)).astype(BF16), s_.astype(BF16))
             + _dot(a_.astype(BF16), v_), qh, b, st, att, vh)
    for (bi, h), o_ in zip(chains, o):
        o_ref[0, bi, :, h * HG_DV:(h + 1) * HG_DV] = o_
    new_st = each(lambda s_, bl, v_, k_, b_: s_ * jnp.exp(bl) + _dot_tn(v_, (k_ * jnp.exp(bl - b_)).astype(BF16)),
                  st, b_last, vh, kh, b)
    for (bi, h), s_ in zip(chains, new_st):
        s_ref[bi, h] = s_


def hgrn_scan(q, hf, v, lb, n_ctx_chunks):
    b, t, _ = q.shape
    n_chunks = t // CHUNK
    stack, pair = _hgrn_masks()
    idx = functools.partial(_chunk_index, n_ctx_chunks=n_ctx_chunks, n_chunks=n_chunks)
    in_map = lambda bi, d, c: (bi, idx(d, c), 0)
    return pl.pallas_call(
        _hgrn_kernel,
        grid=(b // SCAN_BATCH, 2, n_chunks),
        in_specs=[pl.BlockSpec((SCAN_BATCH, CHUNK, HG_K), in_map),
                  pl.BlockSpec((SCAN_BATCH, CHUNK, HG_K), lambda bi, d, c: (bi, idx(d, c), d)),
                  pl.BlockSpec((SCAN_BATCH, CHUNK, HG_W), in_map),
                  pl.BlockSpec((1, SCAN_BATCH * HG_K), lambda bi, d, c: (0, 0)),
                  pl.BlockSpec((1,) + stack.shape[1:], lambda bi, d, c: (d, 0, 0)),
                  pl.BlockSpec((1,) + pair.shape[1:], lambda bi, d, c: (d, 0, 0, 0))],
        out_specs=pl.BlockSpec((1, SCAN_BATCH, CHUNK, HG_W), lambda bi, d, c: (d, bi, idx(d, c), 0)),
        out_shape=jax.ShapeDtypeStruct((2, b, t, HG_W), F32),
        scratch_shapes=[pltpu.VMEM((SCAN_BATCH, HG_HEADS, HG_DV, HG_DK), F32)],
        compiler_params=pltpu.CompilerParams(
            dimension_semantics=("parallel", "parallel", "arbitrary"), vmem_limit_bytes=VMEM_LIMIT),
        name="hgrn_scan",
    )(q, hf, v, jnp.tile(lb, (1, SCAN_BATCH)), jnp.asarray(stack, BF16), jnp.asarray(pair, F32))


def _rwkv_masks():
    L, G = CHUNK, RW_GROUP
    n = G * L
    head = np.arange(n) // L
    bd = (head[:, None] == head[None, :]).astype(np.float32)
    cum = np.zeros((2, L, L), np.float32)
    n_lvl = int(np.log2(L // RW_BASE))
    wide = np.zeros((2, 4 + n_lvl, L, n), np.float32)
    for d in range(2):
        p = np.arange(L) if d == 0 else L - 1 - np.arange(L)
        pt, ps = p[:, None], p[None, :]
        cum[d] = ps <= pt
        m = [ps < pt, ps <= pt, np.eye(L, dtype=bool), (pt // RW_BASE) == (ps // RW_BASE)]
        for i in range(n_lvl):
            blk = RW_BASE << i
            parent, half = p // (2 * blk), (p // blk) % 2
            m.append((parent[:, None] == parent[None, :]) & (half[:, None] == 1) & (half[None, :] == 0))
        for i, mi in enumerate(m):
            wide[d, i] = np.tile(mi, (1, G))
    return bd, cum, wide


def _unit_lower_inverse(n_w, wide_ref, expand):
    mul = lambda a_w, b_w: [_dot(x.astype(BF16), expand(y)) for x, y in zip(a_w, b_w)]
    nd = [n * wide_ref[0, 3] for n in n_w]
    t_inv = [wide_ref[0, 2] - x for x in nd]
    pw = nd
    for _ in range(int(np.log2(RW_BASE)) - 1):
        pw = mul(pw, pw)
        t_inv = [t + x for t, x in zip(t_inv, mul(t_inv, pw))]
    for i in range(wide_ref.shape[1] - 4):
        tc = mul(t_inv, [n * wide_ref[0, 4 + i] for n in n_w])
        t_inv = [t - x for t, x in zip(t_inv, mul(tc, t_inv))]
    return t_inv


def _rwkv_kernel(r_ref, k_ref, v_ref, kk_ref, a_ref, lw_ref, bd_ref, cum_ref, wide_ref, o_ref, s_ref):
    c = pl.program_id(2)

    @pl.when(c == 0)
    def _():
        s_ref[...] = jnp.zeros_like(s_ref)

    G = RW_GROUP
    bd = bd_ref[...]
    strict = wide_ref[0, 0]
    incl = wide_ref[0, 1]
    cum = cum_ref[0]

    def expand(x):
        return (jnp.concatenate([x] * G, axis=0) * bd).astype(BF16)

    L = CHUNK
    chains = [(bi, g) for bi in range(SCAN_BATCH) for g in range(RW_HEADS // G)]
    cols = [slice(g * RW_GW, (g + 1) * RW_GW) for _, g in chains]
    each = lambda fn, *lists: [fn(*xs) for xs in zip(*lists)]
    r = [r_ref[bi, :, cs_] for (bi, _), cs_ in zip(chains, cols)]
    k = [k_ref[0, bi, :, cs_] for (bi, _), cs_ in zip(chains, cols)]
    v = [v_ref[bi, :, cs_] for (bi, _), cs_ in zip(chains, cols)]
    kk = [kk_ref[bi, :, cs_] for (bi, _), cs_ in zip(chains, cols)]
    lw = [lw_ref[0, bi, :, cs_] for (bi, _), cs_ in zip(chains, cols)]
    kb = [x * a_ref[0, bi, :, cs_] for x, (bi, _), cs_ in zip(kk, chains, cols)]
    cs = each(lambda x: _mask_dot(cum, x), lw)
    c_last = each(lambda x: jnp.sum(x, axis=0, keepdims=True), lw)
    p_inv = each(lambda x: jnp.exp(-x), cs)
    p_to_end = each(lambda cl, x: jnp.exp(cl - x), c_last, cs)
    kkd_rp = each(lambda kk_, r_, cs_, lw_: jnp.concatenate(
        [kk_ * jnp.exp(cs_ - lw_), r_ * jnp.exp(cs_)], axis=0).astype(BF16), kk, r, cs, lw)
    v_bd = each(expand, v)
    sc_k = each(lambda q_, k_, p_: _dot_nt(q_, expand(k_ * p_)), kkd_rp, k, p_inv)
    sc_b = each(lambda q_, b_, p_: _dot_nt(q_, expand(b_ * p_)), kkd_rp, kb, p_inv)
    n_w = each(lambda x: strict * x[:L], sc_b)
    m_w = each(lambda x: jnp.concatenate([strict * x[:L], incl * x[L:]], axis=0).astype(BF16), sc_k)
    mrb_w = each(lambda x: (incl * x[L:]).astype(BF16), sc_b)
    st = [s_ref[bi, g] for bi, g in chains]
    from_state = each(lambda q_, s_, m_, vb_: _dot_nt(q_, s_.astype(BF16)) + _dot(m_, vb_),
                      kkd_rp, st, m_w, v_bd)
    t_inv = _unit_lower_inverse(n_w, wide_ref, expand)
    u = each(lambda t_, f_: _dot(t_.astype(BF16), expand(f_[:L])), t_inv, from_state)
    y = each(lambda f_, m_, u_: f_[L:] - _dot(m_, expand(u_)), from_state, mrb_w, u)
    for (bi, g), cs_, y_ in zip(chains, cols, y):
        o_ref[0, bi, :, cs_] = y_
    new_st = each(lambda s_, cl, v_, u_, k_, b_, p_: s_ * jnp.exp(cl) + bd * _dot_tn(
        jnp.concatenate([v_, -u_], axis=0).astype(BF16),
        jnp.concatenate([k_ * p_, b_ * p_], axis=0).astype(BF16)), st, c_last, v, u, k, kb, p_to_end)
    for (bi, g), s_ in zip(chains, new_st):
        s_ref[bi, g] = s_


def rwkv_scan(r, k_eff, v, kk, a, lw, n_ctx_chunks):
    b, t, _ = r.shape
    n_chunks = t // CHUNK
    assert b % SCAN_BATCH == 0
    bd, cum, wide = _rwkv_masks()
    idx = functools.partial(_chunk_index, n_ctx_chunks=n_ctx_chunks, n_chunks=n_chunks)
    shared = pl.BlockSpec((SCAN_BATCH, CHUNK, RW_W), lambda bi, d, c: (bi, idx(d, c), 0))
    per_dir = pl.BlockSpec((1, SCAN_BATCH, CHUNK, RW_W), lambda bi, d, c: (d, bi, idx(d, c), 0))
    n = RW_GROUP * CHUNK
    return pl.pallas_call(
        _rwkv_kernel,
        grid=(b // SCAN_BATCH, 2, n_chunks),
        in_specs=[shared, per_dir, shared, shared, per_dir, per_dir,
                  pl.BlockSpec((n, RW_GW), lambda bi, d, c: (0, 0)),
                  pl.BlockSpec((1, CHUNK, CHUNK), lambda bi, d, c: (d, 0, 0)),
                  pl.BlockSpec((1,) + wide.shape[1:], lambda bi, d, c: (d, 0, 0, 0))],
        out_specs=per_dir,
        out_shape=jax.ShapeDtypeStruct((2, b, t, RW_W), F32),
        scratch_shapes=[pltpu.VMEM((SCAN_BATCH, RW_HEADS // RW_GROUP, RW_GW, RW_GW), F32)],
        compiler_params=pltpu.CompilerParams(
            dimension_semantics=("parallel", "parallel", "arbitrary"), vmem_limit_bytes=VMEM_LIMIT),
        name="rwkv7_scan",
    )(r, k_eff, v, kk, a, lw, jnp.asarray(bd, F32), jnp.asarray(cum, BF16), jnp.asarray(wide, F32))


def _moe_kernel(e_ref, x_ref, wgu_ref, bgu_ref, wdn_ref, bdn_ref, o_ref, wgu16_ref, wdn16_ref):
    i = pl.program_id(0)
    new_expert = jnp.logical_or(i == 0, e_ref[i] != e_ref[jnp.maximum(i - 1, 0)])

    @pl.when(new_expert)
    def _():
        wgu16_ref[...] = wgu_ref[0, 0].astype(BF16)
        wdn16_ref[...] = wdn_ref[0, 0].astype(BF16)

    d_ff = wdn16_ref.shape[0]
    gu = _dot(x_ref[...].astype(BF16), wgu16_ref[...]) + bgu_ref[0, 0]
    glu = jnp.minimum(gu[:, :d_ff], SWIGLU_LIMIT)
    lin = jnp.clip(gu[:, d_ff:], -SWIGLU_LIMIT, SWIGLU_LIMIT)
    act = glu * jax.nn.sigmoid(SWIGLU_ALPHA * glu) * (lin + 1.0)
    o_ref[...] = _dot(act.astype(BF16), wdn16_ref[...]) + bdn_ref[0, 0]


def moe_experts(xb, blk_e, layer, w_gu, b_gu, w_dn, b_dn):
    n_rows, dm = xb.shape
    d_ff = w_dn.shape[2]
    n_blocks = n_rows // MOE_ROWS
    return pl.pallas_call(
        _moe_kernel,
        grid_spec=pltpu.PrefetchScalarGridSpec(
            num_scalar_prefetch=1,
            grid=(n_blocks,),
            in_specs=[pl.BlockSpec((MOE_ROWS, dm), lambda i, e: (i, 0)),
                      pl.BlockSpec((1, 1, dm, 2 * d_ff), lambda i, e: (layer, e[i], 0, 0)),
                      pl.BlockSpec((1, 1, 1, 2 * d_ff), lambda i, e: (layer, e[i], 0, 0)),
                      pl.BlockSpec((1, 1, d_ff, dm), lambda i, e: (layer, e[i], 0, 0)),
                      pl.BlockSpec((1, 1, 1, dm), lambda i, e: (layer, e[i], 0, 0))],
            out_specs=pl.BlockSpec((MOE_ROWS, dm), lambda i, e: (i, 0)),
            scratch_shapes=[pltpu.VMEM((dm, 2 * d_ff), BF16), pltpu.VMEM((d_ff, dm), BF16)]),
        out_shape=jax.ShapeDtypeStruct((n_rows, dm), F32),
        compiler_params=pltpu.CompilerParams(dimension_semantics=("arbitrary",),
                                             vmem_limit_bytes=VMEM_LIMIT),
        name="moe_experts",
    )(blk_e, xb, w_gu, b_gu, w_dn, b_dn)


def moe_ffn(t, router_w, router_b, layer, w_gu, b_gu, w_dn, b_dn):
    n, dm = t.shape
    logits = jnp.dot(t, router_w, precision=lax.Precision.HIGHEST) + router_b
    expert_id = lax.broadcasted_iota(jnp.int32, logits.shape, 1)
    rest, top_logit, top_e = logits, [], []
    for _ in range(TOP_K):
        best = jnp.max(rest, axis=-1, keepdims=True)
        pick = jnp.min(jnp.where(rest == best, expert_id, N_EXPERTS), axis=-1, keepdims=True)
        top_logit.append(best)
        top_e.append(pick)
        rest = jnp.where(expert_id == pick, -jnp.inf, rest)
    top_logit, top_e = jnp.concatenate(top_logit, axis=-1), jnp.concatenate(top_e, axis=-1)
    top_w = jax.nn.softmax(top_logit, axis=-1)
    flat_e = top_e.reshape(-1).astype(jnp.int32)
    n_asg = n * TOP_K
    seg = 128
    assert n_asg % seg == 0
    onehot = (flat_e.reshape(n_asg // seg, seg, 1) == jnp.arange(N_EXPERTS, dtype=jnp.int32)).astype(F32)
    within = jnp.einsum('ts,bse->bte', jnp.tril(jnp.ones((seg, seg), F32)), onehot)
    seg_tot = within[:, -1, :]
    before = jnp.cumsum(seg_tot, axis=0) - seg_tot
    rank = (jnp.sum(onehot * (within + before[:, None, :]), axis=-1) - 1.0).astype(jnp.int32).reshape(-1)
    counts = jnp.sum(seg_tot, axis=0).astype(jnp.int32)
    padded = (counts + MOE_ROWS - 1) // MOE_ROWS * MOE_ROWS
    pad_end = jnp.cumsum(padded)
    pad_start = pad_end - padded
    grp_start = jnp.cumsum(counts) - counts
    n_blocks = -(-n_asg // MOE_ROWS) + N_EXPERTS
    n_rows = n_blocks * MOE_ROWS
    blk_start = jnp.arange(n_blocks, dtype=jnp.int32) * MOE_ROWS
    blk_e = jnp.minimum(jnp.sum((pad_end[None, :] <= blk_start[:, None]).astype(jnp.int32), axis=1),
                        N_EXPERTS - 1)
    order = jnp.argsort(flat_e)
    e_row = jnp.repeat(blk_e, MOE_ROWS)
    j = jnp.arange(n_rows, dtype=jnp.int32) - pad_start[e_row]
    used = j < counts[e_row]
    row_tok = jnp.where(used, order[jnp.where(used, grp_start[e_row] + j, 0)] // TOP_K, 0)
    xb = t[row_tok]
    yb = moe_experts(xb, blk_e, layer, w_gu, b_gu[:, :, None, :], w_dn, b_dn[:, :, None, :])
    slot = (pad_start[flat_e] + rank).reshape(n, TOP_K)
    y = jnp.zeros((n, dm), F32)
    for j in range(TOP_K):
        y = y + yb[slot[:, j]] * top_w[:, j:j + 1]
    return y


def rms_norm(x, g):
    return x * lax.rsqrt(jnp.mean(x * x, axis=-1, keepdims=True) + EPS) * g


def head_rms_norm(o, g):
    return o * lax.rsqrt(jnp.mean(o * o, axis=-1, keepdims=True) + EPS) * g


def dwconv3(x, w, n_ctx):
    t = x.shape[1]
    xp = jnp.pad(x, ((0, 0), (1, 1), (0, 0)))
    pos = jnp.arange(t)[None, :, None]
    has_prev = (pos != 0) & (pos != n_ctx)
    has_next = (pos != n_ctx - 1) & (pos != t - 1)
    return (w[0] * jnp.where(has_prev, xp[:, :-2], 0.0) + w[1] * x
            + w[2] * jnp.where(has_next, xp[:, 2:], 0.0))


def rope_tables(n_ctx, seq):
    rows = seq // GRID_W
    pos_r = jnp.repeat(jnp.arange(rows, dtype=F32), GRID_W)
    pos_c = jnp.tile(jnp.arange(GRID_W, dtype=F32), rows)
    nf = RET_DK // 4
    inv = ROPE_BASE ** (-jnp.arange(nf, dtype=F32) / nf)
    ang = jnp.concatenate([pos_r[:, None] * inv, pos_c[:, None] * inv], axis=-1)
    cos = jnp.concatenate([jnp.ones((n_ctx, RET_DK // 2), F32), jnp.cos(ang)], axis=0)
    sin = jnp.concatenate([jnp.zeros((n_ctx, RET_DK // 2), F32), jnp.sin(ang)], axis=0)
    return cos, sin


def apply_rope(x, cos, sin):
    half = x.shape[-1] // 2
    x1, x2 = x[..., :half], x[..., half:]
    c, s = cos[None, :, None, :], sin[None, :, None, :]
    return jnp.concatenate([x1 * c - x2 * s, x1 * s + x2 * c], axis=-1)


def project(u, w):
    b, t, dm = u.shape
    m = w.shape[1]
    m_pad = -(-m // LANES) * LANES
    w16 = jnp.pad(w, ((0, 0), (0, m_pad - m))).astype(BF16)
    z = matmul(u.reshape(b * t, dm).astype(BF16), w16)
    return z[:, :m].reshape(b, t, m)


def mixer_ab(u, n_ctx, rope, w_in, w_out, ret_decay, ret_norm, ml_conv, ml_gate_b, ml_norm):
    b, t, _ = u.shape
    n_ctx_chunks = n_ctx // CHUNK
    rq, rk, rv, rg, mqk, mv, mo, mg = _split(project(u, w_in), AB_SPLITS)
    rq = apply_rope(rq.reshape(b, t, RET_HEADS, RET_DK), *rope).reshape(b, t, RET_QK)
    rk = apply_rope(rk.reshape(b, t, RET_HEADS, RET_DK) * RET_DK ** -0.5, *rope).reshape(b, t, RET_QK)
    log_gamma = jnp.log1p(-jnp.exp(ret_decay))
    ret = retention_scan(rq, rk, rv, log_gamma, n_ctx_chunks)
    ret = ret[0] + ret[1]

    mq, mk = _split(jax.nn.silu(dwconv3(mqk, ml_conv, n_ctx)), (ML_QK, ML_QK))
    mq = mq * ML_DK ** -0.5
    gates = (mg + ml_gate_b).reshape(b, t, 2, 2, ML_HEADS)
    gates = jnp.stack([gates[:, :, :, 0], jax.nn.log_sigmoid(gates[:, :, :, 1])], axis=3)
    g_col = jnp.moveaxis(gates, 2, 0).reshape(2, b, t, 2 * ML_HEADS)
    g_row = jnp.swapaxes(g_col.reshape(2, b, t // CHUNK, CHUNK, 2 * ML_HEADS), 3, 4)
    ml = mlstm_scan(mq, mk, mv, g_col, g_row, n_ctx_chunks)
    ml = ml[0] + ml[1]

    ret = head_rms_norm(ret.reshape(b, t, RET_HEADS, RET_DV), ret_norm).reshape(b, t, RET_W) * jax.nn.silu(rg)
    ml = head_rms_norm(ml.reshape(b, t, ML_HEADS, ML_DV), ml_norm).reshape(b, t, ML_W) * jax.nn.sigmoid(mo)
    return project(jnp.concatenate([ret, ml], axis=-1), w_out)


def mixer_cd(u, n_ctx, w_in, w_out, lb, hg_norm, rw_shift, rw_w0, rw_w2, rw_a0, rw_a2, rw_g2,
             rw_kk_scale, rw_k_a, rw_r_k, rw_norm):
    b, t, _ = u.shape
    n_ctx_chunks = n_ctx // CHUNK
    hq, hf, hi, hg, zr = _split(project(u, w_in), CD_SPLITS)
    hgo = hgrn_scan(hq, hf, hi, lb.reshape(1, HG_K), n_ctx_chunks)
    hgo = hgo[0] + hgo[1]

    r, k, v, wl, al, gl = _split(dwconv3(zr, rw_shift, n_ctx), RWKV_SPLITS)
    kk = (k * rw_kk_scale).reshape(b, t, RW_HEADS, RW_HD)
    kk = (kk * lax.rsqrt(jnp.sum(kk * kk, axis=-1, keepdims=True) + EPS)).reshape(b, t, RW_W)
    gate = jax.nn.sigmoid(gl) @ rw_g2
    k_eff, a_all, lw_all = [], [], []
    for d in range(2):
        wl_d = wl[..., d * RW_W_RANK:(d + 1) * RW_W_RANK]
        al_d = al[..., d * RW_A_RANK:(d + 1) * RW_A_RANK]
        w_log = -jax.nn.softplus(-(rw_w0[d] + jnp.tanh(wl_d) @ rw_w2[d])) - 0.5
        lw_all.append(-jnp.exp(w_log))
        a = jax.nn.sigmoid(rw_a0[d] + al_d @ rw_a2[d])
        a_all.append(a)
        k_eff.append(k * (1.0 + (a - 1.0) * rw_k_a))
    k_eff, a_all, lw_all = jnp.stack(k_eff), jnp.stack(a_all), jnp.stack(lw_all)
    rwo = rwkv_scan(r, k_eff, v, kk, a_all, lw_all, n_ctx_chunks)
    rwo = (rwo[0] + rwo[1]).reshape(b, t, RW_HEADS, RW_HD)

    r4, v4 = r.reshape(b, t, RW_HEADS, RW_HD), v.reshape(b, t, RW_HEADS, RW_HD)
    bonus = sum(jnp.sum(r4 * k_eff[d].reshape(b, t, RW_HEADS, RW_HD) * rw_r_k, axis=-1, keepdims=True) * v4
                for d in range(2))
    hgo = head_rms_norm(hgo.reshape(b, t, HG_HEADS, HG_DV), hg_norm).reshape(b, t, HG_W) * jax.nn.sigmoid(hg)
    rwo = (head_rms_norm(rwo, rw_norm) + bonus).reshape(b, t, RW_W) * gate
    return project(jnp.concatenate([hgo, rwo], axis=-1), w_out)


def hgrn_lower_bound(p, layer):
    sm = jax.nn.softmax(p, axis=0)
    return jnp.cumsum(sm, axis=0)[layer] - sm[0]


def kernel(x, c, ctx, c_ctx, mod_w, mod_b, norm_mix, norm_ffn, norm_final, ab_w_in, ab_w_out, ret_decay, ret_norm, mlstm_conv, mlstm_gate_b, mlstm_norm, cd_w_in, cd_w_out, hgrn_lb, hgrn_norm, rwkv_shift, rwkv_w0, rwkv_w2, rwkv_a0, rwkv_a2, rwkv_g2, rwkv_kk_scale, rwkv_k_a, rwkv_r_k, rwkv_norm, router_w, router_b, exp_w_gate_up, exp_b_gate_up, exp_w_down, exp_b_down):
    bsz, seq, dm = x.shape
    n_ctx = ctx.shape[1]
    depth = mod_w.shape[0]
    t = n_ctx + seq
    assert n_ctx % CHUNK == 0 and seq % CHUNK == 0 and seq % GRID_W == 0
    rope = rope_tables(n_ctx, seq)
    cond_l = jax.nn.silu(c)
    cond_c = jax.nn.silu(c_ctx)[None, :]
    h = jnp.concatenate([ctx, x], axis=1)
    hp = lax.Precision.HIGHEST

    is_ctx = (jnp.arange(t) < n_ctx)[None, :, None]

    def per_token(m_c, m_l):
        return jnp.where(is_ctx, m_c[:, None, :], m_l[:, None, :])

    for i in range(depth):
        j = i // 2
        m_l = jnp.split(jnp.dot(cond_l, mod_w[i], precision=hp) + mod_b[i], 6, axis=-1)
        m_c = jnp.split(jnp.dot(cond_c, mod_w[i], precision=hp) + mod_b[i], 6, axis=-1)
        m = [per_token(a, b_) for a, b_ in zip(m_c, m_l)]
        u = rms_norm(h, norm_mix[i]) * (1.0 + m[1]) + m[0]
        if i % 2 == 0:
            y = mixer_ab(u, n_ctx, rope, ab_w_in[j], ab_w_out[j], ret_decay[j], ret_norm[j],
                         mlstm_conv[j], mlstm_gate_b[j], mlstm_norm[j])
        else:
            y = mixer_cd(u, n_ctx, cd_w_in[j], cd_w_out[j], hgrn_lower_bound(hgrn_lb, i), hgrn_norm[j],
                         rwkv_shift[j], rwkv_w0[j], rwkv_w2[j], rwkv_a0[j], rwkv_a2[j], rwkv_g2[j],
                         rwkv_kk_scale[j], rwkv_k_a[j], rwkv_r_k[j], rwkv_norm[j])
        h = h + m[2] * y
        vv = rms_norm(h, norm_ffn[i]) * (1.0 + m[4]) + m[3]
        f = moe_ffn(vv.reshape(-1, dm), router_w[i], router_b[i], i, exp_w_gate_up, exp_b_gate_up,
                    exp_w_down, exp_b_down).reshape(vv.shape)
        h = h + m[5] * f
    return rms_norm(h[:, n_ctx:], norm_final)
```
